```python
import jax
import jax.numpy as jnp
from jax import lax
import numpy as np

D_MODEL = 2048
BATCH = 2
SEQ = 4096
DEPTH = 2

CTX_LEN = 256
GRID_W = 64

BRANCH_W = D_MODEL // 2
N_BRANCH = 3
ML_HEADS = 4
ML_DV = BRANCH_W // ML_HEADS
ML_DQK = ML_DV // 2
ML_CHUNK = 64
RW_DH = 64
RW_HEADS = BRANCH_W // RW_DH
RW_W = RW_HEADS * RW_DH
RW_DECAY_LORA = 96
RW_ICLR_LORA = 96
RW_GATE_LORA = 256
RW_GN_EPS = 64e-5
NA_DH = 128
NA_HEADS = BRANCH_W // NA_DH
NA_WIN_R = 8
NA_WIN_C = 16
D_FF = 4 * D_MODEL
ROPE_BASE = 10000.0
EPS = 1e-6

ML_SIZES = (ML_HEADS * ML_DQK, ML_HEADS * ML_DQK, ML_HEADS * ML_DV, ML_HEADS * ML_DV, 4 * ML_HEADS)
RW_SIZES = (RW_W, RW_W, RW_W, RW_DECAY_LORA, RW_DECAY_LORA, RW_ICLR_LORA, RW_ICLR_LORA, RW_GATE_LORA)
NA_SIZES = (NA_HEADS * NA_DH, NA_HEADS * NA_DH, NA_HEADS * NA_DH)
GROUP_SIZES = (sum(ML_SIZES), sum(RW_SIZES), sum(NA_SIZES), N_BRANCH * D_MODEL)
D_IN = sum(GROUP_SIZES)

kernel_name = 'hybrid_mlstm_rwkv7_natten_dit'


def split_last(t, sizes):
    out, o = [], 0
    for s in sizes:
        out.append(t[..., o:o + s])
        o += s
    return out


def rms_norm(x, g):
    xf = x.astype(jnp.float32)
    y = xf * lax.rsqrt(jnp.mean(xf * xf, axis=-1, keepdims=True) + EPS)
    return (y * g).astype(x.dtype)


def modulate(h, shift, scale):
    return (h * (1 + scale) + shift).astype(h.dtype)


def sq_relu_mlp(h, w1, w2):
    return jnp.square(jax.nn.relu(h @ w1)) @ w2


def rope_1d(x, pos):
    nf = x.shape[-1] // 2
    freqs = ROPE_BASE ** (-jnp.arange(nf, dtype=jnp.float32) / nf)
    ang = pos.astype(jnp.float32)[:, None] * freqs[None, :]
    cos, sin = jnp.cos(ang)[:, None, :], jnp.sin(ang)[:, None, :]
    x1, x2 = x[..., :nf].astype(jnp.float32), x[..., nf:].astype(jnp.float32)
    return jnp.concatenate([x1 * cos - x2 * sin, x1 * sin + x2 * cos], axis=-1).astype(x.dtype)


def axial_rope(x):
    t = jnp.arange(x.shape[1])
    half = x.shape[-1] // 2
    return jnp.concatenate([rope_1d(x[..., :half], t // GRID_W), rope_1d(x[..., half:], t % GRID_W)], axis=-1)


def centred_shift(p, mu):
    prev = jnp.pad(p[:, :-1], ((0, 0), (1, 0), (0, 0)))
    nxt = jnp.pad(p[:, 1:], ((0, 0), (0, 1), (0, 0)))
    return p + mu * (0.5 * (prev + nxt) - p)


def mlstm_chunkwise(q, k, v, ig, lf, state, with_output):
    B, H, T, dk = k.shape
    dv = v.shape[-1]
    L = ML_CHUNK
    nc = T // L
    kc = k.astype(jnp.float32).reshape(B, H, nc, L, dk)
    vc = v.astype(jnp.float32).reshape(B, H, nc, L, dv)
    igc = ig.reshape(B, H, nc, L)
    b = jnp.cumsum(lf.reshape(B, H, nc, L), axis=-1)
    b_end = b[..., -1]
    a = b_end[..., None] - b + igc
    a_max = jnp.max(a, axis=-1)
    wgt = jnp.exp(a - a_max[..., None])
    C_loc = jnp.einsum('bhclv,bhclk->bhcvk', vc * wgt[..., None], kc)
    n_loc = jnp.einsum('bhcl,bhclk->bhck', wgt, kc)

    def step(carry, inp):
        C, n, m = carry
        Cl, nl, ge, am = inp
        m_new = jnp.maximum(ge + m, am)
        so, sl = jnp.exp(ge + m - m_new), jnp.exp(am - m_new)
        new = (so[..., None, None] * C + sl[..., None, None] * Cl, so[..., None] * n + sl[..., None] * nl, m_new)
        return new, (C, n, m)

    xs = tuple(jnp.moveaxis(t, 2, 0) for t in (C_loc, n_loc, b_end, a_max))
    final, inits = lax.scan(step, state, xs)
    if not with_output:
        return None, final
    C0, n0, m0 = (jnp.moveaxis(t, 0, 2) for t in inits)
    qc = q.astype(jnp.float32).reshape(B, H, nc, L, dk)
    lower = jnp.tril(jnp.ones((L, L), dtype=bool))
    log_intra = jnp.where(lower, b[..., :, None] - b[..., None, :] + igc[..., None, :], -jnp.inf)
    log_inter = b + m0[..., None]
    m_j = jnp.maximum(log_inter, jnp.max(log_intra, axis=-1))
    s = jnp.einsum('bhcjd,bhcsd->bhcjs', qc, kc) * jnp.exp(log_intra - m_j[..., None])
    w_inter = jnp.exp(log_inter - m_j)
    num = jnp.einsum('bhcjs,bhcsv->bhcjv', s, vc) + w_inter[..., None] * jnp.einsum('bhcvd,bhcjd->bhcjv', C0, qc)
    den = jnp.sum(s, axis=-1) + w_inter * jnp.einsum('bhcd,bhcjd->bhcj', n0, qc)
    h = num / jnp.maximum(jnp.abs(den), jnp.exp(-m_j))[..., None]
    return h.reshape(B, H, T, dv), final


def mlstm_branch(pc, pl, gate_b, norm_g, with_ctx):
    qc, kc, vc, oc, gc = split_last(pc, ML_SIZES)
    ql, kl, vl, ol, gl = split_last(pl, ML_SIZES)
    B = pl.shape[0]
    scale = ML_DQK ** -0.5

    def heads(t, d, rope):
        t = t.reshape(t.shape[0], t.shape[1], ML_HEADS, d)
        if rope:
            t = axial_rope(t)
        return jnp.swapaxes(t, 1, 2)

    def gates(g):
        g = (g + gate_b).astype(jnp.float32).reshape(g.shape[0], g.shape[1], 4, ML_HEADS)
        g = jnp.moveaxis(g, 1, 3)
        return (g[:, 0], jax.nn.log_sigmoid(g[:, 1])), (g[:, 2], jax.nn.log_sigmoid(g[:, 3]))

    fl = lambda t: jnp.flip(t, axis=2)
    q_c = heads(qc * scale, ML_DQK, False) if with_ctx else None
    k_c, v_c = heads(kc, ML_DQK, False), heads(vc, ML_DV, False)
    q_l, k_l, v_l = heads(ql * scale, ML_DQK, True), heads(kl, ML_DQK, True), heads(vl, ML_DV, False)
    (ic_f, lc_f), (ic_b, lc_b) = gates(gc)
    (il_f, ll_f), (il_b, ll_b) = gates(gl)
    zero = (jnp.zeros((B, ML_HEADS, ML_DV, ML_DQK), jnp.float32),
            jnp.zeros((B, ML_HEADS, ML_DQK), jnp.float32),
            jnp.zeros((B, ML_HEADS), jnp.float32))
    h_cf, s_f = mlstm_chunkwise(q_c, k_c, v_c, ic_f, lc_f, zero, with_ctx)
    h_lf, _ = mlstm_chunkwise(q_l, k_l, v_l, il_f, ll_f, s_f, True)
    h_cb, s_b = mlstm_chunkwise(fl(q_c) if with_ctx else None, fl(k_c), fl(v_c), fl(ic_b), fl(lc_b), zero, with_ctx)
    h_lb, _ = mlstm_chunkwise(fl(q_l), fl(k_l), fl(v_l), fl(il_b), fl(ll_b), s_b, True)

    def post(h, o):
        Bh, _, T, _ = h.shape
        h = h * lax.rsqrt(jnp.mean(h * h, axis=-1, keepdims=True) + EPS)
        h = jnp.swapaxes(h, 1, 2).reshape(Bh, T, BRANCH_W) * norm_g
        return (h * jax.nn.sigmoid(o.astype(jnp.float32))).astype(o.dtype)

    y_l = post(h_lf + fl(h_lb), ol)
    y_c = post(h_cf + fl(h_cb), oc) if with_ctx else None
    return y_c, y_l


def rwkv7_scan(r, dec, k, v, a, b, S0):
    def step(S, inp):
        r_t, d_t, k_t, v_t, a_t, b_t = inp
        sa = jnp.einsum('bhvk,bhk->bhv', S, a_t)
        S = S * d_t[:, :, None, :] + sa[..., :, None] * b_t[..., None, :] + v_t[..., :, None] * k_t[..., None, :]
        return S, jnp.einsum('bhvk,bhk->bhv', S, r_t)
    xs = tuple(jnp.moveaxis(t.astype(jnp.float32), 1, 0) for t in (r, dec, k, v, a, b))
    S, ys = lax.scan(step, S0, xs)
    return jnp.moveaxis(ys, 0, 1), S


def rwkv_prep(p, prm):
    p = centred_shift(p, prm['rw_mu'])
    r, k, v, wfd, wbd, afd, abd, gd = split_last(p, RW_SIZES)
    B, T = p.shape[:2]
    hs = lambda t: t.reshape(B, T, RW_HEADS, RW_DH)
    dirs = []
    for i, (wd, ad) in enumerate(((wfd, afd), (wbd, abd))):
        w = -jax.nn.softplus(-(prm['rw_w0'][i] + jnp.tanh(wd) @ prm['rw_w_up'][i])) - 0.5
        dec = jnp.exp(-jnp.exp(w.astype(jnp.float32)))
        a = jax.nn.sigmoid((prm['rw_a0'][i] + ad @ prm['rw_a_up'][i]).astype(jnp.float32))
        k_dir = k * (1 + (a - 1) * prm['rw_k_a'])
        dirs.append((hs(dec), hs(k_dir), hs(a)))
    kk = hs((k * prm['rw_k_k']).astype(jnp.float32))
    kk = kk / jnp.maximum(jnp.sqrt(jnp.sum(kk * kk, axis=-1, keepdims=True)), 1e-12)
    bonus = jnp.sum(hs(r * k * prm['rw_r_k']), axis=-1, keepdims=True) * hs(v)
    g = jax.nn.sigmoid(gd) @ prm['rw_g_up']
    return {'r': hs(r), 'v': hs(v), 'kk': kk, 'dirs': dirs, 'bonus': bonus, 'g': g}


def rwkv_post(y, s, prm):
    B, T = y.shape[:2]
    mu = jnp.mean(y, axis=-1, keepdims=True)
    var = jnp.mean(jnp.square(y - mu), axis=-1, keepdims=True)
    y = ((y - mu) * lax.rsqrt(var + RW_GN_EPS)).reshape(B, T, RW_W)
    y = y * prm['rw_ln_w'] + prm['rw_ln_b'] + s['bonus'].reshape(B, T, RW_W)
    return (y * s['g']).astype(s['g'].dtype)


def rwkv_branch(pc, pl, prm, with_ctx):
    s_c, s_l = rwkv_prep(pc, prm), rwkv_prep(pl, prm)
    B = pl.shape[0]
    S0 = jnp.zeros((B, RW_HEADS, RW_DH, RW_DH), jnp.float32)
    ys_c, ys_l = [], []
    for d in range(2):
        f = (lambda t: t) if d == 0 else (lambda t: jnp.flip(t, axis=1))

        def run(s, state):
            dec, kd, a = s['dirs'][d]
            y, S = rwkv7_scan(f(s['r']), f(dec), f(kd), f(s['v']), f(-s['kk']), f(s['kk'] * a), state)
            return f(y), S

        y_c, S_ctx = run(s_c, S0)
        y_l, _ = run(s_l, S_ctx)
        ys_c.append(y_c)
        ys_l.append(y_l)
    out_l = rwkv_post(ys_l[0] + ys_l[1], s_l, prm)
    out_c = rwkv_post(ys_c[0] + ys_c[1], s_c, prm) if with_ctx else None
    return out_c, out_l


def na_latent(q, k, v, k_ctx, v_ctx, rpb):
    B, T, H, dh = q.shape
    rows = T // GRID_W
    kr = min(NA_WIN_R, rows)
    r_ix = jnp.arange(rows)
    row_idx = jnp.clip(r_ix - kr // 2, 0, rows - kr)[:, None] + jnp.arange(kr)[None, :]
    c_ix = jnp.arange(GRID_W)
    c_start = jnp.clip(c_ix - NA_WIN_C // 2, 0, GRID_W - NA_WIN_C)
    col_ok = (c_ix[None, :] >= c_start[:, None]) & (c_ix[None, :] < c_start[:, None] + NA_WIN_C)
    row_off = row_idx - r_ix[:, None] + (NA_WIN_R - 1)
    col_off = jnp.clip(c_ix[None, :] - c_ix[:, None], 1 - NA_WIN_C, NA_WIN_C - 1) + (NA_WIN_C - 1)
    bias = rpb[:, row_off[:, None, :, None], col_off[None, :, None, :]]
    qg = q.reshape(B, rows, GRID_W, H, dh)
    kg = k.reshape(B, rows, GRID_W, H, dh)[:, row_idx]
    vg = v.reshape(B, rows, GRID_W, H, dh)[:, row_idx]
    s_win = jnp.einsum('brqhd,brjkhd->bhrqjk', qg, kg).astype(jnp.float32) + bias[None].astype(jnp.float32)
    s_win = jnp.where(col_ok[:, None, :], s_win, -jnp.inf)
    s_ctx = jnp.einsum('brqhd,bmhd->bhrqm', qg, k_ctx).astype(jnp.float32)
    n_win = kr * GRID_W
    p = jax.nn.softmax(jnp.concatenate([s_win.reshape(B, H, rows, GRID_W, n_win), s_ctx], axis=-1), axis=-1)
    p_win = p[..., :n_win].reshape(B, H, rows, GRID_W, kr, GRID_W).astype(v.dtype)
    p_ctx = p[..., n_win:].astype(v.dtype)
    y = jnp.einsum('bhrqjk,brjkhd->brqhd', p_win, vg) + jnp.einsum('bhrqm,bmhd->brqhd', p_ctx, v_ctx)
    return y.reshape(B, T, H * dh)


def na_branch(pc, pl, rpb, with_ctx):
    B = pl.shape[0]
    qc, kc, vc = [t.reshape(B, t.shape[1], NA_HEADS, NA_DH) for t in split_last(pc, NA_SIZES)]
    ql, kl, vl = [t.reshape(B, t.shape[1], NA_HEADS, NA_DH) for t in split_last(pl, NA_SIZES)]
    scale = NA_DH ** -0.5
    y_l = na_latent(ql * scale, kl, vl, kc, vc, rpb).astype(pl.dtype)
    y_c = None
    if with_ctx:
        p = jax.nn.softmax(jnp.einsum('bqhd,bkhd->bhqk', qc * scale, kc).astype(jnp.float32), axis=-1)
        y_c = jnp.einsum('bhqk,bkhd->bqhd', p.astype(vc.dtype), vc).reshape(B, pc.shape[1], BRANCH_W).astype(pc.dtype)
    return y_c, y_l


def merge_branches(ys, gate_pre, w_branch, w_out):
    Y = jnp.stack(ys, axis=-2)
    proj = jnp.einsum('btnw,nwd->btnd', Y, w_branch)
    g = jax.nn.sigmoid(gate_pre).reshape(*gate_pre.shape[:-1], N_BRANCH, D_MODEL)
    return (jnp.sum(g * proj, axis=-2) @ w_out).astype(gate_pre.dtype)


def token_mixer(hc, hl, prm, with_ctx):
    pc = hc @ prm['w_in']
    pl = hl @ prm['w_in']
    ml_c, rw_c, na_c, gt_c = split_last(pc, GROUP_SIZES)
    ml_l, rw_l, na_l, gt_l = split_last(pl, GROUP_SIZES)
    m_c, m_l = mlstm_branch(ml_c, ml_l, prm['ml_gate_b'], prm['ml_norm_g'], with_ctx)
    r_c, r_l = rwkv_branch(rw_c, rw_l, prm, with_ctx)
    n_c, n_l = na_branch(na_c, na_l, prm['na_rpb'], with_ctx)
    y_l = merge_branches([m_l.astype(hl.dtype), r_l.astype(hl.dtype), n_l], gt_l, prm['w_branch'], prm['w_out'])
    y_c = None
    if with_ctx:
        y_c = merge_branches([m_c.astype(hc.dtype), r_c.astype(hc.dtype), n_c], gt_c, prm['w_branch'], prm['w_out'])
    return y_c, y_l


def setup_inputs(seed: int = 0) -> dict:
    key = jax.random.key(seed)
    ks = jax.random.split(key, 32)
    D = D_MODEL

    def nrm(i, shape, s):
        return jax.random.normal(ks[i], shape, jnp.float32) * s

    def uni(i, shape, lo, hi):
        return jax.random.uniform(ks[i], shape, jnp.float32, lo, hi)

    ml_gate_b = jnp.concatenate([nrm(8, (DEPTH, 2, 1, ML_HEADS), 0.1),
                                 uni(9, (DEPTH, 2, 1, ML_HEADS), 3.0, 6.0)], axis=2).reshape(DEPTH, 4 * ML_HEADS)
    return {
        'x': nrm(0, (BATCH, SEQ, D), 1.0),
        'c': nrm(1, (BATCH, D), 1.0),
        'ctx': nrm(2, (BATCH, CTX_LEN, D), 1.0),
        'c_ctx': nrm(3, (D,), 1.0),
        'w_mod': nrm(4, (DEPTH, D, 6 * D), 0.5 * D ** -0.5),
        'b_mod': nrm(5, (DEPTH, 6 * D), 0.02),
        'norm1_g': 1.0 + nrm(6, (DEPTH, D), 0.02),
        'w_in': nrm(7, (DEPTH, D, D_IN), D ** -0.5),
        'ml_gate_b': ml_gate_b,
        'ml_norm_g': 1.0 + nrm(10, (DEPTH, BRANCH_W), 0.02),
        'rw_mu': uni(11, (DEPTH, sum(RW_SIZES)), 0.0, 1.0),
        'rw_w0': uni(12, (DEPTH, 2, RW_W), -5.0, 0.0),
        'rw_w_up': nrm(13, (DEPTH, 2, RW_DECAY_LORA, RW_W), 0.5 * RW_DECAY_LORA ** -0.5),
        'rw_a0': nrm(14, (DEPTH, 2, RW_W), 0.1),
        'rw_a_up': nrm(15, (DEPTH, 2, RW_ICLR_LORA, RW_W), RW_ICLR_LORA ** -0.5),
        'rw_g_up': nrm(16, (DEPTH, RW_GATE_LORA, RW_W), RW_GATE_LORA ** -0.5),
        'rw_k_k': 1.0 + nrm(17, (DEPTH, RW_W), 0.1),
        'rw_k_a': 1.0 + nrm(18, (DEPTH, RW_W), 0.1),
        'rw_r_k': nrm(19, (DEPTH, RW_W), 0.1),
        'rw_ln_w': 1.0 + nrm(20, (DEPTH, RW_W), 0.02),
        'rw_ln_b': nrm(21, (DEPTH, RW_W), 0.02),
        'na_rpb': nrm(22, (DEPTH, NA_HEADS, 2 * NA_WIN_R - 1, 2 * NA_WIN_C - 1), 0.1),
        'w_branch': nrm(23, (DEPTH, N_BRANCH, BRANCH_W, D), BRANCH_W ** -0.5),
        'w_out': nrm(24, (DEPTH, D, D), D ** -0.5),
        'norm2_g': 1.0 + nrm(25, (DEPTH, D), 0.02),
        'w_ff1': nrm(26, (DEPTH, D, D_FF), D ** -0.5),
        'w_ff2': nrm(27, (DEPTH, D_FF, D), D_FF ** -0.5),
        'final_g': 1.0 + nrm(28, (D,), 0.02),
    }


def reference(x, c, ctx, c_ctx, w_mod, b_mod, norm1_g, w_in, ml_gate_b, ml_norm_g, rw_mu, rw_w0, rw_w_up,
              rw_a0, rw_a_up, rw_g_up, rw_k_k, rw_k_a, rw_r_k, rw_ln_w, rw_ln_b, na_rpb, w_branch, w_out,
              norm2_g, w_ff1, w_ff2, final_g):
    xl, xc = x, ctx
    s_lat, s_ctx = jax.nn.silu(c), jax.nn.silu(c_ctx)
    for l in range(DEPTH):
        with_ctx = l < DEPTH - 1
        mod_l = jnp.split((s_lat @ w_mod[l] + b_mod[l])[:, None, :], 6, axis=-1)
        mod_c = jnp.split((s_ctx @ w_mod[l] + b_mod[l])[None, None, :], 6, axis=-1)
        prm = {'w_in': w_in[l], 'ml_gate_b': ml_gate_b[l], 'ml_norm_g': ml_norm_g[l], 'rw_mu': rw_mu[l],
               'rw_w0': rw_w0[l], 'rw_w_up': rw_w_up[l], 'rw_a0': rw_a0[l], 'rw_a_up': rw_a_up[l],
               'rw_g_up': rw_g_up[l], 'rw_k_k': rw_k_k[l], 'rw_k_a': rw_k_a[l], 'rw_r_k': rw_r_k[l],
               'rw_ln_w': rw_ln_w[l], 'rw_ln_b': rw_ln_b[l], 'na_rpb': na_rpb[l],
               'w_branch': w_branch[l], 'w_out': w_out[l]}
        hl = modulate(rms_norm(xl, norm1_g[l]), mod_l[0], mod_l[1])
        hc = modulate(rms_norm(xc, norm1_g[l]), mod_c[0], mod_c[1])
        y_c, y_l = token_mixer(hc, hl, prm, with_ctx)
        xl = (xl + mod_l[2] * y_l).astype(x.dtype)
        hl = modulate(rms_norm(xl, norm2_g[l]), mod_l[3], mod_l[4])
        xl = (xl + mod_l[5] * sq_relu_mlp(hl, w_ff1[l], w_ff2[l])).astype(x.dtype)
        if with_ctx:
            xc = (xc + mod_c[2] * y_c).astype(ctx.dtype)
            hc = modulate(rms_norm(xc, norm2_g[l]), mod_c[3], mod_c[4])
            xc = (xc + mod_c[5] * sq_relu_mlp(hc, w_ff1[l], w_ff2[l])).astype(ctx.dtype)
    return rms_norm(xl, final_g)
```

```python
import functools
import math

import jax
import jax.numpy as jnp
from jax import lax
from jax.experimental import pallas as pl
from jax.experimental.pallas import tpu as pltpu

F32 = jnp.float32
BF16 = jnp.bfloat16
HI = lax.Precision.HIGHEST

D_MODEL = 2048
GRID_W = 64
BRANCH_W = D_MODEL // 2
N_BRANCH = 3
ML_HEADS = 4
ML_DV = BRANCH_W // ML_HEADS
ML_DQK = ML_DV // 2
RW_DH = 64
RW_HEADS = BRANCH_W // RW_DH
RW_LORA = 96
RW_GATE_LORA = 256
RW_GN_EPS = 64e-5
NA_DH = 128
NA_HEADS = BRANCH_W // NA_DH
NA_WIN_R = 8
NA_WIN_C = 16
D_FF = 4 * D_MODEL
ROPE_BASE = 10000.0
EPS = 1e-6
CHUNK = 64
NEG = -1e30

ML_Q0, ML_K0, ML_V0, ML_O0 = 0, 512, 1024, 2048
NA_0 = 3072
GATE_0 = 6144
RW_0 = 12288
RW_SPAN = 4096
RW_COLS = 3712
MLG_0 = RW_0 + RW_COLS
P_COLS = RW_0 + RW_SPAN

VMEM_LIMIT = 56 * 1024 * 1024


def _cparams(sem):
    return pltpu.CompilerParams(dimension_semantics=sem, vmem_limit_bytes=VMEM_LIMIT)


def _tile(n, cap, mult):
    best = None
    for t in range(mult, min(n, cap) + 1, mult):
        if n % t == 0:
            best = t
    assert best is not None, (n, cap, mult)
    return best


def _dot(a, b, prec=None):
    return jnp.dot(a, b, preferred_element_type=F32, precision=prec)


def _dot_nt(a, b, prec=None):
    return lax.dot_general(a, b, (((1,), (1,)), ((), ())), preferred_element_type=F32, precision=prec)


def _dot_tn(a, b, prec=None):
    return lax.dot_general(a, b, (((0,), (0,)), ((), ())), preferred_element_type=F32, precision=prec)


def _sigmoid(x):
    return 1.0 / (1.0 + jnp.exp(-x))


def _softplus(x):
    return jnp.maximum(x, 0.0) + jnp.log(1.0 + jnp.exp(-jnp.abs(x)))


def _mod_kernel(s_ref, w_ref, b_ref, o_ref):
    o_ref[...] = _dot(s_ref[...], w_ref[...], HI) + b_ref[...]


def _modulation(s, w, b):
    R, D = s.shape
    N = w.shape[1]
    tn = _tile(N, 1024, 128)
    return pl.pallas_call(
        _mod_kernel,
        grid=(N // tn,),
        in_specs=[pl.BlockSpec((R, D), lambda j: (0, 0)),
                  pl.BlockSpec((D, tn), lambda j: (0, j)),
                  pl.BlockSpec((1, tn), lambda j: (0, j))],
        out_specs=pl.BlockSpec((R, tn), lambda j: (0, j)),
        out_shape=jax.ShapeDtypeStruct((R, N), F32),
        compiler_params=_cparams(("arbitrary",)),
        name="modulation",
    )(s, w, b.reshape(1, N))


def _mm_norm_kernel(x_ref, g_ref, mod_ref, w_ref, o_ref, h_scr, *, tm, ctx_len, relu2):
    i = pl.program_id(1)

    @pl.when(pl.program_id(2) == 0)
    def _():
        x = x_ref[0]
        y = x * lax.rsqrt(jnp.mean(x * x, axis=-1, keepdims=True) + EPS) * g_ref[...]
        row = i * tm + lax.broadcasted_iota(jnp.int32, (tm, 1), 0)
        is_ctx = row < ctx_len
        m = mod_ref[0]
        shift = jnp.where(is_ctx, m[0:1], m[2:3])
        scale = jnp.where(is_ctx, m[1:2], m[3:4])
        h_scr[...] = (y * (1.0 + scale) + shift).astype(BF16)

    acc = _dot(h_scr[...], w_ref[...])
    if relu2:
        acc = jnp.square(jnp.maximum(acc, 0.0))
    o_ref[0] = acc.astype(o_ref.dtype)


def _mm_norm(x, g, mod4, w, ctx_len, out_dtype, relu2):
    B, T, D = x.shape
    N = w.shape[1]
    tm = _tile(T, 640, 16)
    tn = _tile(N, 1024, 128)
    kern = functools.partial(_mm_norm_kernel, tm=tm, ctx_len=ctx_len, relu2=relu2)
    return pl.pallas_call(
        kern,
        grid=(B, T // tm, N // tn),
        in_specs=[pl.BlockSpec((1, tm, D), lambda b, i, j: (b, i, 0)),
                  pl.BlockSpec((1, D), lambda b, i, j: (0, 0)),
                  pl.BlockSpec((1, 4, D), lambda b, i, j: (b, 0, 0)),
                  pl.BlockSpec((D, tn), lambda b, i, j: (0, j))],
        out_specs=pl.BlockSpec((1, tm, tn), lambda b, i, j: (b, i, j)),
        out_shape=jax.ShapeDtypeStruct((B, T, N), out_dtype),
        scratch_shapes=[pltpu.VMEM((tm, D), BF16)],
        compiler_params=_cparams(("parallel", "parallel", "arbitrary")),
        name="mm_norm",
    )(x, g.reshape(1, D), mod4, w)


def _mm_res_kernel(a_ref, w_ref, x_ref, gate_ref, o_ref, acc_ref, *, tm, ctx_len, nk):
    i = pl.program_id(1)
    k = pl.program_id(3)

    @pl.when(k == 0)
    def _():
        acc_ref[...] = jnp.zeros_like(acc_ref)

    acc_ref[...] += _dot(a_ref[0], w_ref[...])

    @pl.when(k == nk - 1)
    def _():
        row = i * tm + lax.broadcasted_iota(jnp.int32, (tm, 1), 0)
        gt = gate_ref[0]
        gate = jnp.where(row < ctx_len, gt[0:1], gt[1:2])
        o_ref[0] = x_ref[0] + gate * acc_ref[...]


def _mm_res(a, w, x, gate2, ctx_len):
    B, T, K = a.shape
    D = w.shape[1]
    tm = _tile(T, 640, 16)
    tn = _tile(D, 1024, 128)
    tk = _tile(K, 2048, 128)
    nk = K // tk
    kern = functools.partial(_mm_res_kernel, tm=tm, ctx_len=ctx_len, nk=nk)
    return pl.pallas_call(
        kern,
        grid=(B, T // tm, D // tn, nk),
        in_specs=[pl.BlockSpec((1, tm, tk), lambda b, i, j, k: (b, i, k)),
                  pl.BlockSpec((tk, tn), lambda b, i, j, k: (k, j)),
                  pl.BlockSpec((1, tm, tn), lambda b, i, j, k: (b, i, j)),
                  pl.BlockSpec((1, 2, tn), lambda b, i, j, k: (b, 0, j))],
        out_specs=pl.BlockSpec((1, tm, tn), lambda b, i, j, k: (b, i, j)),
        out_shape=jax.ShapeDtypeStruct((B, T, D), F32),
        scratch_shapes=[pltpu.VMEM((tm, tn), F32)],
        compiler_params=_cparams(("parallel", "parallel", "parallel", "arbitrary")),
        name="mm_res",
    )(a, w, x, gate2)


def _merge_kernel(ym_ref, yr_ref, yn_ref, g0_ref, g1_ref, g2_ref, wb_ref, o_ref):
    acc = _sigmoid(g0_ref[0]) * _dot(ym_ref[0], wb_ref[0])
    acc += _sigmoid(g1_ref[0]) * _dot(yr_ref[0], wb_ref[1])
    acc += _sigmoid(g2_ref[0]) * _dot(yn_ref[0], wb_ref[2])
    o_ref[0] = acc.astype(o_ref.dtype)


def _merge(ym, yr, yn, p, wb):
    B, T, W = ym.shape
    D = wb.shape[2]
    tm = _tile(T, 640, 16)
    tn = _tile(D, 512, 128)
    yspec = pl.BlockSpec((1, tm, W), lambda b, i, j: (b, i, 0))

    def gspec(n):
        base = (GATE_0 + n * D) // tn
        return pl.BlockSpec((1, tm, tn), lambda b, i, j: (b, i, base + j))

    return pl.pallas_call(
        _merge_kernel,
        grid=(B, T // tm, D // tn),
        in_specs=[yspec, yspec, yspec, gspec(0), gspec(1), gspec(2),
                  pl.BlockSpec((N_BRANCH, W, tn), lambda b, i, j: (0, 0, j))],
        out_specs=pl.BlockSpec((1, tm, tn), lambda b, i, j: (b, i, j)),
        out_shape=jax.ShapeDtypeStruct((B, T, D), BF16),
        compiler_params=_cparams(("parallel", "parallel", "arbitrary")),
        name="merge",
    )(ym, yr, yn, p, p, p, wb)


def _final_norm_kernel(x_ref, g_ref, o_ref):
    x = x_ref[0]
    o_ref[0] = x * lax.rsqrt(jnp.mean(x * x, axis=-1, keepdims=True) + EPS) * g_ref[...]


def _final_norm(xs, g, ctx_len):
    B, T, D = xs.shape
    seq = T - ctx_len
    tm = _tile(math.gcd(seq, ctx_len), 256, 8)
    off = ctx_len // tm
    return pl.pallas_call(
        _final_norm_kernel,
        grid=(B, seq // tm),
        in_specs=[pl.BlockSpec((1, tm, D), lambda b, i: (b, off + i, 0)),
                  pl.BlockSpec((1, D), lambda b, i: (0, 0))],
        out_specs=pl.BlockSpec((1, tm, D), lambda b, i: (b, i, 0)),
        out_shape=jax.ShapeDtypeStruct((B, seq, D), F32),
        compiler_params=_cparams(("parallel", "parallel")),
        name="final_norm",
    )(xs, g.reshape(1, D))


def _chunk_index(d, s, n_ctx, n_all):
    back = jnp.where(s < n_ctx, n_ctx - 1 - s, n_all + n_ctx - 1 - s)
    return jnp.where(d == 0, s, back)


def _rope(x, cos, sin_signed, first_half):
    swapped = jnp.where(first_half, pltpu.roll(x, 96, 1), pltpu.roll(x, 32, 1))
    return x * cos + swapped * sin_signed


def _mlstm_kernel(bias_ref, q_ref, k_ref, v_ref, gi_ref, gf_ref, grow_ref, cos_ref, sin_ref,
                  h_ref, ct_ref, m_ref):
    L = CHUNK
    d = pl.program_id(0)
    hh = pl.program_id(2)

    @pl.when(pl.program_id(3) == 0)
    def _():
        ct_ref[...] = jnp.zeros_like(ct_ref)
        m_ref[...] = jnp.zeros_like(m_ref)

    b_i = bias_ref[8 * d + hh]
    b_f = bias_ref[8 * d + 4 + hh]

    def log_sigmoid(x):
        return jnp.minimum(x, 0.0) - jnp.log(1.0 + jnp.exp(-jnp.abs(x)))

    ig_c = gi_ref[0, 0, 0, 0] + b_i
    lf_c = log_sigmoid(gf_ref[0, 0, 0, 0] + b_f)
    gr = grow_ref[0, 0, 0, 0]
    ig_r = gr[0:1, :] + b_i
    lf_r = log_sigmoid(gr[1:2, :] + b_f)

    row = lax.broadcasted_iota(jnp.int32, (L, L), 0)
    col = lax.broadcasted_iota(jnp.int32, (L, L), 1)
    diff = (row - col) * jnp.where(d == 0, 1, -1)
    incl = diff >= 0
    incl_t = diff <= 0
    b_c = jnp.sum(jnp.where(incl, lf_r, 0.0), axis=1, keepdims=True)
    b_r = jnp.sum(jnp.where(incl_t, lf_c, 0.0), axis=0, keepdims=True)
    b_end = jnp.sum(lf_r, axis=1, keepdims=True)
    a_c = b_end - b_c + ig_c
    a_r = b_end - b_r + ig_r
    a_max = jnp.max(a_r, axis=1, keepdims=True)
    wgt_c = jnp.exp(a_c - a_max)

    cos = cos_ref[...]
    sin = sin_ref[...]
    lane = lax.broadcasted_iota(jnp.int32, (L, ML_DQK), 1)
    first_half = (lane % 64) < 32
    q = _rope(q_ref[0] * (ML_DQK ** -0.5), cos, sin, first_half).astype(BF16)
    k = _rope(k_ref[0], cos, sin, first_half).astype(BF16)
    vext = jnp.concatenate([v_ref[0], jnp.ones((L, 128), F32)], axis=1)

    m0 = m_ref[...]
    ct0 = ct_ref[...]

    log_intra = jnp.where(incl, b_c - b_r + ig_r, NEG)
    log_inter = b_c + m0
    m_j = jnp.maximum(log_inter, jnp.max(log_intra, axis=1, keepdims=True))
    smat = _dot_nt(q, k) * jnp.exp(log_intra - m_j)
    w_inter = jnp.exp(log_inter - m_j)
    numext = _dot(smat.astype(BF16), vext.astype(BF16)) + w_inter * _dot(q, ct0.astype(BF16))
    den = jnp.maximum(jnp.abs(numext[:, ML_DV:]), jnp.exp(-m_j))
    h_ref[0, 0] = numext[:, :ML_DV] / jnp.concatenate([den, den], axis=1)

    m_new = jnp.maximum(b_end + m0, a_max)
    s_old = jnp.exp(b_end + m0 - m_new)
    s_loc = jnp.exp(a_max - m_new)
    ct_ref[...] = s_old * ct0 + s_loc * _dot_tn(k, (vext * wgt_c).astype(BF16))
    m_ref[...] = m_new


def _mlstm(p, gate_b, gates, cos, sin, ctx_len):
    B, T, _ = p.shape
    L = CHUNK
    nch = T // L
    nctx = ctx_len // L
    g4 = gates.reshape(B, nch, L, 2, 2, ML_HEADS)
    gcol = jnp.transpose(g4, (3, 4, 0, 5, 1, 2))[..., None]
    grow = jnp.transpose(g4, (3, 0, 5, 1, 4, 2))

    def tmap(colblk):
        def f(d, b, h, s):
            return (b, _chunk_index(d, s, nctx, nch), colblk + h)
        return f

    def gmap(d, b, h, s):
        return (d, b, h, _chunk_index(d, s, nctx, nch), 0, 0)

    def cmap(d, b, h, s):
        return (_chunk_index(d, s, nctx, nch), 0)

    def omap(d, b, h, s):
        return (d, b, _chunk_index(d, s, nctx, nch), h)

    return pl.pallas_call(
        _mlstm_kernel,
        grid=(2, B, ML_HEADS, nch),
        in_specs=[pl.BlockSpec(memory_space=pltpu.SMEM),
                  pl.BlockSpec((1, L, ML_DQK), tmap(ML_Q0 // ML_DQK)),
                  pl.BlockSpec((1, L, ML_DQK), tmap(ML_K0 // ML_DQK)),
                  pl.BlockSpec((1, L, ML_DV), tmap(ML_V0 // ML_DV)),
                  pl.BlockSpec((1, 1, 1, 1, L, 1), gmap),
                  pl.BlockSpec((1, 1, 1, 1, L, 1), gmap),
                  pl.BlockSpec((1, 1, 1, 1, 2, L), gmap),
                  pl.BlockSpec((L, ML_DQK), cmap),
                  pl.BlockSpec((L, ML_DQK), cmap)],
        out_specs=pl.BlockSpec((1, 1, L, ML_DV), omap),
        out_shape=jax.ShapeDtypeStruct((2, B, T, BRANCH_W), F32),
        scratch_shapes=[pltpu.VMEM((ML_DQK, ML_DV + 128), F32), pltpu.VMEM((1, 1), F32)],
        compiler_params=_cparams(("parallel", "parallel", "parallel", "arbitrary")),
        name="mlstm",
    )(gate_b, p, p, p, gcol[:, 0], gcol[:, 1], grow, cos, sin)


def _mlstm_post_kernel(hf_ref, hb_ref, o_ref, g_ref, y_ref):
    h = hf_ref[0, 0] + hb_ref[0, 0]
    h = h * lax.rsqrt(jnp.mean(h * h, axis=-1, keepdims=True) + EPS) * g_ref[...]
    y_ref[0] = (h * _sigmoid(o_ref[0])).astype(y_ref.dtype)


def _mlstm_post(h2, p, norm_g):
    _, B, T, W = h2.shape
    tm = _tile(T, 1024, 16)
    return pl.pallas_call(
        _mlstm_post_kernel,
        grid=(B, T // tm, ML_HEADS),
        in_specs=[pl.BlockSpec((1, 1, tm, ML_DV), lambda b, i, h: (0, b, i, h)),
                  pl.BlockSpec((1, 1, tm, ML_DV), lambda b, i, h: (1, b, i, h)),
                  pl.BlockSpec((1, tm, ML_DV), lambda b, i, h: (b, i, ML_O0 // ML_DV + h)),
                  pl.BlockSpec((1, ML_DV), lambda b, i, h: (0, h))],
        out_specs=pl.BlockSpec((1, tm, ML_DV), lambda b, i, h: (b, i, h)),
        out_shape=jax.ShapeDtypeStruct((B, T, W), BF16),
        compiler_params=_cparams(("parallel", "parallel", "parallel")),
        name="mlstm_post",
    )(h2, h2, p, norm_g.reshape(1, W))


def _rwkv_prep_kernel(x_ref, xp_ref, xn_ref, mu_ref, w0_ref, a0_ref, wd_ref, wa_ref, wg_ref,
                      r_ref, k_ref, v_ref, ld_ref, ar_ref, g_ref, *, tt, ctx_len, nt):
    i = pl.program_id(1)
    x = x_ref[0][:, :RW_COLS]
    first = (i == 0) | (i * tt == ctx_len)
    last = ((i + 1) * tt == ctx_len) | (i == nt - 1)
    prev_row = jnp.where(first, 0.0, xp_ref[0][7:8, :RW_COLS])
    next_row = jnp.where(last, 0.0, xn_ref[0][0:1, :RW_COLS])
    ridx = lax.broadcasted_iota(jnp.int32, (tt, 1), 0)
    prev = jnp.where(ridx == 0, prev_row, pltpu.roll(x, 1, 0))
    nxt = jnp.where(ridx == tt - 1, next_row, pltpu.roll(x, tt - 1, 0))
    xs = x + mu_ref[...] * (0.5 * (prev + nxt) - x)

    W = BRANCH_W
    r_ref[0] = xs[:, 0:W]
    k_ref[0] = xs[:, W:2 * W]
    v_ref[0] = xs[:, 2 * W:3 * W]
    wlo = _dot(jnp.tanh(xs[:, 3 * W:3 * W + 256]), wd_ref[...], HI)
    alo = _dot(xs[:, 3 * W + 128:3 * W + 384], wa_ref[...], HI)
    for d in range(2):
        w = -_softplus(-(w0_ref[d:d + 1, :] + wlo[:, d * W:(d + 1) * W])) - 0.5
        ld_ref[d, 0] = -jnp.exp(w)
        ar_ref[d, 0] = _sigmoid(a0_ref[d:d + 1, :] + alo[:, d * W:(d + 1) * W])
    g_ref[0] = _dot(_sigmoid(xs[:, 3 * W + 384:3 * W + 640]).astype(BF16), wg_ref[...])


def _rwkv_prep(p, prm, ctx_len):
    B, T, _ = p.shape
    W = BRANCH_W
    tt = _tile(math.gcd(T, ctx_len), 256, 8)
    nt = T // tt
    h8 = tt // 8
    nb8 = T // 8
    z = lambda *s: jnp.zeros(s, F32)
    wd = jnp.concatenate([
        jnp.concatenate([prm['rw_w_up'][0], z(RW_LORA, W)], axis=1),
        jnp.concatenate([z(RW_LORA, W), prm['rw_w_up'][1]], axis=1),
        z(256 - 2 * RW_LORA, 2 * W)], axis=0)
    wa = jnp.concatenate([
        z(2 * RW_LORA - 128, 2 * W),
        jnp.concatenate([prm['rw_a_up'][0], z(RW_LORA, W)], axis=1),
        jnp.concatenate([z(RW_LORA, W), prm['rw_a_up'][1]], axis=1)], axis=0)
    wg = prm['rw_g_up'].astype(BF16)
    mu = prm['rw_mu'].reshape(1, RW_COLS)
    kern = functools.partial(_rwkv_prep_kernel, tt=tt, ctx_len=ctx_len, nt=nt)
    rwblk = RW_0 // RW_SPAN
    tok = jax.ShapeDtypeStruct((B, T, W), F32)
    tok2 = jax.ShapeDtypeStruct((2, B, T, W), F32)
    full = lambda shape: pl.BlockSpec(shape, lambda b, i: (0,) * len(shape))
    ospec = pl.BlockSpec((1, tt, W), lambda b, i: (b, i, 0))
    ospec2 = pl.BlockSpec((2, 1, tt, W), lambda b, i: (0, b, i, 0))
    return pl.pallas_call(
        kern,
        grid=(B, nt),
        in_specs=[pl.BlockSpec((1, tt, RW_SPAN), lambda b, i: (b, i, rwblk)),
                  pl.BlockSpec((1, 8, RW_SPAN), lambda b, i: (b, jnp.maximum(i * h8 - 1, 0), rwblk)),
                  pl.BlockSpec((1, 8, RW_SPAN), lambda b, i: (b, jnp.minimum((i + 1) * h8, nb8 - 1), rwblk)),
                  full((1, RW_COLS)), full((2, W)), full((2, W)),
                  full((256, 2 * W)), full((256, 2 * W)), full((256, W))],
        out_specs=[ospec, ospec, ospec, ospec2, ospec2, ospec],
        out_shape=[tok, tok, tok, tok2, tok2, tok],
        compiler_params=_cparams(("parallel", "parallel")),
        name="rwkv_prep",
    )(p, p, p, mu, prm['rw_w0'], prm['rw_a0'], wd, wa, wg)


def _rwkv_scan_kernel(r_ref, k_ref, v_ref, ld_ref, ar_ref, kk_ref, ka_ref, y_ref, st_ref, *, G):
    L = CHUNK
    N = RW_DH
    d = pl.program_id(0)

    @pl.when(pl.program_id(3) == 0)
    def _():
        st_ref[...] = jnp.zeros_like(st_ref)

    row = lax.broadcasted_iota(jnp.int32, (L, L), 0)
    col = lax.broadcasted_iota(jnp.int32, (L, L), 1)
    diff = (row - col) * jnp.where(d == 0, 1, -1)
    before = diff > 0
    incl = diff >= 0
    tri = incl.astype(F32)
    eye = (row == col).astype(F32)
    ones = jnp.ones((L, N), F32)

    def head(h, carry):
        R = r_ref[0, h]
        K = k_ref[0, h]
        V = v_ref[0, h]
        LD = ld_ref[0, 0, h]
        AR = ar_ref[0, 0, h]
        kk = K * kk_ref[h]
        kk = kk / jnp.maximum(jnp.sqrt(jnp.sum(kk * kk, axis=-1, keepdims=True)), 1e-12)
        KD = K * (1.0 + (AR - 1.0) * ka_ref[h])
        BV = kk * AR
        c = _dot(tri, LD, HI)
        c_end = jnp.sum(LD, axis=0, keepdims=True)
        At = -kk * jnp.exp(c - LD)
        e_neg = jnp.exp(-c)
        Bt = BV * e_neg
        Kt = KD * e_neg
        Rt = R * jnp.exp(c)
        e_end = jnp.exp(c_end - c)
        Bh = BV * e_end
        Kh = KD * e_end
        AtRt = jnp.concatenate([At, Rt], axis=0)
        X1 = _dot_nt(AtRt, Bt, HI)
        X2 = _dot_nt(AtRt, Kt, HI)
        Nab = jnp.where(before, X1[:L], 0.0)
        Mrb = jnp.where(incl, X1[L:], 0.0)
        Nak = jnp.where(before, X2[:L], 0.0)
        Mrk = jnp.where(incl, X2[L:], 0.0)
        T = eye + Nab
        Np = Nab
        for _ in range(int(math.log2(L)) - 1):
            Np = _dot(Np, Np, HI)
            T = T + _dot(T, Np, HI)
        Wm = _dot(T, At, HI)
        Ul = _dot(T, _dot(Nak, V, HI), HI)
        Q = Rt + _dot(Mrb, Wm, HI)
        Yl = _dot(Mrb, Ul, HI) + _dot(Mrk, V, HI)
        Pm = _dot_tn(Bh, Wm, HI)
        Sl = _dot_tn(Bh, Ul, HI) + _dot_tn(Kh, V, HI)
        dcol = jnp.exp(_dot_tn(LD, ones, HI))
        S0 = st_ref[h]
        y_ref[0, 0, h] = _dot(Q, S0, HI) + Yl
        st_ref[h] = dcol * S0 + _dot(Pm, S0, HI) + Sl
        return carry

    lax.fori_loop(0, G, head, 0)


def _rwkv_scan(r, k, v, ld, ar, k_k, k_a, ctx_len):
    B, H, T, N = r.shape
    L = CHUNK
    nch = T // L
    nctx = ctx_len // L
    G = H

    def smap(d, b, g, s):
        return (b, g, _chunk_index(d, s, nctx, nch), 0)

    def dmap(d, b, g, s):
        return (d, b, g, _chunk_index(d, s, nctx, nch), 0)

    pspec = pl.BlockSpec((G, 1, N), lambda d, b, g, s: (g, 0, 0))
    kern = functools.partial(_rwkv_scan_kernel, G=G)
    return pl.pallas_call(
        kern,
        grid=(2, B, H // G, nch),
        in_specs=[pl.BlockSpec((1, G, L, N), smap), pl.BlockSpec((1, G, L, N), smap),
                  pl.BlockSpec((1, G, L, N), smap),
                  pl.BlockSpec((1, 1, G, L, N), dmap), pl.BlockSpec((1, 1, G, L, N), dmap),
                  pspec, pspec],
        out_specs=pl.BlockSpec((1, 1, G, L, N), dmap),
        out_shape=jax.ShapeDtypeStruct((2, B, H, T, N), F32),
        scratch_shapes=[pltpu.VMEM((G, N, N), F32)],
        compiler_params=_cparams(("parallel", "parallel", "parallel", "arbitrary")),
        name="rwkv_scan",
    )(r, k, v, ld, ar, k_k.reshape(H, 1, N), k_a.reshape(H, 1, N))


def _rwkv_post_kernel(yf_ref, yb_ref, r_ref, k_ref, v_ref, g_ref, rk_ref, lw_ref, lb_ref, o_ref):
    y = yf_ref[0, 0] + yb_ref[0, 0]
    mu = jnp.mean(y, axis=-1, keepdims=True)
    yc = y - mu
    var = jnp.mean(yc * yc, axis=-1, keepdims=True)
    yn = yc * lax.rsqrt(var + RW_GN_EPS)
    bonus = jnp.sum(r_ref[0] * k_ref[0] * rk_ref[...], axis=-1, keepdims=True) * v_ref[0]
    o_ref[0] = ((yn * lw_ref[...] + lb_ref[...] + bonus) * g_ref[0]).astype(o_ref.dtype)


def _rwkv_post(y2, r, k, v, g, r_k, ln_w, ln_b):
    _, B, H, T, N = y2.shape
    tt = _tile(T, 512, 16)
    yspec = lambda d: pl.BlockSpec((1, 1, H, tt, N), lambda b, i: (d, b, 0, i, 0))
    hspec = pl.BlockSpec((1, H, tt, N), lambda b, i: (b, 0, i, 0))
    pspec = pl.BlockSpec((H, 1, N), lambda b, i: (0, 0, 0))
    return pl.pallas_call(
        _rwkv_post_kernel,
        grid=(B, T // tt),
        in_specs=[yspec(0), yspec(1), hspec, hspec, hspec, hspec, pspec, pspec, pspec],
        out_specs=hspec,
        out_shape=jax.ShapeDtypeStruct((B, H, T, N), BF16),
        compiler_params=_cparams(("parallel", "parallel")),
        name="rwkv_post",
    )(y2, y2, r, k, v, g, r_k.reshape(H, 1, N), ln_w.reshape(H, 1, N), ln_b.reshape(H, 1, N))


def _rwkv_branch(p, prm, ctx_len):
    B, T, _ = p.shape
    H, N = RW_HEADS, RW_DH
    r, k, v, ld, ar, g = _rwkv_prep(p, prm, ctx_len)
    hm = lambda t: jnp.swapaxes(t.reshape(t.shape[:-1] + (H, N)), -2, -3)
    r, k, v, ld, ar, g = (hm(t) for t in (r, k, v, ld, ar, g))
    y2 = _rwkv_scan(r, k, v, ld, ar, prm['rw_k_k'], prm['rw_k_a'], ctx_len)
    y = _rwkv_post(y2, r, k, v, g, prm['rw_r_k'], prm['rw_ln_w'], prm['rw_ln_b'])
    return jnp.swapaxes(y, 1, 2).reshape(B, T, H * N)


def _na_kernel(q_ref, k_ref, v_ref, bias_ref, o_ref, *, ctx_len, rows):
    r = pl.program_id(2)
    start = jnp.clip(r - NA_WIN_R // 2, 0, rows - NA_WIN_R)
    ro0 = start - r + (NA_WIN_R - 1)
    nwin = NA_WIN_R * GRID_W
    koff = pl.multiple_of(ctx_len + start * GRID_W, GRID_W)
    scale = NA_DH ** -0.5
    q = q_ref[0].astype(BF16)
    k_win = k_ref[0, pl.ds(koff, nwin), :].astype(BF16)
    v_win = v_ref[0, pl.ds(koff, nwin), :].astype(BF16)
    k_ctx = k_ref[0, 0:ctx_len, :].astype(BF16)
    v_ctx = v_ref[0, 0:ctx_len, :].astype(BF16)
    s_win = _dot_nt(q, k_win) * scale + bias_ref[0, ro0]
    s_ctx = _dot_nt(q, k_ctx) * scale
    m = jnp.maximum(jnp.max(s_win, axis=-1, keepdims=True), jnp.max(s_ctx, axis=-1, keepdims=True))
    p_win = jnp.exp(s_win - m)
    p_ctx = jnp.exp(s_ctx - m)
    den = jnp.sum(p_win, axis=-1, keepdims=True) + jnp.sum(p_ctx, axis=-1, keepdims=True)
    y = _dot(p_win.astype(BF16), v_win) + _dot(p_ctx.astype(BF16), v_ctx)
    o_ref[0] = (y / den).astype(o_ref.dtype)


def _na_ctx_kernel(q_ref, k_ref, v_ref, o_ref):
    scale = NA_DH ** -0.5
    q = q_ref[0].astype(BF16)
    k = k_ref[0].astype(BF16)
    s = _dot_nt(q, k) * scale
    p = jnp.exp(s - jnp.max(s, axis=-1, keepdims=True))
    y = _dot(p.astype(BF16), v_ref[0].astype(BF16))
    o_ref[0] = (y / jnp.sum(p, axis=-1, keepdims=True)).astype(o_ref.dtype)


def _na_bias_table(rpb, rows):
    c = jnp.arange(GRID_W)
    c_start = jnp.clip(c - NA_WIN_C // 2, 0, GRID_W - NA_WIN_C)
    col_ok = (c[None, :] >= c_start[:, None]) & (c[None, :] < c_start[:, None] + NA_WIN_C)
    col_off = jnp.clip(c[None, :] - c[:, None], 1 - NA_WIN_C, NA_WIN_C - 1) + (NA_WIN_C - 1)
    toe = jnp.where(col_ok[None, None], rpb[:, :, col_off], NEG)
    slabs = [jnp.concatenate([toe[:, ro0 + j] for j in range(NA_WIN_R)], axis=-1) for ro0 in range(NA_WIN_R)]
    return jnp.stack(slabs, axis=1)


def _na_branch(p, rpb, ctx_len, with_ctx):
    B, T, _ = p.shape
    seq = T - ctx_len
    rows = seq // GRID_W
    H = NA_HEADS
    nwin = NA_WIN_R * GRID_W
    qb, kb, vb = NA_0 // NA_DH, NA_0 // NA_DH + H, NA_0 // NA_DH + 2 * H
    table = _na_bias_table(rpb, rows)
    kern = functools.partial(_na_kernel, ctx_len=ctx_len, rows=rows)
    coff = ctx_len // GRID_W
    y_l = pl.pallas_call(
        kern,
        grid=(B, H, rows),
        in_specs=[pl.BlockSpec((1, GRID_W, NA_DH), lambda b, h, r: (b, coff + r, qb + h)),
                  pl.BlockSpec((1, T, NA_DH), lambda b, h, r: (b, 0, kb + h)),
                  pl.BlockSpec((1, T, NA_DH), lambda b, h, r: (b, 0, vb + h)),
                  pl.BlockSpec((1, NA_WIN_R, GRID_W, nwin), lambda b, h, r: (h, 0, 0, 0))],
        out_specs=pl.BlockSpec((1, GRID_W, NA_DH), lambda b, h, r: (b, r, h)),
        out_shape=jax.ShapeDtypeStruct((B, seq, BRANCH_W), BF16),
        compiler_params=_cparams(("parallel", "parallel", "arbitrary")),
        name="na_latent",
    )(p, p, p, table)
    if not with_ctx:
        return jnp.concatenate([jnp.zeros((B, ctx_len, BRANCH_W), BF16), y_l], axis=1)
    y_c = pl.pallas_call(
        _na_ctx_kernel,
        grid=(B, H),
        in_specs=[pl.BlockSpec((1, ctx_len, NA_DH), lambda b, h: (b, 0, qb + h)),
                  pl.BlockSpec((1, ctx_len, NA_DH), lambda b, h: (b, 0, kb + h)),
                  pl.BlockSpec((1, ctx_len, NA_DH), lambda b, h: (b, 0, vb + h))],
        out_specs=pl.BlockSpec((1, ctx_len, NA_DH), lambda b, h: (b, 0, h)),
        out_shape=jax.ShapeDtypeStruct((B, ctx_len, BRANCH_W), BF16),
        compiler_params=_cparams(("parallel", "parallel")),
        name="na_ctx",
    )(p, p, p)
    return jnp.concatenate([y_c, y_l], axis=1)


def _pack_w_in(w_in):
    ml, rw, na, gt = 3088, 3712, 3072, 6144
    o_rw, o_na, o_gt = ml, ml + rw, ml + rw + na
    pad = jnp.zeros((w_in.shape[0], P_COLS - MLG_0 - 16), w_in.dtype)
    return jnp.concatenate([w_in[:, 0:3072], w_in[:, o_na:o_na + na], w_in[:, o_gt:o_gt + gt],
                            w_in[:, o_rw:o_rw + rw], w_in[:, 3072:3088], pad], axis=1).astype(BF16)


def _rope_tables(seq, ctx_len):
    t = jnp.arange(seq)
    nf = ML_DQK // 4
    freqs = ROPE_BASE ** (-jnp.arange(nf, dtype=F32) / nf)
    ang_r = (t // GRID_W).astype(F32)[:, None] * freqs[None, :]
    ang_c = (t % GRID_W).astype(F32)[:, None] * freqs[None, :]
    cos = jnp.concatenate([jnp.cos(ang_r)] * 2 + [jnp.cos(ang_c)] * 2, axis=1)
    sin = jnp.concatenate([-jnp.sin(ang_r), jnp.sin(ang_r), -jnp.sin(ang_c), jnp.sin(ang_c)], axis=1)
    cos = jnp.concatenate([jnp.ones((ctx_len, ML_DQK), F32), cos], axis=0)
    sin = jnp.concatenate([jnp.zeros((ctx_len, ML_DQK), F32), sin], axis=0)
    return cos, sin


def kernel(x, c, ctx, c_ctx, w_mod, b_mod, norm1_g, w_in, ml_gate_b, ml_norm_g, rw_mu, rw_w0, rw_w_up, rw_a0, rw_a_up, rw_g_up, rw_k_k, rw_k_a, rw_r_k, rw_ln_w, rw_ln_b, na_rpb, w_branch, w_out, norm2_g, w_ff1, w_ff2, final_g):
    B, seq, D = x.shape
    ctx_len = ctx.shape[1]
    depth = w_mod.shape[0]
    xs = jnp.concatenate([ctx, x], axis=1)
    s_all = jax.nn.silu(jnp.concatenate([c_ctx[None, :], c], axis=0))
    s_all = jnp.concatenate([s_all, jnp.zeros((8 - (B + 1) % 8, D), F32)], axis=0) if (B + 1) % 8 else s_all
    cos, sin = _rope_tables(seq, ctx_len)
    for l in range(depth):
        with_ctx = l < depth - 1
        mod = _modulation(s_all, w_mod[l], b_mod[l]).reshape(-1, 6, D)
        mc = jnp.broadcast_to(mod[0:1], (B, 6, D))
        ml_ = mod[1:B + 1]
        mod1 = jnp.stack([mc[:, 0], mc[:, 1], ml_[:, 0], ml_[:, 1]], axis=1)
        gate1 = jnp.stack([mc[:, 2], ml_[:, 2]], axis=1)
        mod2 = jnp.stack([mc[:, 3], mc[:, 4], ml_[:, 3], ml_[:, 4]], axis=1)
        gate2 = jnp.stack([mc[:, 5], ml_[:, 5]], axis=1)
        prm = {'rw_mu': rw_mu[l], 'rw_w0': rw_w0[l], 'rw_w_up': rw_w_up[l], 'rw_a0': rw_a0[l],
               'rw_a_up': rw_a_up[l], 'rw_g_up': rw_g_up[l], 'rw_k_k': rw_k_k[l], 'rw_k_a': rw_k_a[l],
               'rw_r_k': rw_r_k[l], 'rw_ln_w': rw_ln_w[l], 'rw_ln_b': rw_ln_b[l]}

        p = _mm_norm(xs, norm1_g[l], mod1, _pack_w_in(w_in[l]), ctx_len, F32, False)
        h2 = _mlstm(p, ml_gate_b[l], p[:, :, MLG_0:MLG_0 + 16], cos, sin, ctx_len)
        y_m = _mlstm_post(h2, p, ml_norm_g[l])
        y_r = _rwkv_branch(p, prm, ctx_len)
        y_n = _na_branch(p, na_rpb[l], ctx_len, with_ctx)
        merged = _merge(y_m, y_r, y_n, p, w_branch[l].astype(BF16))
        xs = _mm_res(merged, w_out[l].astype(BF16), xs, gate1, ctx_len)
        hff = _mm_norm(xs, norm2_g[l], mod2, w_ff1[l].astype(BF16), ctx_len, BF16, True)
        xs = _mm_res(hff, w_ff2[l].astype(BF16), xs, gate2, ctx_len)
    return _final_norm(xs, final_g, ctx_len)
```

```python
import functools
import math

import jax
import jax.numpy as jnp
import numpy as np
from jax import lax
from jax.experimental import pallas as pl
from jax.experimental.pallas import tpu as pltpu

F32 = jnp.float32
BF16 = jnp.bfloat16
HI = lax.Precision.HIGHEST

D_MODEL = 2048
GRID_W = 64
BRANCH_W = D_MODEL // 2
N_BRANCH = 3
ML_HEADS = 4
ML_DV = BRANCH_W // ML_HEADS
ML_DQK = ML_DV // 2
RW_DH = 64
RW_HEADS = BRANCH_W // RW_DH
RW_LORA = 96
RW_GATE_LORA = 256
RW_GN_EPS = 64e-5
NA_DH = 128
NA_HEADS = BRANCH_W // NA_DH
NA_WIN_R = 8
NA_WIN_C = 16
D_FF = 4 * D_MODEL
ROPE_BASE = 10000.0
EPS = 1e-6
CHUNK = 64
NEG = -1e30

ML_Q0, ML_K0, ML_V0, ML_O0 = 0, 512, 1024, 2048
NA_0 = 3072
GATE_0 = 6144
RW_0 = 12288
RW_SPAN = 4096
RW_COLS = 3712
MLG_0 = RW_0 + RW_COLS
P_COLS = RW_0 + RW_SPAN

VMEM_LIMIT = 56 * 1024 * 1024


def _cparams(sem):
    return pltpu.CompilerParams(dimension_semantics=sem, vmem_limit_bytes=VMEM_LIMIT)


def _tile(n, cap, mult):
    best = None
    for t in range(mult, min(n, cap) + 1, mult):
        if n % t == 0:
            best = t
    assert best is not None, (n, cap, mult)
    return best


def _dot(a, b, prec=None):
    return jnp.dot(a, b, preferred_element_type=F32, precision=prec)


def _dot_nt(a, b, prec=None):
    return lax.dot_general(a, b, (((1,), (1,)), ((), ())), preferred_element_type=F32, precision=prec)


def _dot_tn(a, b, prec=None):
    return lax.dot_general(a, b, (((0,), (0,)), ((), ())), preferred_element_type=F32, precision=prec)


def _sigmoid(x):
    return 1.0 / (1.0 + jnp.exp(-x))


def _softplus(x):
    return jnp.maximum(x, 0.0) + jnp.log(1.0 + jnp.exp(-jnp.abs(x)))


def _mod_kernel(s_ref, w_ref, b_ref, o_ref):
    o_ref[...] = _dot(s_ref[...], w_ref[...], HI) + b_ref[...]


def _modulation(s, w, b):
    R, D = s.shape
    N = w.shape[1]
    tn = _tile(N, 1024, 128)
    return pl.pallas_call(
        _mod_kernel,
        grid=(N // tn,),
        in_specs=[pl.BlockSpec((R, D), lambda j: (0, 0)),
                  pl.BlockSpec((D, tn), lambda j: (0, j)),
                  pl.BlockSpec((1, tn), lambda j: (0, j))],
        out_specs=pl.BlockSpec((R, tn), lambda j: (0, j)),
        out_shape=jax.ShapeDtypeStruct((R, N), F32),
        compiler_params=_cparams(("arbitrary",)),
        name="modulation",
    )(s, w, b.reshape(1, N))


def _mm_norm_kernel(x_ref, g_ref, mod_ref, w_ref, o_ref, h_scr, *, tm, ctx_len, relu2):
    i = pl.program_id(1)

    @pl.when(pl.program_id(2) == 0)
    def _():
        x = x_ref[0]
        y = x * lax.rsqrt(jnp.mean(x * x, axis=-1, keepdims=True) + EPS) * g_ref[...]
        row = i * tm + lax.broadcasted_iota(jnp.int32, (tm, 1), 0)
        is_ctx = row < ctx_len
        m = mod_ref[0]
        shift = jnp.where(is_ctx, m[0:1], m[2:3])
        scale = jnp.where(is_ctx, m[1:2], m[3:4])
        h_scr[...] = (y * (1.0 + scale) + shift).astype(BF16)

    acc = _dot(h_scr[...], w_ref[...])
    if relu2:
        acc = jnp.square(jnp.maximum(acc, 0.0))
    o_ref[0] = acc.astype(o_ref.dtype)


def _mm_norm(x, g, mod4, w, ctx_len, out_dtype, relu2):
    B, T, D = x.shape
    N = w.shape[1]
    tm = _tile(T, 640, 16)
    tn = _tile(N, 1024, 128)
    kern = functools.partial(_mm_norm_kernel, tm=tm, ctx_len=ctx_len, relu2=relu2)
    return pl.pallas_call(
        kern,
        grid=(B, T // tm, N // tn),
        in_specs=[pl.BlockSpec((1, tm, D), lambda b, i, j: (b, i, 0)),
                  pl.BlockSpec((1, D), lambda b, i, j: (0, 0)),
                  pl.BlockSpec((1, 4, D), lambda b, i, j: (b, 0, 0)),
                  pl.BlockSpec((D, tn), lambda b, i, j: (0, j))],
        out_specs=pl.BlockSpec((1, tm, tn), lambda b, i, j: (b, i, j)),
        out_shape=jax.ShapeDtypeStruct((B, T, N), out_dtype),
        scratch_shapes=[pltpu.VMEM((tm, D), BF16)],
        compiler_params=_cparams(("parallel", "parallel", "arbitrary")),
        name="mm_norm",
    )(x, g.reshape(1, D), mod4, w)


def _mm_res_kernel(a_ref, w_ref, x_ref, gate_ref, o_ref, acc_ref, *, tm, ctx_len, nk):
    i = pl.program_id(1)
    k = pl.program_id(3)

    @pl.when(k == 0)
    def _():
        acc_ref[...] = jnp.zeros_like(acc_ref)

    acc_ref[...] += _dot(a_ref[0], w_ref[...])

    @pl.when(k == nk - 1)
    def _():
        row = i * tm + lax.broadcasted_iota(jnp.int32, (tm, 1), 0)
        gt = gate_ref[0]
        gate = jnp.where(row < ctx_len, gt[0:1], gt[1:2])
        o_ref[0] = x_ref[0] + gate * acc_ref[...]


def _mm_res(a, w, x, gate2, ctx_len):
    B, T, K = a.shape
    D = w.shape[1]
    tm = _tile(T, 640, 16)
    tn = _tile(D, 1024, 128)
    tk = _tile(K, 2048, 128)
    nk = K // tk
    kern = functools.partial(_mm_res_kernel, tm=tm, ctx_len=ctx_len, nk=nk)
    return pl.pallas_call(
        kern,
        grid=(B, T // tm, D // tn, nk),
        in_specs=[pl.BlockSpec((1, tm, tk), lambda b, i, j, k: (b, i, k)),
                  pl.BlockSpec((tk, tn), lambda b, i, j, k: (k, j)),
                  pl.BlockSpec((1, tm, tn), lambda b, i, j, k: (b, i, j)),
                  pl.BlockSpec((1, 2, tn), lambda b, i, j, k: (b, 0, j))],
        out_specs=pl.BlockSpec((1, tm, tn), lambda b, i, j, k: (b, i, j)),
        out_shape=jax.ShapeDtypeStruct((B, T, D), F32),
        scratch_shapes=[pltpu.VMEM((tm, tn), F32)],
        compiler_params=_cparams(("parallel", "parallel", "parallel", "arbitrary")),
        name="mm_res",
    )(a, w, x, gate2)


def _merge_kernel(ym_ref, yr_ref, yn_ref, g0_ref, g1_ref, g2_ref, wb_ref, o_ref):
    acc = _sigmoid(g0_ref[0]) * _dot(ym_ref[0], wb_ref[0])
    acc += _sigmoid(g1_ref[0]) * _dot(yr_ref[0], wb_ref[1])
    acc += _sigmoid(g2_ref[0]) * _dot(yn_ref[0], wb_ref[2])
    o_ref[0] = acc.astype(o_ref.dtype)


def _merge(ym, yr, yn, p, wb):
    B, T, W = ym.shape
    D = wb.shape[2]
    tm = _tile(T, 640, 16)
    tn = _tile(D, 512, 128)
    yspec = pl.BlockSpec((1, tm, W), lambda b, i, j: (b, i, 0))

    def gspec(n):
        base = (GATE_0 + n * D) // tn
        return pl.BlockSpec((1, tm, tn), lambda b, i, j: (b, i, base + j))

    return pl.pallas_call(
        _merge_kernel,
        grid=(B, T // tm, D // tn),
        in_specs=[yspec, yspec, yspec, gspec(0), gspec(1), gspec(2),
                  pl.BlockSpec((N_BRANCH, W, tn), lambda b, i, j: (0, 0, j))],
        out_specs=pl.BlockSpec((1, tm, tn), lambda b, i, j: (b, i, j)),
        out_shape=jax.ShapeDtypeStruct((B, T, D), BF16),
        compiler_params=_cparams(("parallel", "parallel", "arbitrary")),
        name="merge",
    )(ym, yr, yn, p, p, p, wb)


def _final_norm_kernel(x_ref, g_ref, o_ref):
    x = x_ref[0]
    o_ref[0] = x * lax.rsqrt(jnp.mean(x * x, axis=-1, keepdims=True) + EPS) * g_ref[...]


def _final_norm(xs, g, ctx_len):
    B, T, D = xs.shape
    seq = T - ctx_len
    tm = _tile(math.gcd(seq, ctx_len), 256, 8)
    off = ctx_len // tm
    return pl.pallas_call(
        _final_norm_kernel,
        grid=(B, seq // tm),
        in_specs=[pl.BlockSpec((1, tm, D), lambda b, i: (b, off + i, 0)),
                  pl.BlockSpec((1, D), lambda b, i: (0, 0))],
        out_specs=pl.BlockSpec((1, tm, D), lambda b, i: (b, i, 0)),
        out_shape=jax.ShapeDtypeStruct((B, seq, D), F32),
        compiler_params=_cparams(("parallel", "parallel")),
        name="final_norm",
    )(xs, g.reshape(1, D))


def _chunk_index(d, s, n_ctx, n_all):
    back = jnp.where(s < n_ctx, n_ctx - 1 - s, n_all + n_ctx - 1 - s)
    return jnp.where(d == 0, s, back)


def _rope(x, cos, sin_signed, first_half):
    swapped = jnp.where(first_half, pltpu.roll(x, 96, 1), pltpu.roll(x, 32, 1))
    return x * cos + swapped * sin_signed


def _log_sigmoid(x):
    return jnp.minimum(x, 0.0) - jnp.log(1.0 + jnp.exp(-jnp.abs(x)))


def _mlstm_kernel(bias_ref, *refs):
    L = CHUNK
    n_in = 7
    dir_refs = (refs[:n_in], refs[n_in:2 * n_in])
    out_refs = refs[2 * n_in:2 * n_in + 2]
    ct_ref, m_ref = refs[2 * n_in + 2:]

    @pl.when(pl.program_id(1) == 0)
    def _():
        ct_ref[...] = jnp.zeros_like(ct_ref)
        m_ref[...] = jnp.zeros_like(m_ref)

    row = lax.broadcasted_iota(jnp.int32, (L, L), 0)
    col = lax.broadcasted_iota(jnp.int32, (L, L), 1)
    lane = lax.broadcasted_iota(jnp.int32, (L, ML_DQK), 1)
    first_half = (lane % 64) < 32
    ones = jnp.ones((L, 128), BF16)

    def st_scores(s):
        d, h = s['d'], s['h']
        q_ref, k_ref, v_ref, gc_ref, gr_ref, cos_ref, sin_ref = dir_refs[d]
        b_i = bias_ref[8 * d + h]
        b_f = bias_ref[8 * d + 4 + h]
        ig_c = gc_ref[0, 0, 0, 0, h] + b_i
        lf_c = _log_sigmoid(gc_ref[0, 0, 0, 1, h] + b_f)
        gr = gr_ref[0, 0, 0, h]
        ig_r = gr[0:1, :] + b_i
        lf_r = _log_sigmoid(gr[1:2, :] + b_f)
        incl = (col <= row) if d == 0 else (col >= row)
        incl_t = (row <= col) if d == 0 else (row >= col)
        b_c = jnp.sum(jnp.where(incl, lf_r, 0.0), axis=1, keepdims=True)
        b_r = jnp.sum(jnp.where(incl_t, lf_c, 0.0), axis=0, keepdims=True)
        b_end = jnp.sum(lf_r, axis=1, keepdims=True)
        a_c = b_end - b_c + ig_c
        a_r = b_end - b_r + ig_r
        a_max = jnp.max(a_r, axis=1, keepdims=True)
        cos, sin = cos_ref[...], sin_ref[...]
        hq = slice(h * ML_DQK, (h + 1) * ML_DQK)
        q = _rope(q_ref[0, :, hq] * (ML_DQK ** -0.5), cos, sin, first_half).astype(BF16)
        k = _rope(k_ref[0, :, hq], cos, sin, first_half).astype(BF16)
        v = v_ref[0, :, h * ML_DV:(h + 1) * ML_DV]
        m0 = m_ref[s['c']]
        log_intra = jnp.where(incl, b_c - b_r + ig_r, NEG)
        log_inter = b_c + m0
        m_j = jnp.maximum(log_inter, jnp.max(log_intra, axis=1, keepdims=True))
        s['decay'] = jnp.exp(log_intra - m_j)
        s['w_inter'] = jnp.exp(log_inter - m_j)
        s['floor'] = jnp.exp(-m_j)
        s['m_new'] = jnp.maximum(b_end + m0, a_max)
        s['s_old'] = jnp.exp(b_end + m0 - s['m_new'])
        s['s_loc'] = jnp.exp(a_max - s['m_new'])
        s['q'], s['k'] = q, k
        s['vext'] = jnp.concatenate([v.astype(BF16), ones], axis=1)
        s['vw'] = jnp.concatenate([v * jnp.exp(a_c - a_max), jnp.broadcast_to(jnp.exp(a_c - a_max), (L, 128))],
                                  axis=1).astype(BF16)
        s['qk'] = _dot_nt(q, k)

    def st_pv(s):
        smat = (s.pop('qk') * s.pop('decay')).astype(BF16)
        ct0 = ct_ref[s['c']]
        s['num'] = _dot(smat, s.pop('vext')) + s.pop('w_inter') * _dot(s.pop('q'), ct0.astype(BF16))
        s['ct'] = s.pop('s_old') * ct0 + s.pop('s_loc') * _dot_tn(s.pop('k'), s.pop('vw'))

    def st_out(s):
        num = s.pop('num')
        den = jnp.maximum(jnp.abs(num[:, ML_DV:]), s.pop('floor'))
        h = s['h']
        out_refs[s['d']][0, :, h * ML_DV:(h + 1) * ML_DV] = num[:, :ML_DV] / jnp.concatenate([den, den], axis=1)
        ct_ref[s['c']] = s.pop('ct')
        m_ref[s['c']] = s.pop('m_new')

    states = [{'d': d, 'h': h, 'c': d * ML_HEADS + h} for d in range(2) for h in range(ML_HEADS)]
    for stage in (st_scores, st_pv, st_out):
        for s in states:
            stage(s)


def _mlstm(p, gate_b, gates, cos, sin, ctx_len):
    B, T, _ = p.shape
    L = CHUNK
    nch = T // L
    nctx = ctx_len // L
    H = ML_HEADS
    g4 = gates.reshape(B, nch, L, 2, 2, H)
    gcol = jnp.transpose(g4, (3, 0, 1, 4, 5, 2))[..., None]
    grow = jnp.transpose(g4, (3, 0, 1, 5, 4, 2))

    def specs(d):
        ch = lambda s: _chunk_index(d, s, nctx, nch)
        return [pl.BlockSpec((1, L, H * ML_DQK), lambda b, s: (b, ch(s), ML_Q0 // (H * ML_DQK))),
                pl.BlockSpec((1, L, H * ML_DQK), lambda b, s: (b, ch(s), ML_K0 // (H * ML_DQK))),
                pl.BlockSpec((1, L, H * ML_DV), lambda b, s: (b, ch(s), ML_V0 // (H * ML_DV))),
                pl.BlockSpec((1, 1, 1, 2, H, L, 1), lambda b, s: (d, b, ch(s), 0, 0, 0, 0)),
                pl.BlockSpec((1, 1, 1, H, 2, L), lambda b, s: (d, b, ch(s), 0, 0, 0)),
                pl.BlockSpec((L, ML_DQK), lambda b, s: (ch(s), 0)),
                pl.BlockSpec((L, ML_DQK), lambda b, s: (ch(s), 0))]

    def ospec(d):
        return pl.BlockSpec((1, L, BRANCH_W), lambda b, s: (b, _chunk_index(d, s, nctx, nch), 0))

    hshape = jax.ShapeDtypeStruct((B, T, BRANCH_W), F32)
    args = (p, p, p, gcol, grow, cos, sin)
    return pl.pallas_call(
        _mlstm_kernel,
        grid=(B, nch),
        in_specs=[pl.BlockSpec(memory_space=pltpu.SMEM)] + specs(0) + specs(1),
        out_specs=[ospec(0), ospec(1)],
        out_shape=[hshape, hshape],
        scratch_shapes=[pltpu.VMEM((2 * H, ML_DQK, ML_DV + 128), F32), pltpu.VMEM((2 * H, 1, 1), F32)],
        compiler_params=_cparams(("parallel", "arbitrary")),
        name="mlstm",
    )(gate_b, *args, *args)


def _mlstm_post_kernel(hf_ref, hb_ref, o_ref, g_ref, y_ref):
    h = hf_ref[0] + hb_ref[0]
    h = h * lax.rsqrt(jnp.mean(h * h, axis=-1, keepdims=True) + EPS) * g_ref[...]
    y_ref[0] = (h * _sigmoid(o_ref[0])).astype(y_ref.dtype)


def _mlstm_post(hf, hb, p, norm_g):
    B, T, W = hf.shape
    tm = _tile(T, 1024, 16)
    hspec = pl.BlockSpec((1, tm, ML_DV), lambda b, i, h: (b, i, h))
    return pl.pallas_call(
        _mlstm_post_kernel,
        grid=(B, T // tm, ML_HEADS),
        in_specs=[hspec, hspec,
                  pl.BlockSpec((1, tm, ML_DV), lambda b, i, h: (b, i, ML_O0 // ML_DV + h)),
                  pl.BlockSpec((1, ML_DV), lambda b, i, h: (0, h))],
        out_specs=hspec,
        out_shape=jax.ShapeDtypeStruct((B, T, W), BF16),
        compiler_params=_cparams(("parallel", "parallel", "parallel")),
        name="mlstm_post",
    )(hf, hb, p, norm_g.reshape(1, W))


def _rwkv_prep_kernel(x_ref, xp_ref, xn_ref, mu_ref, w0_ref, a0_ref, wd_ref, wa_ref, wg_ref,
                      r_ref, k_ref, v_ref, ld_ref, ar_ref, g_ref, *, tt, ctx_len, nt):
    i = pl.program_id(1)
    x = x_ref[0][:, :RW_COLS]
    first = (i == 0) | (i * tt == ctx_len)
    last = ((i + 1) * tt == ctx_len) | (i == nt - 1)
    prev_row = jnp.where(first, 0.0, xp_ref[0][7:8, :RW_COLS])
    next_row = jnp.where(last, 0.0, xn_ref[0][0:1, :RW_COLS])
    ridx = lax.broadcasted_iota(jnp.int32, (tt, 1), 0)
    prev = jnp.where(ridx == 0, prev_row, pltpu.roll(x, 1, 0))
    nxt = jnp.where(ridx == tt - 1, next_row, pltpu.roll(x, tt - 1, 0))
    xs = x + mu_ref[...] * (0.5 * (prev + nxt) - x)

    W = BRANCH_W
    r_ref[0] = xs[:, 0:W]
    k_ref[0] = xs[:, W:2 * W]
    v_ref[0] = xs[:, 2 * W:3 * W]
    wlo = _dot(jnp.tanh(xs[:, 3 * W:3 * W + 256]), wd_ref[...], HI)
    alo = _dot(xs[:, 3 * W + 128:3 * W + 384], wa_ref[...], HI)
    for d in range(2):
        w = -_softplus(-(w0_ref[d:d + 1, :] + wlo[:, d * W:(d + 1) * W])) - 0.5
        ld_ref[d, 0] = -jnp.exp(w)
        ar_ref[d, 0] = _sigmoid(a0_ref[d:d + 1, :] + alo[:, d * W:(d + 1) * W])
    g_ref[0] = _dot(_sigmoid(xs[:, 3 * W + 384:3 * W + 640]).astype(BF16), wg_ref[...])


def _rwkv_prep(p, prm, ctx_len):
    B, T, _ = p.shape
    W = BRANCH_W
    tt = _tile(math.gcd(T, ctx_len), 256, 8)
    nt = T // tt
    h8 = tt // 8
    nb8 = T // 8
    z = lambda *s: jnp.zeros(s, F32)
    wd = jnp.concatenate([
        jnp.concatenate([prm['rw_w_up'][0], z(RW_LORA, W)], axis=1),
        jnp.concatenate([z(RW_LORA, W), prm['rw_w_up'][1]], axis=1),
        z(256 - 2 * RW_LORA, 2 * W)], axis=0)
    wa = jnp.concatenate([
        z(2 * RW_LORA - 128, 2 * W),
        jnp.concatenate([prm['rw_a_up'][0], z(RW_LORA, W)], axis=1),
        jnp.concatenate([z(RW_LORA, W), prm['rw_a_up'][1]], axis=1)], axis=0)
    wg = prm['rw_g_up'].astype(BF16)
    mu = prm['rw_mu'].reshape(1, RW_COLS)
    kern = functools.partial(_rwkv_prep_kernel, tt=tt, ctx_len=ctx_len, nt=nt)
    rwblk = RW_0 // RW_SPAN
    tok = jax.ShapeDtypeStruct((B, T, W), F32)
    tok2 = jax.ShapeDtypeStruct((2, B, T, W), F32)
    full = lambda shape: pl.BlockSpec(shape, lambda b, i: (0,) * len(shape))
    ospec = pl.BlockSpec((1, tt, W), lambda b, i: (b, i, 0))
    ospec2 = pl.BlockSpec((2, 1, tt, W), lambda b, i: (0, b, i, 0))
    return pl.pallas_call(
        kern,
        grid=(B, nt),
        in_specs=[pl.BlockSpec((1, tt, RW_SPAN), lambda b, i: (b, i, rwblk)),
                  pl.BlockSpec((1, 8, RW_SPAN), lambda b, i: (b, jnp.maximum(i * h8 - 1, 0), rwblk)),
                  pl.BlockSpec((1, 8, RW_SPAN), lambda b, i: (b, jnp.minimum((i + 1) * h8, nb8 - 1), rwblk)),
                  full((1, RW_COLS)), full((2, W)), full((2, W)),
                  full((256, 2 * W)), full((256, 2 * W)), full((256, W))],
        out_specs=[ospec, ospec, ospec, ospec2, ospec2, ospec],
        out_shape=[tok, tok, tok, tok2, tok2, tok],
        compiler_params=_cparams(("parallel", "parallel")),
        name="rwkv_prep",
    )(p, p, p, mu, prm['rw_w0'], prm['rw_a0'], wd, wa, wg)


def _split2(x):
    hi = x.astype(BF16)
    return hi, (x - hi.astype(F32)).astype(BF16)


def _split3(x):
    hi = x.astype(BF16)
    r1 = x - hi.astype(F32)
    mid = r1.astype(BF16)
    return hi, mid, (r1 - mid.astype(F32)).astype(BF16)


def _mm3(a, b, f=_dot):
    return f(a[0], b[0]) + f(a[0], b[1]) + f(a[1], b[0])


def _mm_exact(e, x3, f=_dot):
    return f(e, x3[0]) + f(e, x3[1]) + f(e, x3[2])


def _rwkv_scan_kernel(r_ref, k_ref, v_ref, ld_ref, ar_ref, kk_ref, ka_ref, y_ref, st_ref, *, n_pairs):
    L = CHUNK
    P = 2 * RW_DH
    d = pl.program_id(0)

    @pl.when(pl.program_id(2) == 0)
    def _():
        st_ref[...] = jnp.zeros_like(st_ref)

    row = lax.broadcasted_iota(jnp.int32, (L, P), 0)
    col = lax.broadcasted_iota(jnp.int32, (L, P), 1) % RW_DH
    diff = (row - col) * jnp.where(d == 0, 1, -1)
    before = diff > 0
    incl = diff >= 0
    tri = incl[:, :L].astype(BF16)
    eye = (row == col).astype(F32)
    ones = jnp.ones((L, P), BF16)
    lane = lax.broadcasted_iota(jnp.int32, (1, P), 1)
    m0 = lane < RW_DH
    r2 = lax.broadcasted_iota(jnp.int32, (P, P), 0) // RW_DH
    c2 = lax.broadcasted_iota(jnp.int32, (P, P), 1) // RW_DH
    same_head = r2 == c2
    ones_bd = same_head.astype(BF16)

    def bd(x):
        z = jnp.zeros_like(x)
        return jnp.concatenate([jnp.where(m0, x, z), jnp.where(m0, z, x)], axis=0)

    def bd2(x):
        return tuple(bd(t) for t in x)

    def cat(xs, axis):
        return tuple(jnp.concatenate(list(t), axis=axis) for t in zip(*xs))

    def st_load(s):
        sl = s['sl']
        s['R'], s['K'], s['V'] = r_ref[0, :, sl], k_ref[0, :, sl], v_ref[0, :, sl]
        s['LD'], s['AR'] = ld_ref[0, 0, :, sl], ar_ref[0, 0, :, sl]
        s['S0'] = st_ref[s['j']]
        s['kk'] = s['K'] * kk_ref[:, sl]
        s['nrm2'] = _mm_exact(ones_bd, _split3(s['kk'] * s['kk']), lambda e, x: _dot(x, e))

    def st_cum(s):
        s['kk'] = s['kk'] / jnp.maximum(jnp.sqrt(s.pop('nrm2')), 1e-12)
        s['KD'] = s.pop('K') * (1.0 + (s['AR'] - 1.0) * ka_ref[:, s['sl']])
        s['BV'] = s['kk'] * s.pop('AR')
        s['LD3'] = _split3(s['LD'])
        s['c'] = _mm_exact(tri, s['LD3'])

    def st_x(s):
        c, LD = s.pop('c'), s.pop('LD')
        c_end = jnp.sum(LD, axis=0, keepdims=True)
        At = -s.pop('kk') * jnp.exp(c - LD)
        e_neg = jnp.exp(-c)
        BV, KD = s.pop('BV'), s.pop('KD')
        s['Rt'] = s.pop('R') * jnp.exp(c)
        e_end = jnp.exp(c_end - c)
        s['Bh_s'], s['Kh_s'] = _split2(BV * e_end), _split2(KD * e_end)
        s['At_s'], s['V_s'] = _split2(At), _split2(s.pop('V'))
        s['X'] = _mm3(cat([s['At_s'], _split2(s['Rt'])], 0),
                      cat([bd2(_split2(BV * e_neg)), bd2(_split2(KD * e_neg))], 0), _dot_nt)

    def st_inv0(s):
        X = s.pop('X')
        Nab = jnp.where(before, X[:L, :P], 0.0)
        s['Nak_s'] = _split2(jnp.where(before, X[:L, P:], 0.0))
        s['Mrb_s'] = _split2(jnp.where(incl, X[L:, :P], 0.0))
        s['Mrk_s'] = _split2(jnp.where(incl, X[L:, P:], 0.0))
        s['T'] = eye + Nab
        Np_s = _split2(Nab)
        s['Z'] = _mm3(Np_s, bd2(Np_s))
        s['NV'] = _mm3(s.pop('Nak_s'), bd2(s['V_s']))

    def st_inv1(s):
        Z = s.pop('Z')
        if Z.shape[0] == 2 * L:
            s['T'] = s['T'] + Z[L:]
        Np_s = _split2(Z[:L])
        s['Z'] = _mm3(cat([Np_s, _split2(s['T'])], 0), bd2(Np_s))

    def st_inv2(s):
        Z = s.pop('Z')
        s['T'] = s['T'] + Z[L:]
        s['Z'] = _mm3(_split2(s['T']), bd2(_split2(Z[:L])))

    def st_wu(s):
        T_s = _split2(s.pop('T') + s.pop('Z'))
        s['WU'] = _mm3(T_s, cat([bd2(s.pop('At_s')), bd2(_split2(s.pop('NV')))], 1))

    def st_loc(s):
        WU = s.pop('WU')
        W_s, Ul_s = _split2(WU[:, :P]), _split2(WU[:, P:])
        MM = _mm3(s.pop('Mrb_s'), cat([bd2(W_s), bd2(Ul_s)], 1))
        s['Q_s'] = _split2(s.pop('Rt') + MM[:, :P])
        s['Yl'] = MM[:, P:] + _mm3(s.pop('Mrk_s'), bd2(s['V_s']))
        PS = _mm3(s.pop('Bh_s'), cat([W_s, Ul_s], 1), _dot_tn)
        s['Pm_s'] = _split2(jnp.where(same_head, PS[:, :P], 0.0))
        s['Sl'] = jnp.where(same_head, PS[:, P:] + _mm3(s.pop('Kh_s'), s.pop('V_s'), _dot_tn), 0.0)
        s['dcol'] = jnp.exp(_mm_exact(ones, s.pop('LD3'), lambda e, x: _dot_tn(x, e)))

    def st_out(s):
        S0 = s.pop('S0')
        S0_s = _split2(S0)
        y_ref[0, 0, :, s['sl']] = _mm3(s.pop('Q_s'), S0_s) + s.pop('Yl')
        st_ref[s['j']] = s.pop('dcol') * S0 + _mm3(s.pop('Pm_s'), S0_s) + s.pop('Sl')

    n_dbl = int(math.log2(L)) - 2
    stages = [st_load, st_cum, st_x, st_inv0] + [st_inv1] * n_dbl + [st_inv2, st_wu, st_loc, st_out]
    states = [{'j': j, 'sl': slice(P * j, P * (j + 1))} for j in range(n_pairs)]
    for stage in stages:
        for s in states:
            stage(s)


def _rwkv_scan(r, k, v, ld, ar, k_k, k_a, ctx_len):
    B, T, W = r.shape
    L = CHUNK
    nch = T // L
    nctx = ctx_len // L

    def smap(d, b, s):
        return (b, _chunk_index(d, s, nctx, nch), 0)

    def dmap(d, b, s):
        return (d, b, _chunk_index(d, s, nctx, nch), 0)

    pspec = pl.BlockSpec((1, W), lambda d, b, s: (0, 0))
    kern = functools.partial(_rwkv_scan_kernel, n_pairs=W // (2 * RW_DH))
    return pl.pallas_call(
        kern,
        grid=(2, B, nch),
        in_specs=[pl.BlockSpec((1, L, W), smap), pl.BlockSpec((1, L, W), smap), pl.BlockSpec((1, L, W), smap),
                  pl.BlockSpec((1, 1, L, W), dmap), pl.BlockSpec((1, 1, L, W), dmap),
                  pspec, pspec],
        out_specs=pl.BlockSpec((1, 1, L, W), dmap),
        out_shape=jax.ShapeDtypeStruct((2, B, T, W), F32),
        scratch_shapes=[pltpu.VMEM((W // (2 * RW_DH), 2 * RW_DH, 2 * RW_DH), F32)],
        compiler_params=_cparams(("parallel", "parallel", "arbitrary")),
        name="rwkv_scan",
    )(r, k, v, ld, ar, k_k.reshape(1, W), k_a.reshape(1, W))


def _rwkv_post_kernel(yf_ref, yb_ref, r_ref, k_ref, v_ref, g_ref, rk_ref, lw_ref, lb_ref, o_ref):
    P = 2 * RW_DH
    r2 = lax.broadcasted_iota(jnp.int32, (P, P), 0) // RW_DH
    c2 = lax.broadcasted_iota(jnp.int32, (P, P), 1) // RW_DH
    ones_bd = (r2 == c2).astype(BF16)

    def head_sum(x):
        return _mm_exact(ones_bd, _split3(x), lambda e, t: _dot(t, e))

    for j in range(o_ref.shape[2] // P):
        sl = slice(P * j, P * (j + 1))
        y = yf_ref[0, 0, :, sl] + yb_ref[0, 0, :, sl]
        yc = y - head_sum(y) * (1.0 / RW_DH)
        var = head_sum(yc * yc) * (1.0 / RW_DH)
        yn = yc * lax.rsqrt(var + RW_GN_EPS)
        bonus = head_sum(r_ref[0, :, sl] * k_ref[0, :, sl] * rk_ref[:, sl]) * v_ref[0, :, sl]
        o_ref[0, :, sl] = ((yn * lw_ref[:, sl] + lb_ref[:, sl] + bonus) * g_ref[0, :, sl]).astype(o_ref.dtype)


def _rwkv_post(y2, r, k, v, g, r_k, ln_w, ln_b):
    _, B, T, W = y2.shape
    tt = _tile(T, 512, 16)
    yspec = lambda d: pl.BlockSpec((1, 1, tt, W), lambda b, i: (d, b, i, 0))
    tspec = pl.BlockSpec((1, tt, W), lambda b, i: (b, i, 0))
    pspec = pl.BlockSpec((1, W), lambda b, i: (0, 0))
    return pl.pallas_call(
        _rwkv_post_kernel,
        grid=(B, T // tt),
        in_specs=[yspec(0), yspec(1), tspec, tspec, tspec, tspec, pspec, pspec, pspec],
        out_specs=tspec,
        out_shape=jax.ShapeDtypeStruct((B, T, W), BF16),
        compiler_params=_cparams(("parallel", "parallel")),
        name="rwkv_post",
    )(y2, y2, r, k, v, g, r_k.reshape(1, W), ln_w.reshape(1, W), ln_b.reshape(1, W))


def _rwkv_branch(p, prm, ctx_len):
    r, k, v, ld, ar, g = _rwkv_prep(p, prm, ctx_len)
    y2 = _rwkv_scan(r, k, v, ld, ar, prm['rw_k_k'], prm['rw_k_a'], ctx_len)
    return _rwkv_post(y2, r, k, v, g, prm['rw_r_k'], prm['rw_ln_w'], prm['rw_ln_b'])


NA_RB = 4
NA_SLAB = NA_RB + NA_WIN_R - 1


def _na_kernel(q_ref, k_ref, v_ref, bias_ref, o_ref, *, ctx_len, rows):
    rb = pl.program_id(2)
    scale = NA_DH ** -0.5
    q = q_ref[0].astype(BF16)
    k_ctx = k_ref[0, 0:ctx_len, :].astype(BF16)
    v_ctx = v_ref[0, 0:ctx_len, :].astype(BF16)
    s_ctx = _dot_nt(q, k_ctx) * scale
    m_ctx = jnp.max(s_ctx, axis=-1, keepdims=True)

    @pl.when(rb == 0)
    def _():
        p_ctx = jnp.exp(s_ctx - m_ctx)
        y = _dot(p_ctx.astype(BF16), v_ctx)
        o_ref[0] = (y / jnp.sum(p_ctx, axis=-1, keepdims=True)).astype(o_ref.dtype)

    @pl.when(rb > 0)
    def _():
        nslab = NA_SLAB * GRID_W
        start = jnp.clip((rb - 1) * NA_RB - NA_WIN_R // 2, 0, rows - NA_SLAB)
        koff = pl.multiple_of(ctx_len + start * GRID_W, GRID_W)
        k_win = k_ref[0, pl.ds(koff, nslab), :].astype(BF16)
        v_win = v_ref[0, pl.ds(koff, nslab), :].astype(BF16)
        s_win = _dot_nt(q, k_win) * scale + bias_ref[0, 0]
        m = jnp.maximum(jnp.max(s_win, axis=-1, keepdims=True), m_ctx)
        p_win = jnp.exp(s_win - m)
        p_ctx = jnp.exp(s_ctx - m)
        den = jnp.sum(p_win, axis=-1, keepdims=True) + jnp.sum(p_ctx, axis=-1, keepdims=True)
        y = _dot(p_win.astype(BF16), v_win) + _dot(p_ctx.astype(BF16), v_ctx)
        o_ref[0] = (y / den).astype(o_ref.dtype)


def _na_block_type(lb, nblk):
    return jnp.where(lb <= 0, 0, jnp.where(lb == nblk - 1, 2, 1))


def _na_bias_table(rpb, rows):
    c = jnp.arange(GRID_W)
    c_start = jnp.clip(c - NA_WIN_C // 2, 0, GRID_W - NA_WIN_C)
    col_ok = (c[None, :] >= c_start[:, None]) & (c[None, :] < c_start[:, None] + NA_WIN_C)
    col_off = jnp.clip(c[None, :] - c[:, None], 1 - NA_WIN_C, NA_WIN_C - 1) + (NA_WIN_C - 1)
    toe = jnp.where(col_ok[None, None], rpb[:, :, col_off], NEG)
    nblk = rows // NA_RB
    reps = np.array([0, min(1, nblk - 1), nblk - 1])
    r = reps[:, None] * NA_RB + np.arange(NA_RB)[None, :]
    slab0 = np.clip(reps * NA_RB - NA_WIN_R // 2, 0, rows - NA_SLAB)
    kr = slab0[:, None] + np.arange(NA_SLAB)[None, :]
    win0 = np.clip(r - NA_WIN_R // 2, 0, rows - NA_WIN_R)
    ro = kr[:, None, :] - r[:, :, None] + (NA_WIN_R - 1)
    ok = (kr[:, None, :] >= win0[:, :, None]) & (kr[:, None, :] < win0[:, :, None] + NA_WIN_R)
    tab = toe[:, np.clip(ro, 0, 2 * NA_WIN_R - 2)]
    tab = jnp.where(jnp.asarray(ok)[None, :, :, :, None, None], tab, NEG)
    tab = jnp.transpose(tab, (0, 1, 2, 4, 3, 5))
    return tab.reshape(rpb.shape[0], 3, NA_RB * GRID_W, NA_SLAB * GRID_W)


def _na_branch(p, rpb, ctx_len):
    B, T, _ = p.shape
    seq = T - ctx_len
    rows = seq // GRID_W
    H = NA_HEADS
    nq = NA_RB * GRID_W
    assert ctx_len == nq and rows % NA_RB == 0 and rows >= NA_SLAB, (ctx_len, rows)
    nblk = rows // NA_RB
    qb, kb, vb = NA_0 // NA_DH, NA_0 // NA_DH + H, NA_0 // NA_DH + 2 * H
    table = _na_bias_table(rpb, rows)
    kern = functools.partial(_na_kernel, ctx_len=ctx_len, rows=rows)
    return pl.pallas_call(
        kern,
        grid=(B, H, 1 + nblk),
        in_specs=[pl.BlockSpec((1, nq, NA_DH), lambda b, h, r: (b, r, qb + h)),
                  pl.BlockSpec((1, T, NA_DH), lambda b, h, r: (b, 0, kb + h)),
                  pl.BlockSpec((1, T, NA_DH), lambda b, h, r: (b, 0, vb + h)),
                  pl.BlockSpec((1, 1, nq, NA_SLAB * GRID_W),
                               lambda b, h, r: (h, _na_block_type(r - 1, nblk), 0, 0))],
        out_specs=pl.BlockSpec((1, nq, NA_DH), lambda b, h, r: (b, r, h)),
        out_shape=jax.ShapeDtypeStruct((B, T, BRANCH_W), BF16),
        compiler_params=_cparams(("parallel", "parallel", "arbitrary")),
        name="na",
    )(p, p, p, table)


def _pack_w_in(w_in):
    ml, rw, na, gt = 3088, 3712, 3072, 6144
    o_rw, o_na, o_gt = ml, ml + rw, ml + rw + na
    pad = jnp.zeros((w_in.shape[0], P_COLS - MLG_0 - 16), w_in.dtype)
    return jnp.concatenate([w_in[:, 0:3072], w_in[:, o_na:o_na + na], w_in[:, o_gt:o_gt + gt],
                            w_in[:, o_rw:o_rw + rw], w_in[:, 3072:3088], pad], axis=1).astype(BF16)


def _rope_tables(seq, ctx_len):
    t = jnp.arange(seq)
    nf = ML_DQK // 4
    freqs = ROPE_BASE ** (-jnp.arange(nf, dtype=F32) / nf)
    ang_r = (t // GRID_W).astype(F32)[:, None] * freqs[None, :]
    ang_c = (t % GRID_W).astype(F32)[:, None] * freqs[None, :]
    cos = jnp.concatenate([jnp.cos(ang_r)] * 2 + [jnp.cos(ang_c)] * 2, axis=1)
    sin = jnp.concatenate([-jnp.sin(ang_r), jnp.sin(ang_r), -jnp.sin(ang_c), jnp.sin(ang_c)], axis=1)
    cos = jnp.concatenate([jnp.ones((ctx_len, ML_DQK), F32), cos], axis=0)
    sin = jnp.concatenate([jnp.zeros((ctx_len, ML_DQK), F32), sin], axis=0)
    return cos, sin


def kernel(x, c, ctx, c_ctx, w_mod, b_mod, norm1_g, w_in, ml_gate_b, ml_norm_g, rw_mu, rw_w0, rw_w_up, rw_a0, rw_a_up, rw_g_up, rw_k_k, rw_k_a, rw_r_k, rw_ln_w, rw_ln_b, na_rpb, w_branch, w_out, norm2_g, w_ff1, w_ff2, final_g):
    B, seq, D = x.shape
    ctx_len = ctx.shape[1]
    depth = w_mod.shape[0]
    xs = jnp.concatenate([ctx, x], axis=1)
    s_all = jax.nn.silu(jnp.concatenate([c_ctx[None, :], c], axis=0))
    s_all = jnp.concatenate([s_all, jnp.zeros((8 - (B + 1) % 8, D), F32)], axis=0) if (B + 1) % 8 else s_all
    cos, sin = _rope_tables(seq, ctx_len)
    for l in range(depth):
        with_ctx = l < depth - 1
        mod = _modulation(s_all, w_mod[l], b_mod[l]).reshape(-1, 6, D)
        mc = jnp.broadcast_to(mod[0:1], (B, 6, D))
        ml_ = mod[1:B + 1]
        mod1 = jnp.stack([mc[:, 0], mc[:, 1], ml_[:, 0], ml_[:, 1]], axis=1)
        gate1 = jnp.stack([mc[:, 2], ml_[:, 2]], axis=1)
        mod2 = jnp.stack([mc[:, 3], mc[:, 4], ml_[:, 3], ml_[:, 4]], axis=1)
        gate2 = jnp.stack([mc[:, 5], ml_[:, 5]], axis=1)
        prm = {'rw_mu': rw_mu[l], 'rw_w0': rw_w0[l], 'rw_w_up': rw_w_up[l], 'rw_a0': rw_a0[l],
               'rw_a_up': rw_a_up[l], 'rw_g_up': rw_g_up[l], 'rw_k_k': rw_k_k[l], 'rw_k_a': rw_k_a[l],
               'rw_r_k': rw_r_k[l], 'rw_ln_w': rw_ln_w[l], 'rw_ln_b': rw_ln_b[l]}

        p = _mm_norm(xs, norm1_g[l], mod1, _pack_w_in(w_in[l]), ctx_len, F32, False)
        h_f, h_b = _mlstm(p, ml_gate_b[l], p[:, :, MLG_0:MLG_0 + 16], cos, sin, ctx_len)
        y_m = _mlstm_post(h_f, h_b, p, ml_norm_g[l])
        y_r = _rwkv_branch(p, prm, ctx_len)
        y_n = _na_branch(p, na_rpb[l], ctx_len)
        merged = _merge(y_m, y_r, y_n, p, w_branch[l].astype(BF16))
        xs = _mm_res(merged, w_out[l].astype(BF16), xs, gate1, ctx_len)
        hff = _mm_norm(xs, norm2_g[l], mod2, w_ff1[l].astype(BF16), ctx_len, BF16, True)
        xs = _mm_res(hff, w_ff2[l].astype(BF16), xs, gate2, ctx_len)
    return _final_norm(xs, final_g, ctx_len)
```

```python
import functools
import math

import jax
import jax.numpy as jnp
import numpy as np
from jax import lax
from jax.experimental import pallas as pl
from jax.experimental.pallas import tpu as pltpu

F32 = jnp.float32
BF16 = jnp.bfloat16
HI = lax.Precision.HIGHEST

D_MODEL = 2048
GRID_W = 64
BRANCH_W = D_MODEL // 2
N_BRANCH = 3
ML_HEADS = 4
ML_DV = BRANCH_W // ML_HEADS
ML_DQK = ML_DV // 2
RW_DH = 64
RW_HEADS = BRANCH_W // RW_DH
RW_LORA = 96
RW_GATE_LORA = 256
RW_GN_EPS = 64e-5
NA_DH = 128
NA_HEADS = BRANCH_W // NA_DH
NA_WIN_R = 8
NA_WIN_C = 16
D_FF = 4 * D_MODEL
ROPE_BASE = 10000.0
EPS = 1e-6
CHUNK = 64
NEG = -1e30

ML_Q0, ML_K0, ML_V0, ML_O0 = 0, 512, 1024, 2048
NA_0 = 3072
GATE_0 = 6144
RW_0 = 12288
RW_SPAN = 4096
RW_COLS = 3712
MLG_0 = RW_0 + RW_COLS
P_COLS = RW_0 + RW_SPAN

VMEM_LIMIT = 56 * 1024 * 1024
MM_ROWS = 1100


def _cparams(sem):
    return pltpu.CompilerParams(dimension_semantics=sem, vmem_limit_bytes=VMEM_LIMIT)


def _tile(n, cap, mult):
    best = None
    for t in range(mult, min(n, cap) + 1, mult):
        if n % t == 0:
            best = t
    assert best is not None, (n, cap, mult)
    return best


def _dot(a, b, prec=None):
    return jnp.dot(a, b, preferred_element_type=F32, precision=prec)


def _dot_nt(a, b, prec=None):
    return lax.dot_general(a, b, (((1,), (1,)), ((), ())), preferred_element_type=F32, precision=prec)


def _dot_tn(a, b, prec=None):
    return lax.dot_general(a, b, (((0,), (0,)), ((), ())), preferred_element_type=F32, precision=prec)


def _sigmoid(x):
    return 1.0 / (1.0 + jnp.exp(-x))


def _softplus(x):
    return jnp.maximum(x, 0.0) + jnp.log(1.0 + jnp.exp(-jnp.abs(x)))


def _mod_kernel(s_ref, w_ref, b_ref, o_ref):
    o_ref[...] = _dot(s_ref[...], w_ref[...], HI) + b_ref[...]


def _modulation(s, w, b):
    R, D = s.shape
    N = w.shape[1]
    tn = _tile(N, 1024, 128)
    return pl.pallas_call(
        _mod_kernel,
        grid=(N // tn,),
        in_specs=[pl.BlockSpec((R, D), lambda j: (0, 0)),
                  pl.BlockSpec((D, tn), lambda j: (0, j)),
                  pl.BlockSpec((1, tn), lambda j: (0, j))],
        out_specs=pl.BlockSpec((R, tn), lambda j: (0, j)),
        out_shape=jax.ShapeDtypeStruct((R, N), F32),
        compiler_params=_cparams(("arbitrary",)),
        name="modulation",
    )(s, w, b.reshape(1, N))


def _mm_norm_kernel(x_ref, g_ref, mod_ref, w_ref, o_ref, h_scr, *, tm, ctx_len, relu2):
    i = pl.program_id(1)

    @pl.when(pl.program_id(2) == 0)
    def _():
        x = x_ref[0]
        y = x * lax.rsqrt(jnp.mean(x * x, axis=-1, keepdims=True) + EPS) * g_ref[...]
        row = i * tm + lax.broadcasted_iota(jnp.int32, (tm, 1), 0)
        is_ctx = row < ctx_len
        m = mod_ref[0]
        shift = jnp.where(is_ctx, m[0:1], m[2:3])
        scale = jnp.where(is_ctx, m[1:2], m[3:4])
        h_scr[...] = (y * (1.0 + scale) + shift).astype(BF16)

    acc = _dot(h_scr[...], w_ref[...])
    if relu2:
        acc = jnp.square(jnp.maximum(acc, 0.0))
    o_ref[0] = acc.astype(o_ref.dtype)


def _mm_norm(x, g, mod4, w, ctx_len, out_dtype, relu2):
    B, T, D = x.shape
    N = w.shape[1]
    tm = _tile(T, MM_ROWS // 2, 16)
    tn = _tile(N, 1024, 128)
    kern = functools.partial(_mm_norm_kernel, tm=tm, ctx_len=ctx_len, relu2=relu2)
    return pl.pallas_call(
        kern,
        grid=(B, T // tm, N // tn),
        in_specs=[pl.BlockSpec((1, tm, D), lambda b, i, j: (b, i, 0)),
                  pl.BlockSpec((1, D), lambda b, i, j: (0, 0)),
                  pl.BlockSpec((1, 4, D), lambda b, i, j: (b, 0, 0)),
                  pl.BlockSpec((D, tn), lambda b, i, j: (0, j))],
        out_specs=pl.BlockSpec((1, tm, tn), lambda b, i, j: (b, i, j)),
        out_shape=jax.ShapeDtypeStruct((B, T, N), out_dtype),
        scratch_shapes=[pltpu.VMEM((tm, D), BF16)],
        compiler_params=_cparams(("parallel", "parallel", "arbitrary")),
        name="mm_norm",
    )(x, g.reshape(1, D), mod4, w)


def _mm_res_kernel(a_ref, w_ref, x_ref, gate_ref, o_ref, acc_ref, *, tm, ctx_len, nk):
    i = pl.program_id(1)
    k = pl.program_id(3)

    @pl.when(k == 0)
    def _():
        acc_ref[...] = jnp.zeros_like(acc_ref)

    acc_ref[...] += _dot(a_ref[0], w_ref[...])

    @pl.when(k == nk - 1)
    def _():
        row = i * tm + lax.broadcasted_iota(jnp.int32, (tm, 1), 0)
        gt = gate_ref[0]
        gate = jnp.where(row < ctx_len, gt[0:1], gt[1:2])
        o_ref[0] = x_ref[0] + gate * acc_ref[...]


def _mm_res(a, w, x, gate2, ctx_len):
    B, T, K = a.shape
    D = w.shape[1]
    tm = _tile(T, MM_ROWS, 16)
    tn = _tile(D, 1024, 128)
    tk = _tile(K, 2048, 128)
    nk = K // tk
    kern = functools.partial(_mm_res_kernel, tm=tm, ctx_len=ctx_len, nk=nk)
    return pl.pallas_call(
        kern,
        grid=(B, T // tm, D // tn, nk),
        in_specs=[pl.BlockSpec((1, tm, tk), lambda b, i, j, k: (b, i, k)),
                  pl.BlockSpec((tk, tn), lambda b, i, j, k: (k, j)),
                  pl.BlockSpec((1, tm, tn), lambda b, i, j, k: (b, i, j)),
                  pl.BlockSpec((1, 2, tn), lambda b, i, j, k: (b, 0, j))],
        out_specs=pl.BlockSpec((1, tm, tn), lambda b, i, j, k: (b, i, j)),
        out_shape=jax.ShapeDtypeStruct((B, T, D), F32),
        scratch_shapes=[pltpu.VMEM((tm, tn), F32)],
        compiler_params=_cparams(("parallel", "parallel", "parallel", "arbitrary")),
        name="mm_res",
    )(a, w, x, gate2)


def _merge_kernel(ym_ref, yr_ref, yn_ref, g0_ref, g1_ref, g2_ref, wb_ref, o_ref):
    acc = _sigmoid(g0_ref[0]) * _dot(ym_ref[0], wb_ref[0])
    acc += _sigmoid(g1_ref[0]) * _dot(yr_ref[0], wb_ref[1])
    acc += _sigmoid(g2_ref[0]) * _dot(yn_ref[0], wb_ref[2])
    o_ref[0] = acc.astype(o_ref.dtype)


def _merge(ym, yr, yn, p, wb):
    B, T, W = ym.shape
    D = wb.shape[2]
    tm = _tile(T, MM_ROWS, 16)
    tn = _tile(D, 512, 128)
    yspec = pl.BlockSpec((1, tm, W), lambda b, i, j: (b, i, 0))

    def gspec(n):
        base = (GATE_0 + n * D) // tn
        return pl.BlockSpec((1, tm, tn), lambda b, i, j: (b, i, base + j))

    return pl.pallas_call(
        _merge_kernel,
        grid=(B, T // tm, D // tn),
        in_specs=[yspec, yspec, yspec, gspec(0), gspec(1), gspec(2),
                  pl.BlockSpec((N_BRANCH, W, tn), lambda b, i, j: (0, 0, j))],
        out_specs=pl.BlockSpec((1, tm, tn), lambda b, i, j: (b, i, j)),
        out_shape=jax.ShapeDtypeStruct((B, T, D), BF16),
        compiler_params=_cparams(("parallel", "parallel", "arbitrary")),
        name="merge",
    )(ym, yr, yn, p, p, p, wb)


def _final_norm_kernel(x_ref, g_ref, o_ref):
    x = x_ref[0]
    o_ref[0] = x * lax.rsqrt(jnp.mean(x * x, axis=-1, keepdims=True) + EPS) * g_ref[...]


def _final_norm(xs, g, ctx_len):
    B, T, D = xs.shape
    seq = T - ctx_len
    tm = _tile(math.gcd(seq, ctx_len), 256, 8)
    off = ctx_len // tm
    return pl.pallas_call(
        _final_norm_kernel,
        grid=(B, seq // tm),
        in_specs=[pl.BlockSpec((1, tm, D), lambda b, i: (b, off + i, 0)),
                  pl.BlockSpec((1, D), lambda b, i: (0, 0))],
        out_specs=pl.BlockSpec((1, tm, D), lambda b, i: (b, i, 0)),
        out_shape=jax.ShapeDtypeStruct((B, seq, D), F32),
        compiler_params=_cparams(("parallel", "parallel")),
        name="final_norm",
    )(xs, g.reshape(1, D))


def _chunk_index(d, s, n_ctx, n_all):
    back = jnp.where(s < n_ctx, n_ctx - 1 - s, n_all + n_ctx - 1 - s)
    return jnp.where(d == 0, s, back)


def _rope(x, cos, sin_signed, first_half):
    swapped = jnp.where(first_half, pltpu.roll(x, 96, 1), pltpu.roll(x, 32, 1))
    return x * cos + swapped * sin_signed


def _log_sigmoid(x):
    return jnp.minimum(x, 0.0) - jnp.log(1.0 + jnp.exp(-jnp.abs(x)))


def _mlstm_kernel(bias_ref, *refs):
    L = CHUNK
    n_in = 7
    dir_refs = (refs[:n_in], refs[n_in:2 * n_in])
    out_refs = refs[2 * n_in:2 * n_in + 2]
    ct_ref, m_ref = refs[2 * n_in + 2:]

    @pl.when(pl.program_id(1) == 0)
    def _():
        ct_ref[...] = jnp.zeros_like(ct_ref)
        m_ref[...] = jnp.zeros_like(m_ref)

    row = lax.broadcasted_iota(jnp.int32, (L, L), 0)
    col = lax.broadcasted_iota(jnp.int32, (L, L), 1)
    lane = lax.broadcasted_iota(jnp.int32, (L, ML_DQK), 1)
    first_half = (lane % 64) < 32
    ones = jnp.ones((L, 128), BF16)

    def st_qk(s):
        d, h = s['d'], s['h']
        q_ref, k_ref, v_ref, _, _, cos_ref, sin_ref = dir_refs[d]
        cos, sin = cos_ref[...], sin_ref[...]
        hq = slice(h * ML_DQK, (h + 1) * ML_DQK)
        s['q'] = _rope(q_ref[0, :, hq] * (ML_DQK ** -0.5), cos, sin, first_half).astype(BF16)
        s['k'] = _rope(k_ref[0, :, hq], cos, sin, first_half).astype(BF16)
        s['v'] = v_ref[0, :, h * ML_DV:(h + 1) * ML_DV]
        s['qk'] = _dot_nt(s['q'], s['k'])

    def st_gate(s):
        d, h = s['d'], s['h']
        gc_ref, gr_ref = dir_refs[d][3:5]
        b_i = bias_ref[8 * d + h]
        b_f = bias_ref[8 * d + 4 + h]
        s['ig_c'] = gc_ref[0, 0, 0, 0, h] + b_i
        lf_c = _log_sigmoid(gc_ref[0, 0, 0, 1, h] + b_f)
        gr = gr_ref[0, 0, 0, h]
        s['ig_r'] = gr[0:1, :] + b_i
        lf_r = _log_sigmoid(gr[1:2, :] + b_f)
        s['incl'] = (col <= row) if d == 0 else (col >= row)
        incl_t = (row <= col) if d == 0 else (row >= col)
        s['b_c'] = jnp.sum(jnp.where(s['incl'], lf_r, 0.0), axis=1, keepdims=True)
        s['b_r'] = jnp.sum(jnp.where(incl_t, lf_c, 0.0), axis=0, keepdims=True)
        s['b_end'] = jnp.sum(lf_r, axis=1, keepdims=True)

    def st_max(s):
        b_c, b_r, b_end, ig_r = s['b_c'], s.pop('b_r'), s['b_end'], s.pop('ig_r')
        s['a_c'] = b_end - b_c + s.pop('ig_c')
        s['a_max'] = jnp.max(b_end - b_r + ig_r, axis=1, keepdims=True)
        s['log_intra'] = jnp.where(s.pop('incl'), b_c - b_r + ig_r, NEG)
        s['mx'] = jnp.max(s['log_intra'], axis=1, keepdims=True)

    def st_exp(s):
        m0 = m_ref[s['c']]
        b_end, a_max = s.pop('b_end'), s.pop('a_max')
        log_inter = s.pop('b_c') + m0
        m_j = jnp.maximum(log_inter, s.pop('mx'))
        s['decay'] = jnp.exp(s.pop('log_intra') - m_j)
        s['w_inter'] = jnp.exp(log_inter - m_j)
        s['floor'] = jnp.exp(-m_j)
        s['m_new'] = jnp.maximum(b_end + m0, a_max)
        s['s_old'] = jnp.exp(b_end + m0 - s['m_new'])
        s['s_loc'] = jnp.exp(a_max - s['m_new'])
        v = s.pop('v')
        wgt = jnp.exp(s.pop('a_c') - a_max)
        s['vext'] = jnp.concatenate([v.astype(BF16), ones], axis=1)
        s['vw'] = jnp.concatenate([v * wgt, jnp.broadcast_to(wgt, (L, 128))], axis=1).astype(BF16)

    def st_pv(s):
        smat = (s.pop('qk') * s.pop('decay')).astype(BF16)
        ct0 = ct_ref[s['c']]
        s['num'] = _dot(smat, s.pop('vext')) + s.pop('w_inter') * _dot(s.pop('q'), ct0.astype(BF16))
        s['ct'] = s.pop('s_old') * ct0 + s.pop('s_loc') * _dot_tn(s.pop('k'), s.pop('vw'))

    def st_out(s):
        num = s.pop('num')
        den = jnp.maximum(jnp.abs(num[:, ML_DV:]), s.pop('floor'))
        h = s['h']
        out_refs[s['d']][0, :, h * ML_DV:(h + 1) * ML_DV] = num[:, :ML_DV] / jnp.concatenate([den, den], axis=1)
        ct_ref[s['c']] = s.pop('ct')
        m_ref[s['c']] = s.pop('m_new')

    states = [{'d': d, 'h': h, 'c': d * ML_HEADS + h} for d in range(2) for h in range(ML_HEADS)]
    for stage in (st_qk, st_gate, st_max, st_exp, st_pv, st_out):
        for s in states:
            stage(s)


def _mlstm(p, gate_b, gates, cos, sin, ctx_len):
    B, T, _ = p.shape
    L = CHUNK
    nch = T // L
    nctx = ctx_len // L
    H = ML_HEADS
    g4 = gates.reshape(B, nch, L, 2, 2, H)
    gcol = jnp.transpose(g4, (3, 0, 1, 4, 5, 2))[..., None]
    grow = jnp.transpose(g4, (3, 0, 1, 5, 4, 2))

    def specs(d):
        ch = lambda s: _chunk_index(d, s, nctx, nch)
        return [pl.BlockSpec((1, L, H * ML_DQK), lambda b, s: (b, ch(s), ML_Q0 // (H * ML_DQK))),
                pl.BlockSpec((1, L, H * ML_DQK), lambda b, s: (b, ch(s), ML_K0 // (H * ML_DQK))),
                pl.BlockSpec((1, L, H * ML_DV), lambda b, s: (b, ch(s), ML_V0 // (H * ML_DV))),
                pl.BlockSpec((1, 1, 1, 2, H, L, 1), lambda b, s: (d, b, ch(s), 0, 0, 0, 0)),
                pl.BlockSpec((1, 1, 1, H, 2, L), lambda b, s: (d, b, ch(s), 0, 0, 0)),
                pl.BlockSpec((L, ML_DQK), lambda b, s: (ch(s), 0)),
                pl.BlockSpec((L, ML_DQK), lambda b, s: (ch(s), 0))]

    def ospec(d):
        return pl.BlockSpec((1, L, BRANCH_W), lambda b, s: (b, _chunk_index(d, s, nctx, nch), 0))

    hshape = jax.ShapeDtypeStruct((B, T, BRANCH_W), F32)
    args = (p, p, p, gcol, grow, cos, sin)
    return pl.pallas_call(
        _mlstm_kernel,
        grid=(B, nch),
        in_specs=[pl.BlockSpec(memory_space=pltpu.SMEM)] + specs(0) + specs(1),
        out_specs=[ospec(0), ospec(1)],
        out_shape=[hshape, hshape],
        scratch_shapes=[pltpu.VMEM((2 * H, ML_DQK, ML_DV + 128), F32), pltpu.VMEM((2 * H, 1, 1), F32)],
        compiler_params=_cparams(("parallel", "arbitrary")),
        name="mlstm",
    )(gate_b, *args, *args)


def _mlstm_post_kernel(hf_ref, hb_ref, o_ref, g_ref, y_ref):
    h = hf_ref[0] + hb_ref[0]
    h = h * lax.rsqrt(jnp.mean(h * h, axis=-1, keepdims=True) + EPS) * g_ref[...]
    y_ref[0] = (h * _sigmoid(o_ref[0])).astype(y_ref.dtype)


def _mlstm_post(hf, hb, p, norm_g):
    B, T, W = hf.shape
    tm = _tile(T, 1024, 16)
    hspec = pl.BlockSpec((1, tm, ML_DV), lambda b, i, h: (b, i, h))
    return pl.pallas_call(
        _mlstm_post_kernel,
        grid=(B, T // tm, ML_HEADS),
        in_specs=[hspec, hspec,
                  pl.BlockSpec((1, tm, ML_DV), lambda b, i, h: (b, i, ML_O0 // ML_DV + h)),
                  pl.BlockSpec((1, ML_DV), lambda b, i, h: (0, h))],
        out_specs=hspec,
        out_shape=jax.ShapeDtypeStruct((B, T, W), BF16),
        compiler_params=_cparams(("parallel", "parallel", "parallel")),
        name="mlstm_post",
    )(hf, hb, p, norm_g.reshape(1, W))


def _rwkv_prep_kernel(x_ref, xp_ref, xn_ref, mu_ref, w0_ref, a0_ref, wdh_ref, wdl_ref, wa_ref, wg_ref,
                      r_ref, k_ref, v_ref, ld_ref, ar_ref, g_ref, *, tt, ctx_len, nt):
    i = pl.program_id(1)
    x = x_ref[0][:, :RW_COLS]
    first = (i == 0) | (i * tt == ctx_len)
    last = ((i + 1) * tt == ctx_len) | (i == nt - 1)
    prev_row = jnp.where(first, 0.0, xp_ref[0][7:8, :RW_COLS])
    next_row = jnp.where(last, 0.0, xn_ref[0][0:1, :RW_COLS])
    ridx = lax.broadcasted_iota(jnp.int32, (tt, 1), 0)
    prev = jnp.where(ridx == 0, prev_row, pltpu.roll(x, 1, 0))
    nxt = jnp.where(ridx == tt - 1, next_row, pltpu.roll(x, tt - 1, 0))
    xs = x + mu_ref[...] * (0.5 * (prev + nxt) - x)

    W = BRANCH_W
    r_ref[0] = xs[:, 0:W]
    k_ref[0] = xs[:, W:2 * W]
    v_ref[0] = xs[:, 2 * W:3 * W]
    wlo = _mm3(_split2(jnp.tanh(xs[:, 3 * W:3 * W + 256])), (wdh_ref[...], wdl_ref[...]))
    alo = _dot(xs[:, 3 * W + 128:3 * W + 384].astype(BF16), wa_ref[...])
    for d in range(2):
        w = -_softplus(-(w0_ref[d:d + 1, :] + wlo[:, d * W:(d + 1) * W])) - 0.5
        ld_ref[d, 0] = -jnp.exp(w)
        ar_ref[d, 0] = _sigmoid(a0_ref[d:d + 1, :] + alo[:, d * W:(d + 1) * W])
    g_ref[0] = _dot(_sigmoid(xs[:, 3 * W + 384:3 * W + 640]).astype(BF16), wg_ref[...])


def _rwkv_prep(p, prm, ctx_len):
    B, T, _ = p.shape
    W = BRANCH_W
    tt = _tile(math.gcd(T, ctx_len), 256, 8)
    nt = T // tt
    h8 = tt // 8
    nb8 = T // 8
    z = lambda *s: jnp.zeros(s, F32)
    wd = jnp.concatenate([
        jnp.concatenate([prm['rw_w_up'][0], z(RW_LORA, W)], axis=1),
        jnp.concatenate([z(RW_LORA, W), prm['rw_w_up'][1]], axis=1),
        z(256 - 2 * RW_LORA, 2 * W)], axis=0)
    wa = jnp.concatenate([
        z(2 * RW_LORA - 128, 2 * W),
        jnp.concatenate([prm['rw_a_up'][0], z(RW_LORA, W)], axis=1),
        jnp.concatenate([z(RW_LORA, W), prm['rw_a_up'][1]], axis=1)], axis=0)
    wg = prm['rw_g_up'].astype(BF16)
    wdh, wdl = _split2(wd)
    wa = wa.astype(BF16)
    mu = prm['rw_mu'].reshape(1, RW_COLS)
    kern = functools.partial(_rwkv_prep_kernel, tt=tt, ctx_len=ctx_len, nt=nt)
    rwblk = RW_0 // RW_SPAN
    tok = jax.ShapeDtypeStruct((B, T, W), F32)
    tok2 = jax.ShapeDtypeStruct((2, B, T, W), F32)
    full = lambda shape: pl.BlockSpec(shape, lambda b, i: (0,) * len(shape))
    ospec = pl.BlockSpec((1, tt, W), lambda b, i: (b, i, 0))
    ospec2 = pl.BlockSpec((2, 1, tt, W), lambda b, i: (0, b, i, 0))
    return pl.pallas_call(
        kern,
        grid=(B, nt),
        in_specs=[pl.BlockSpec((1, tt, RW_SPAN), lambda b, i: (b, i, rwblk)),
                  pl.BlockSpec((1, 8, RW_SPAN), lambda b, i: (b, jnp.maximum(i * h8 - 1, 0), rwblk)),
                  pl.BlockSpec((1, 8, RW_SPAN), lambda b, i: (b, jnp.minimum((i + 1) * h8, nb8 - 1), rwblk)),
                  full((1, RW_COLS)), full((2, W)), full((2, W)),
                  full((256, 2 * W)), full((256, 2 * W)), full((256, 2 * W)), full((256, W))],
        out_specs=[ospec, ospec, ospec, ospec2, ospec2, ospec],
        out_shape=[tok, tok, tok, tok2, tok2, tok],
        compiler_params=_cparams(("parallel", "parallel")),
        name="rwkv_prep",
    )(p, p, p, mu, prm['rw_w0'], prm['rw_a0'], wdh, wdl, wa, wg)


def _split2(x):
    hi = x.astype(BF16)
    return hi, (x - hi.astype(F32)).astype(BF16)


def _split3(x):
    hi = x.astype(BF16)
    r1 = x - hi.astype(F32)
    mid = r1.astype(BF16)
    return hi, mid, (r1 - mid.astype(F32)).astype(BF16)


def _split1(x):
    return (x.astype(BF16),)


def _mm3(a, b, f=_dot):
    acc = f(a[0], b[0])
    if len(b) > 1:
        acc = acc + f(a[0], b[1])
    if len(a) > 1:
        acc = acc + f(a[1], b[0])
    return acc


def _mm_exact(e, x3, f=_dot):
    return f(e, x3[0]) + f(e, x3[1]) + f(e, x3[2])


def _rwkv_scan_kernel(r_ref, k_ref, v_ref, ld_ref, ar_ref, kk_ref, ka_ref, y_ref, st_ref, *, n_pairs):
    L = CHUNK
    P = 2 * RW_DH
    d = pl.program_id(0)

    @pl.when(pl.program_id(2) == 0)
    def _():
        st_ref[...] = jnp.zeros_like(st_ref)

    row = lax.broadcasted_iota(jnp.int32, (L, P), 0)
    col = lax.broadcasted_iota(jnp.int32, (L, P), 1) % RW_DH
    diff = (row - col) * jnp.where(d == 0, 1, -1)
    before = diff > 0
    incl = diff >= 0
    tri = incl[:, :L].astype(BF16)
    eye = (row == col).astype(F32)
    ones = jnp.ones((L, P), BF16)
    lane = lax.broadcasted_iota(jnp.int32, (1, P), 1)
    m0 = lane < RW_DH
    r2 = lax.broadcasted_iota(jnp.int32, (P, P), 0) // RW_DH
    c2 = lax.broadcasted_iota(jnp.int32, (P, P), 1) // RW_DH
    same_head = r2 == c2
    ones_bd = same_head.astype(BF16)

    def bd(x):
        z = jnp.zeros_like(x)
        return jnp.concatenate([jnp.where(m0, x, z), jnp.where(m0, z, x)], axis=0)

    def bd2(x):
        return tuple(bd(t) for t in x)

    sp = spi = _split1

    def cat(xs, axis):
        return tuple(jnp.concatenate(list(t), axis=axis) for t in zip(*xs))

    def st_load(s):
        sl = s['sl']
        s['R'], s['K'], s['V'] = r_ref[0, :, sl], k_ref[0, :, sl], v_ref[0, :, sl]
        s['LD'], s['AR'] = ld_ref[0, 0, :, sl], ar_ref[0, 0, :, sl]
        s['S0'] = st_ref[s['j']]
        s['kk'] = s['K'] * kk_ref[:, sl]
        s['nrm2'] = _mm_exact(ones_bd, _split3(s['kk'] * s['kk']), lambda e, x: _dot(x, e))

    def st_cum(s):
        s['kk'] = s['kk'] / jnp.maximum(jnp.sqrt(s.pop('nrm2')), 1e-12)
        s['KD'] = s.pop('K') * (1.0 + (s['AR'] - 1.0) * ka_ref[:, s['sl']])
        s['BV'] = s['kk'] * s.pop('AR')
        s['LD3'] = _split3(s['LD'])
        s['c'] = _mm_exact(tri, s['LD3'])

    def st_x(s):
        c, LD = s.pop('c'), s.pop('LD')
        c_end = jnp.sum(LD, axis=0, keepdims=True)
        At = -s.pop('kk') * jnp.exp(c - LD)
        e_neg = jnp.exp(-c)
        BV, KD = s.pop('BV'), s.pop('KD')
        s['Rt'] = s.pop('R') * jnp.exp(c)
        e_end = jnp.exp(c_end - c)
        s['Bh_s'], s['Kh_s'] = sp(BV * e_end), sp(KD * e_end)
        s['At_s'], s['V_s'] = sp(At), sp(s.pop('V'))
        s['X'] = _mm3(cat([s['At_s'], sp(s['Rt'])], 0),
                      cat([bd2(sp(BV * e_neg)), bd2(sp(KD * e_neg))], 0), _dot_nt)

    def st_inv0(s):
        X = s.pop('X')
        Nab = jnp.where(before, X[:L, :P], 0.0)
        s['Nak_s'] = sp(jnp.where(before, X[:L, P:], 0.0))
        s['Mrb_s'] = sp(jnp.where(incl, X[L:, :P], 0.0))
        s['Mrk_s'] = sp(jnp.where(incl, X[L:, P:], 0.0))
        s['T'] = eye + Nab
        Np_s = spi(Nab)
        s['Z'] = _mm3(Np_s, bd2(Np_s))
        s['NV'] = _mm3(s.pop('Nak_s'), bd2(s['V_s']))

    def st_inv1(s):
        Z = s.pop('Z')
        if Z.shape[0] == 2 * L:
            s['T'] = s['T'] + Z[L:]
        Np_s = spi(Z[:L])
        s['Z'] = _mm3(cat([Np_s, spi(s['T'])], 0), bd2(Np_s))

    def st_inv2(s):
        Z = s.pop('Z')
        s['T'] = s['T'] + Z[L:]
        s['Z'] = _mm3(spi(s['T']), bd2(spi(Z[:L])))

    def st_wu(s):
        T_s = sp(s.pop('T') + s.pop('Z'))
        s['WU'] = _mm3(T_s, cat([bd2(s.pop('At_s')), bd2(sp(s.pop('NV')))], 1))

    def st_loc(s):
        WU = s.pop('WU')
        W_s, Ul_s = sp(WU[:, :P]), sp(WU[:, P:])
        MM = _mm3(s.pop('Mrb_s'), cat([bd2(W_s), bd2(Ul_s)], 1))
        s['Q_s'] = sp(s.pop('Rt') + MM[:, :P])
        s['Yl'] = MM[:, P:] + _mm3(s.pop('Mrk_s'), bd2(s['V_s']))
        PS = _mm3(s.pop('Bh_s'), cat([W_s, Ul_s], 1), _dot_tn)
        s['Pm_s'] = sp(jnp.where(same_head, PS[:, :P], 0.0))
        s['Sl'] = jnp.where(same_head, PS[:, P:] + _mm3(s.pop('Kh_s'), s.pop('V_s'), _dot_tn), 0.0)
        s['dcol'] = jnp.exp(_mm_exact(ones, s.pop('LD3'), lambda e, x: _dot_tn(x, e)))

    def st_out(s):
        S0 = s.pop('S0')
        S0_s = sp(S0)
        y_ref[0, 0, :, s['sl']] = _mm3(s.pop('Q_s'), S0_s) + s.pop('Yl')
        st_ref[s['j']] = s.pop('dcol') * S0 + _mm3(s.pop('Pm_s'), S0_s) + s.pop('Sl')

    n_dbl = int(math.log2(L)) - 2
    stages = [st_load, st_cum, st_x, st_inv0] + [st_inv1] * n_dbl + [st_inv2, st_wu, st_loc, st_out]
    states = [{'j': j, 'sl': slice(P * j, P * (j + 1))} for j in range(n_pairs)]
    for stage in stages:
        for s in states:
            stage(s)


def _rwkv_scan(r, k, v, ld, ar, k_k, k_a, ctx_len):
    B, T, W = r.shape
    L = CHUNK
    nch = T // L
    nctx = ctx_len // L

    def smap(d, b, s):
        return (b, _chunk_index(d, s, nctx, nch), 0)

    def dmap(d, b, s):
        return (d, b, _chunk_index(d, s, nctx, nch), 0)

    pspec = pl.BlockSpec((1, W), lambda d, b, s: (0, 0))
    kern = functools.partial(_rwkv_scan_kernel, n_pairs=W // (2 * RW_DH))
    return pl.pallas_call(
        kern,
        grid=(2, B, nch),
        in_specs=[pl.BlockSpec((1, L, W), smap), pl.BlockSpec((1, L, W), smap), pl.BlockSpec((1, L, W), smap),
                  pl.BlockSpec((1, 1, L, W), dmap), pl.BlockSpec((1, 1, L, W), dmap),
                  pspec, pspec],
        out_specs=pl.BlockSpec((1, 1, L, W), dmap),
        out_shape=jax.ShapeDtypeStruct((2, B, T, W), F32),
        scratch_shapes=[pltpu.VMEM((W // (2 * RW_DH), 2 * RW_DH, 2 * RW_DH), F32)],
        compiler_params=_cparams(("parallel", "parallel", "arbitrary")),
        name="rwkv_scan",
    )(r, k, v, ld, ar, k_k.reshape(1, W), k_a.reshape(1, W))


def _rwkv_post_kernel(yf_ref, yb_ref, r_ref, k_ref, v_ref, g_ref, rk_ref, lw_ref, lb_ref, o_ref):
    P = 2 * RW_DH
    r2 = lax.broadcasted_iota(jnp.int32, (P, P), 0) // RW_DH
    c2 = lax.broadcasted_iota(jnp.int32, (P, P), 1) // RW_DH
    ones_bd = (r2 == c2).astype(BF16)

    def head_sum(x):
        return _mm_exact(ones_bd, _split3(x), lambda e, t: _dot(t, e))

    for j in range(o_ref.shape[2] // P):
        sl = slice(P * j, P * (j + 1))
        y = yf_ref[0, 0, :, sl] + yb_ref[0, 0, :, sl]
        yc = y - head_sum(y) * (1.0 / RW_DH)
        var = head_sum(yc * yc) * (1.0 / RW_DH)
        yn = yc * lax.rsqrt(var + RW_GN_EPS)
        bonus = head_sum(r_ref[0, :, sl] * k_ref[0, :, sl] * rk_ref[:, sl]) * v_ref[0, :, sl]
        o_ref[0, :, sl] = ((yn * lw_ref[:, sl] + lb_ref[:, sl] + bonus) * g_ref[0, :, sl]).astype(o_ref.dtype)


def _rwkv_post(y2, r, k, v, g, r_k, ln_w, ln_b):
    _, B, T, W = y2.shape
    tt = _tile(T, 512, 16)
    yspec = lambda d: pl.BlockSpec((1, 1, tt, W), lambda b, i: (d, b, i, 0))
    tspec = pl.BlockSpec((1, tt, W), lambda b, i: (b, i, 0))
    pspec = pl.BlockSpec((1, W), lambda b, i: (0, 0))
    return pl.pallas_call(
        _rwkv_post_kernel,
        grid=(B, T // tt),
        in_specs=[yspec(0), yspec(1), tspec, tspec, tspec, tspec, pspec, pspec, pspec],
        out_specs=tspec,
        out_shape=jax.ShapeDtypeStruct((B, T, W), BF16),
        compiler_params=_cparams(("parallel", "parallel")),
        name="rwkv_post",
    )(y2, y2, r, k, v, g, r_k.reshape(1, W), ln_w.reshape(1, W), ln_b.reshape(1, W))


def _rwkv_branch(p, prm, ctx_len):
    r, k, v, ld, ar, g = _rwkv_prep(p, prm, ctx_len)
    y2 = _rwkv_scan(r, k, v, ld, ar, prm['rw_k_k'], prm['rw_k_a'], ctx_len)
    return _rwkv_post(y2, r, k, v, g, prm['rw_r_k'], prm['rw_ln_w'], prm['rw_ln_b'])


NA_RB = 4
NA_SLAB = NA_RB + NA_WIN_R - 1
NA_HB = 2


def _na_kernel(q_ref, k_ref, v_ref, bias_ref, o_ref, *, ctx_len, rows):
    rb = pl.program_id(2)
    scale = NA_DH ** -0.5
    heads = [slice(i * NA_DH, (i + 1) * NA_DH) for i in range(NA_HB)]

    @pl.when(rb == 0)
    def _():
        for hs in heads:
            s_ctx = _dot_nt(q_ref[0, :, hs].astype(BF16), k_ref[0, 0:ctx_len, hs].astype(BF16)) * scale
            p_ctx = jnp.exp(s_ctx - jnp.max(s_ctx, axis=-1, keepdims=True))
            y = _dot(p_ctx.astype(BF16), v_ref[0, 0:ctx_len, hs].astype(BF16))
            o_ref[0, :, hs] = (y / jnp.sum(p_ctx, axis=-1, keepdims=True)).astype(o_ref.dtype)

    @pl.when(rb > 0)
    def _():
        nslab = NA_SLAB * GRID_W
        start = jnp.clip((rb - 1) * NA_RB - NA_WIN_R // 2, 0, rows - NA_SLAB)
        koff = pl.multiple_of(ctx_len + start * GRID_W, GRID_W)

        def st_scores(s):
            hs = s['hs']
            q = q_ref[0, :, hs].astype(BF16)
            s['s_win'] = _dot_nt(q, k_ref[0, pl.ds(koff, nslab), hs].astype(BF16)) * scale + bias_ref[s['i'], 0]
            s['s_ctx'] = _dot_nt(q, k_ref[0, 0:ctx_len, hs].astype(BF16)) * scale

        def st_softmax(s):
            s_win, s_ctx = s.pop('s_win'), s.pop('s_ctx')
            m = jnp.maximum(jnp.max(s_win, axis=-1, keepdims=True), jnp.max(s_ctx, axis=-1, keepdims=True))
            p_win = jnp.exp(s_win - m)
            p_ctx = jnp.exp(s_ctx - m)
            s['den'] = jnp.sum(p_win, axis=-1, keepdims=True) + jnp.sum(p_ctx, axis=-1, keepdims=True)
            hs = s['hs']
            s['y'] = (_dot(p_win.astype(BF16), v_ref[0, pl.ds(koff, nslab), hs].astype(BF16))
                      + _dot(p_ctx.astype(BF16), v_ref[0, 0:ctx_len, hs].astype(BF16)))

        def st_out(s):
            o_ref[0, :, s['hs']] = (s.pop('y') / s.pop('den')).astype(o_ref.dtype)

        states = [{'i': i, 'hs': hs} for i, hs in enumerate(heads)]
        for stage in (st_scores, st_softmax, st_out):
            for s in states:
                stage(s)


def _na_block_type(lb, nblk):
    return jnp.where(lb <= 0, 0, jnp.where(lb == nblk - 1, 2, 1))


def _na_bias_table(rpb, rows):
    c = jnp.arange(GRID_W)
    c_start = jnp.clip(c - NA_WIN_C // 2, 0, GRID_W - NA_WIN_C)
    col_ok = (c[None, :] >= c_start[:, None]) & (c[None, :] < c_start[:, None] + NA_WIN_C)
    col_off = jnp.clip(c[None, :] - c[:, None], 1 - NA_WIN_C, NA_WIN_C - 1) + (NA_WIN_C - 1)
    toe = jnp.where(col_ok[None, None], rpb[:, :, col_off], NEG)
    nblk = rows // NA_RB
    reps = np.array([0, min(1, nblk - 1), nblk - 1])
    r = reps[:, None] * NA_RB + np.arange(NA_RB)[None, :]
    slab0 = np.clip(reps * NA_RB - NA_WIN_R // 2, 0, rows - NA_SLAB)
    kr = slab0[:, None] + np.arange(NA_SLAB)[None, :]
    win0 = np.clip(r - NA_WIN_R // 2, 0, rows - NA_WIN_R)
    ro = kr[:, None, :] - r[:, :, None] + (NA_WIN_R - 1)
    ok = (kr[:, None, :] >= win0[:, :, None]) & (kr[:, None, :] < win0[:, :, None] + NA_WIN_R)
    tab = toe[:, np.clip(ro, 0, 2 * NA_WIN_R - 2)]
    tab = jnp.where(jnp.asarray(ok)[None, :, :, :, None, None], tab, NEG)
    tab = jnp.transpose(tab, (0, 1, 2, 4, 3, 5))
    return tab.reshape(rpb.shape[0], 3, NA_RB * GRID_W, NA_SLAB * GRID_W)


def _na_branch(p, rpb, ctx_len):
    B, T, _ = p.shape
    seq = T - ctx_len
    rows = seq // GRID_W
    H = NA_HEADS
    nq = NA_RB * GRID_W
    assert ctx_len == nq and rows % NA_RB == 0 and rows >= NA_SLAB, (ctx_len, rows)
    nblk = rows // NA_RB
    wb = NA_HB * NA_DH
    qb, kb, vb = NA_0 // wb, (NA_0 + BRANCH_W) // wb, (NA_0 + 2 * BRANCH_W) // wb
    table = _na_bias_table(rpb, rows)
    kern = functools.partial(_na_kernel, ctx_len=ctx_len, rows=rows)
    return pl.pallas_call(
        kern,
        grid=(B, H // NA_HB, 1 + nblk),
        in_specs=[pl.BlockSpec((1, nq, wb), lambda b, h, r: (b, r, qb + h)),
                  pl.BlockSpec((1, T, wb), lambda b, h, r: (b, 0, kb + h)),
                  pl.BlockSpec((1, T, wb), lambda b, h, r: (b, 0, vb + h)),
                  pl.BlockSpec((NA_HB, 1, nq, NA_SLAB * GRID_W),
                               lambda b, h, r: (h, _na_block_type(r - 1, nblk), 0, 0))],
        out_specs=pl.BlockSpec((1, nq, wb), lambda b, h, r: (b, r, h)),
        out_shape=jax.ShapeDtypeStruct((B, T, BRANCH_W), BF16),
        compiler_params=_cparams(("parallel", "parallel", "arbitrary")),
        name="na",
    )(p, p, p, table)


def _pack_kernel(w_ref, o_ref):
    ml, rw, na, gt = 3088, 3712, 3072, 6144
    o_rw, o_na, o_gt = ml, ml + rw, ml + rw + na
    w = w_ref[...]
    n = w.shape[0]
    o_ref[:, 0:NA_0] = w[:, 0:NA_0].astype(BF16)
    o_ref[:, NA_0:GATE_0] = w[:, o_na:o_na + na].astype(BF16)
    o_ref[:, GATE_0:RW_0] = w[:, o_gt:o_gt + gt].astype(BF16)
    o_ref[:, RW_0:MLG_0] = w[:, o_rw:o_rw + rw].astype(BF16)
    tail = jnp.concatenate([w[:, NA_0:ml], jnp.zeros((n, 128 - (ml - NA_0)), F32)], axis=1)
    o_ref[:, MLG_0:MLG_0 + 128] = tail.astype(BF16)
    o_ref[:, MLG_0 + 128:] = jnp.zeros((n, P_COLS - MLG_0 - 128), BF16)


def _pack_w_in(w_in):
    D, N = w_in.shape
    tr = _tile(D, 64, 16)
    return pl.pallas_call(
        _pack_kernel,
        grid=(D // tr,),
        in_specs=[pl.BlockSpec((tr, N), lambda i: (i, 0))],
        out_specs=pl.BlockSpec((tr, P_COLS), lambda i: (i, 0)),
        out_shape=jax.ShapeDtypeStruct((D, P_COLS), BF16),
        compiler_params=_cparams(("parallel",)),
        name="pack_w_in",
    )(w_in)


def _rope_tables(seq, ctx_len):
    t = jnp.arange(seq)
    nf = ML_DQK // 4
    freqs = ROPE_BASE ** (-jnp.arange(nf, dtype=F32) / nf)
    ang_r = (t // GRID_W).astype(F32)[:, None] * freqs[None, :]
    ang_c = (t % GRID_W).astype(F32)[:, None] * freqs[None, :]
    cos = jnp.concatenate([jnp.cos(ang_r)] * 2 + [jnp.cos(ang_c)] * 2, axis=1)
    sin = jnp.concatenate([-jnp.sin(ang_r), jnp.sin(ang_r), -jnp.sin(ang_c), jnp.sin(ang_c)], axis=1)
    cos = jnp.concatenate([jnp.ones((ctx_len, ML_DQK), F32), cos], axis=0)
    sin = jnp.concatenate([jnp.zeros((ctx_len, ML_DQK), F32), sin], axis=0)
    return cos, sin


def kernel(x, c, ctx, c_ctx, w_mod, b_mod, norm1_g, w_in, ml_gate_b, ml_norm_g, rw_mu, rw_w0, rw_w_up, rw_a0, rw_a_up, rw_g_up, rw_k_k, rw_k_a, rw_r_k, rw_ln_w, rw_ln_b, na_rpb, w_branch, w_out, norm2_g, w_ff1, w_ff2, final_g):
    B, seq, D = x.shape
    ctx_len = ctx.shape[1]
    depth = w_mod.shape[0]
    xs = jnp.concatenate([ctx, x], axis=1)
    s_all = jax.nn.silu(jnp.concatenate([c_ctx[None, :], c], axis=0))
    s_all = jnp.concatenate([s_all, jnp.zeros((8 - (B + 1) % 8, D), F32)], axis=0) if (B + 1) % 8 else s_all
    cos, sin = _rope_tables(seq, ctx_len)
    for l in range(depth):
        with_ctx = l < depth - 1
        mod = _modulation(s_all, w_mod[l], b_mod[l]).reshape(-1, 6, D)
        mc = jnp.broadcast_to(mod[0:1], (B, 6, D))
        ml_ = mod[1:B + 1]
        mod1 = jnp.stack([mc[:, 0], mc[:, 1], ml_[:, 0], ml_[:, 1]], axis=1)
        gate1 = jnp.stack([mc[:, 2], ml_[:, 2]], axis=1)
        mod2 = jnp.stack([mc[:, 3], mc[:, 4], ml_[:, 3], ml_[:, 4]], axis=1)
        gate2 = jnp.stack([mc[:, 5], ml_[:, 5]], axis=1)
        prm = {'rw_mu': rw_mu[l], 'rw_w0': rw_w0[l], 'rw_w_up': rw_w_up[l], 'rw_a0': rw_a0[l],
               'rw_a_up': rw_a_up[l], 'rw_g_up': rw_g_up[l], 'rw_k_k': rw_k_k[l], 'rw_k_a': rw_k_a[l],
               'rw_r_k': rw_r_k[l], 'rw_ln_w': rw_ln_w[l], 'rw_ln_b': rw_ln_b[l]}

        p = _mm_norm(xs, norm1_g[l], mod1, _pack_w_in(w_in[l]), ctx_len, F32, False)
        h_f, h_b = _mlstm(p, ml_gate_b[l], p[:, :, MLG_0:MLG_0 + 16], cos, sin, ctx_len)
        y_m = _mlstm_post(h_f, h_b, p, ml_norm_g[l])
        y_r = _rwkv_branch(p, prm, ctx_len)
        y_n = _na_branch(p, na_rpb[l], ctx_len)
        merged = _merge(y_m, y_r, y_n, p, w_branch[l].astype(BF16))
        xs = _mm_res(merged, w_out[l].astype(BF16), xs, gate1, ctx_len)
        hff = _mm_norm(xs, norm2_g[l], mod2, w_ff1[l].astype(BF16), ctx_len, BF16, True)
        xs = _mm_res(hff, w_ff2[l].astype(BF16), xs, gate2, ctx_len)
    return _final_norm(xs, final_g, ctx_len)
```

```python
import functools
import math

import jax
import jax.numpy as jnp
import numpy as np
from jax import lax
from jax.experimental import pallas as pl
from jax.experimental.pallas import tpu as pltpu

F32 = jnp.float32
BF16 = jnp.bfloat16
HI = lax.Precision.HIGHEST

D_MODEL = 2048
GRID_W = 64
BRANCH_W = D_MODEL // 2
N_BRANCH = 3
ML_HEADS = 4
ML_DV = BRANCH_W // ML_HEADS
ML_DQK = ML_DV // 2
RW_DH = 64
RW_HEADS = BRANCH_W // RW_DH
RW_LORA = 96
RW_GATE_LORA = 256
RW_GN_EPS = 64e-5
NA_DH = 128
NA_HEADS = BRANCH_W // NA_DH
NA_WIN_R = 8
NA_WIN_C = 16
D_FF = 4 * D_MODEL
ROPE_BASE = 10000.0
EPS = 1e-6
CHUNK = 64
NEG = -1e30

ML_Q0, ML_K0, ML_V0, ML_O0 = 0, 512, 1024, 2048
NA_0 = 3072
GATE_0 = 6144
RW_0 = 12288
RW_SPAN = 4096
RW_COLS = 3712
MLG_0 = RW_0 + RW_COLS
P_COLS = RW_0 + RW_SPAN

VMEM_LIMIT = 56 * 1024 * 1024
MM_ROWS = 1100


def _cparams(sem):
    return pltpu.CompilerParams(dimension_semantics=sem, vmem_limit_bytes=VMEM_LIMIT)


def _tile(n, cap, mult):
    best = None
    for t in range(mult, min(n, cap) + 1, mult):
        if n % t == 0:
            best = t
    assert best is not None, (n, cap, mult)
    return best


def _dot(a, b, prec=None):
    return jnp.dot(a, b, preferred_element_type=F32, precision=prec)


def _dot_nt(a, b, prec=None):
    return lax.dot_general(a, b, (((1,), (1,)), ((), ())), preferred_element_type=F32, precision=prec)


def _dot_tn(a, b, prec=None):
    return lax.dot_general(a, b, (((0,), (0,)), ((), ())), preferred_element_type=F32, precision=prec)


def _sigmoid(x):
    return 1.0 / (1.0 + jnp.exp(-x))


def _softplus(x):
    return jnp.maximum(x, 0.0) + jnp.log(1.0 + jnp.exp(-jnp.abs(x)))


def _mod_kernel(s_ref, w_ref, b_ref, o_ref):
    o_ref[...] = _dot(s_ref[...], w_ref[0], HI) + b_ref[0]


def _modulation(s, w, b, l):
    R, D = s.shape
    N = w.shape[2]
    tn = _tile(N, 1024, 128)
    return pl.pallas_call(
        _mod_kernel,
        grid=(N // tn,),
        in_specs=[pl.BlockSpec((R, D), lambda j: (0, 0)),
                  pl.BlockSpec((1, D, tn), lambda j: (l, 0, j)),
                  pl.BlockSpec((1, 1, tn), lambda j: (l, 0, j))],
        out_specs=pl.BlockSpec((R, tn), lambda j: (0, j)),
        out_shape=jax.ShapeDtypeStruct((R, N), F32),
        compiler_params=_cparams(("arbitrary",)),
        name="modulation",
    )(s, w, b.reshape(b.shape[0], 1, N))


def _norm_mod_kernel(x_ref, g_ref, mod_ref, o_ref, *, tm, ctx_len):
    x = x_ref[0]
    y = x * lax.rsqrt(jnp.mean(x * x, axis=-1, keepdims=True) + EPS) * g_ref[...]
    row = pl.program_id(1) * tm + lax.broadcasted_iota(jnp.int32, (tm, 1), 0)
    is_ctx = row < ctx_len
    m = mod_ref[0]
    shift = jnp.where(is_ctx, m[0:1], m[2:3])
    scale = jnp.where(is_ctx, m[1:2], m[3:4])
    o_ref[0] = (y * (1.0 + scale) + shift).astype(o_ref.dtype)


def _norm_mod(x, g, mod4, ctx_len):
    B, T, D = x.shape
    tm = _tile(T, 320, 16)
    kern = functools.partial(_norm_mod_kernel, tm=tm, ctx_len=ctx_len)
    return pl.pallas_call(
        kern,
        grid=(B, T // tm),
        in_specs=[pl.BlockSpec((1, tm, D), lambda b, i: (b, i, 0)),
                  pl.BlockSpec((1, D), lambda b, i: (0, 0)),
                  pl.BlockSpec((1, 4, D), lambda b, i: (b, 0, 0))],
        out_specs=pl.BlockSpec((1, tm, D), lambda b, i: (b, i, 0)),
        out_shape=jax.ShapeDtypeStruct((B, T, D), BF16),
        compiler_params=_cparams(("parallel", "parallel")),
        name="norm_mod",
    )(x, g.reshape(1, D), mod4)


def _mm_kernel(h_ref, w_ref, o_ref, *, relu2):
    acc = _dot(h_ref[0], w_ref[0])
    if relu2:
        acc = jnp.square(jnp.maximum(acc, 0.0))
    o_ref[0] = acc.astype(o_ref.dtype)


def _mm(h, w, l, out_dtype, relu2):
    B, T, D = h.shape
    N = w.shape[2]
    tm = _tile(T, 2 * MM_ROWS, 16)
    tn = _tile(N, 512, 128)
    kern = functools.partial(_mm_kernel, relu2=relu2)
    return pl.pallas_call(
        kern,
        grid=(B, T // tm, N // tn),
        in_specs=[pl.BlockSpec((1, tm, D), lambda b, i, j: (b, i, 0)),
                  pl.BlockSpec((1, D, tn), lambda b, i, j: (l, 0, j))],
        out_specs=pl.BlockSpec((1, tm, tn), lambda b, i, j: (b, i, j)),
        out_shape=jax.ShapeDtypeStruct((B, T, N), out_dtype),
        compiler_params=_cparams(("parallel", "parallel", "arbitrary")),
        name="mm",
    )(h, w)


def _mm_res_kernel(a_ref, w_ref, x_ref, gate_ref, o_ref, acc_ref, *, tm, ctx_len, nk):
    i = pl.program_id(1)
    k = pl.program_id(3)

    @pl.when(k == 0)
    def _():
        acc_ref[...] = jnp.zeros_like(acc_ref)

    acc_ref[...] += _dot(a_ref[0], w_ref[0])

    @pl.when(k == nk - 1)
    def _():
        row = i * tm + lax.broadcasted_iota(jnp.int32, (tm, 1), 0)
        gt = gate_ref[0]
        gate = jnp.where(row < ctx_len, gt[0:1], gt[1:2])
        o_ref[0] = x_ref[0] + gate * acc_ref[...]


def _mm_res(a, w, l, x, gate2, ctx_len):
    B, T, K = a.shape
    D = w.shape[2]
    tm = _tile(T, MM_ROWS, 16)
    tn = _tile(D, 1024, 128)
    tk = _tile(K, 2048, 128)
    nk = K // tk
    kern = functools.partial(_mm_res_kernel, tm=tm, ctx_len=ctx_len, nk=nk)
    return pl.pallas_call(
        kern,
        grid=(B, T // tm, D // tn, nk),
        in_specs=[pl.BlockSpec((1, tm, tk), lambda b, i, j, k: (b, i, k)),
                  pl.BlockSpec((1, tk, tn), lambda b, i, j, k: (l, k, j)),
                  pl.BlockSpec((1, tm, tn), lambda b, i, j, k: (b, i, j)),
                  pl.BlockSpec((1, 2, tn), lambda b, i, j, k: (b, 0, j))],
        out_specs=pl.BlockSpec((1, tm, tn), lambda b, i, j, k: (b, i, j)),
        out_shape=jax.ShapeDtypeStruct((B, T, D), F32),
        scratch_shapes=[pltpu.VMEM((tm, tn), F32)],
        compiler_params=_cparams(("parallel", "parallel", "parallel", "arbitrary")),
        name="mm_res",
    )(a, w, x, gate2)


def _merge_kernel(ym_ref, yr_ref, yn_ref, g0_ref, g1_ref, g2_ref, wb_ref, o_ref):
    acc = _sigmoid(g0_ref[0]) * _dot(ym_ref[0], wb_ref[0, 0])
    acc += _sigmoid(g1_ref[0]) * _dot(yr_ref[0], wb_ref[0, 1])
    acc += _sigmoid(g2_ref[0]) * _dot(yn_ref[0], wb_ref[0, 2])
    o_ref[0] = acc.astype(o_ref.dtype)


def _merge(ym, yr, yn, p, wb, l):
    B, T, W = ym.shape
    D = wb.shape[3]
    tm = _tile(T, MM_ROWS, 16)
    tn = _tile(D, 512, 128)
    yspec = pl.BlockSpec((1, tm, W), lambda b, i, j: (b, i, 0))

    def gspec(n):
        base = (GATE_0 + n * D) // tn
        return pl.BlockSpec((1, tm, tn), lambda b, i, j: (b, i, base + j))

    return pl.pallas_call(
        _merge_kernel,
        grid=(B, T // tm, D // tn),
        in_specs=[yspec, yspec, yspec, gspec(0), gspec(1), gspec(2),
                  pl.BlockSpec((1, N_BRANCH, W, tn), lambda b, i, j: (l, 0, 0, j))],
        out_specs=pl.BlockSpec((1, tm, tn), lambda b, i, j: (b, i, j)),
        out_shape=jax.ShapeDtypeStruct((B, T, D), BF16),
        compiler_params=_cparams(("parallel", "parallel", "arbitrary")),
        name="merge",
    )(ym, yr, yn, p, p, p, wb)


def _final_norm_kernel(x_ref, g_ref, o_ref):
    x = x_ref[0]
    o_ref[0] = x * lax.rsqrt(jnp.mean(x * x, axis=-1, keepdims=True) + EPS) * g_ref[...]


def _final_norm(xs, g, ctx_len):
    B, T, D = xs.shape
    seq = T - ctx_len
    tm = _tile(math.gcd(seq, ctx_len), 256, 8)
    off = ctx_len // tm
    return pl.pallas_call(
        _final_norm_kernel,
        grid=(B, seq // tm),
        in_specs=[pl.BlockSpec((1, tm, D), lambda b, i: (b, off + i, 0)),
                  pl.BlockSpec((1, D), lambda b, i: (0, 0))],
        out_specs=pl.BlockSpec((1, tm, D), lambda b, i: (b, i, 0)),
        out_shape=jax.ShapeDtypeStruct((B, seq, D), F32),
        compiler_params=_cparams(("parallel", "parallel")),
        name="final_norm",
    )(xs, g.reshape(1, D))


def _chunk_index(d, s, n_ctx, n_all):
    back = jnp.where(s < n_ctx, n_ctx - 1 - s, n_all + n_ctx - 1 - s)
    return jnp.where(d == 0, s, back)


def _rope(x, cos, sin_signed, first_half):
    swapped = jnp.where(first_half, pltpu.roll(x, 96, 1), pltpu.roll(x, 32, 1))
    return x * cos + swapped * sin_signed


def _log_sigmoid(x):
    return jnp.minimum(x, 0.0) - jnp.log(1.0 + jnp.exp(-jnp.abs(x)))


def _mlstm_kernel(bias_ref, *refs):
    L = CHUNK
    n_in = 7
    dir_refs = (refs[:n_in], refs[n_in:2 * n_in])
    out_refs = refs[2 * n_in:2 * n_in + 2]
    ct_ref, m_ref = refs[2 * n_in + 2:]

    @pl.when(pl.program_id(1) == 0)
    def _():
        ct_ref[...] = jnp.zeros_like(ct_ref)
        m_ref[...] = jnp.zeros_like(m_ref)

    row = lax.broadcasted_iota(jnp.int32, (L, L), 0)
    col = lax.broadcasted_iota(jnp.int32, (L, L), 1)
    lane = lax.broadcasted_iota(jnp.int32, (L, ML_DQK), 1)
    first_half = (lane % 64) < 32
    ones = jnp.ones((L, 128), BF16)

    def st_qk(s):
        d, h = s['d'], s['h']
        q_ref, k_ref, v_ref, _, _, cos_ref, sin_ref = dir_refs[d]
        cos, sin = cos_ref[...], sin_ref[...]
        hq = slice(h * ML_DQK, (h + 1) * ML_DQK)
        s['q'] = _rope(q_ref[0, :, hq] * (ML_DQK ** -0.5), cos, sin, first_half).astype(BF16)
        s['k'] = _rope(k_ref[0, :, hq], cos, sin, first_half).astype(BF16)
        s['v'] = v_ref[0, :, h * ML_DV:(h + 1) * ML_DV]
        s['qk'] = _dot_nt(s['q'], s['k'])

    def st_gate(s):
        d, h = s['d'], s['h']
        gc_ref, gr_ref = dir_refs[d][3:5]
        b_i = bias_ref[8 * d + h]
        b_f = bias_ref[8 * d + 4 + h]
        ji, jf = 8 * d + h, 8 * d + 4 + h
        gc = gc_ref[0]
        gr = gr_ref[0, 0]
        s['ig_c'] = gc[:, ji:ji + 1] + b_i
        lf_c = _log_sigmoid(gc[:, jf:jf + 1] + b_f)
        s['ig_r'] = gr[ji:ji + 1, :] + b_i
        lf_r = _log_sigmoid(gr[jf:jf + 1, :] + b_f)
        s['incl'] = (col <= row) if d == 0 else (col >= row)
        incl_t = (row <= col) if d == 0 else (row >= col)
        s['b_c'] = jnp.sum(jnp.where(s['incl'], lf_r, 0.0), axis=1, keepdims=True)
        s['b_r'] = jnp.sum(jnp.where(incl_t, lf_c, 0.0), axis=0, keepdims=True)
        s['b_end'] = jnp.sum(lf_r, axis=1, keepdims=True)

    def st_max(s):
        b_c, b_r, b_end, ig_r = s['b_c'], s.pop('b_r'), s['b_end'], s.pop('ig_r')
        s['a_c'] = b_end - b_c + s.pop('ig_c')
        s['a_max'] = jnp.max(b_end - b_r + ig_r, axis=1, keepdims=True)
        s['log_intra'] = jnp.where(s.pop('incl'), b_c - b_r + ig_r, NEG)
        s['mx'] = jnp.max(s['log_intra'], axis=1, keepdims=True)

    def st_exp(s):
        m0 = m_ref[s['c']]
        b_end, a_max = s.pop('b_end'), s.pop('a_max')
        log_inter = s.pop('b_c') + m0
        m_j = jnp.maximum(log_inter, s.pop('mx'))
        s['decay'] = jnp.exp(s.pop('log_intra') - m_j)
        s['w_inter'] = jnp.exp(log_inter - m_j)
        s['floor'] = jnp.exp(-m_j)
        s['m_new'] = jnp.maximum(b_end + m0, a_max)
        s['s_old'] = jnp.exp(b_end + m0 - s['m_new'])
        s['s_loc'] = jnp.exp(a_max - s['m_new'])
        v = s.pop('v')
        wgt = jnp.exp(s.pop('a_c') - a_max)
        s['vext'] = jnp.concatenate([v.astype(BF16), ones], axis=1)
        s['vw'] = jnp.concatenate([v * wgt, jnp.broadcast_to(wgt, (L, 128))], axis=1).astype(BF16)

    def st_pv(s):
        smat = (s.pop('qk') * s.pop('decay')).astype(BF16)
        ct0 = ct_ref[s['c']]
        s['num'] = _dot(smat, s.pop('vext')) + s.pop('w_inter') * _dot(s.pop('q'), ct0.astype(BF16))
        s['ct'] = s.pop('s_old') * ct0 + s.pop('s_loc') * _dot_tn(s.pop('k'), s.pop('vw'))

    def st_out(s):
        num = s.pop('num')
        den = jnp.maximum(jnp.abs(num[:, ML_DV:]), s.pop('floor'))
        h = s['h']
        out_refs[s['d']][0, :, h * ML_DV:(h + 1) * ML_DV] = num[:, :ML_DV] / jnp.concatenate([den, den], axis=1)
        ct_ref[s['c']] = s.pop('ct')
        m_ref[s['c']] = s.pop('m_new')

    states = [{'d': d, 'h': h, 'c': d * ML_HEADS + h} for d in range(2) for h in range(ML_HEADS)]
    for stage in (st_qk, st_gate, st_max, st_exp, st_pv, st_out):
        for s in states:
            stage(s)


def _mlstm(p, gate_b, cos, sin, ctx_len):
    B, T, _ = p.shape
    L = CHUNK
    nch = T // L
    nctx = ctx_len // L
    H = ML_HEADS
    grow = jnp.swapaxes(p[:, :, MLG_0:MLG_0 + 4 * H].reshape(B, nch, L, 4 * H), 2, 3)

    def specs(d):
        ch = lambda s: _chunk_index(d, s, nctx, nch)
        return [pl.BlockSpec((1, L, H * ML_DQK), lambda b, s: (b, ch(s), ML_Q0 // (H * ML_DQK))),
                pl.BlockSpec((1, L, H * ML_DQK), lambda b, s: (b, ch(s), ML_K0 // (H * ML_DQK))),
                pl.BlockSpec((1, L, H * ML_DV), lambda b, s: (b, ch(s), ML_V0 // (H * ML_DV))),
                pl.BlockSpec((1, L, 128), lambda b, s: (b, ch(s), MLG_0 // 128)),
                pl.BlockSpec((1, 1, 4 * H, L), lambda b, s: (b, ch(s), 0, 0)),
                pl.BlockSpec((L, ML_DQK), lambda b, s: (ch(s), 0)),
                pl.BlockSpec((L, ML_DQK), lambda b, s: (ch(s), 0))]

    def ospec(d):
        return pl.BlockSpec((1, L, BRANCH_W), lambda b, s: (b, _chunk_index(d, s, nctx, nch), 0))

    hshape = jax.ShapeDtypeStruct((B, T, BRANCH_W), F32)
    args = (p, p, p, p, grow, cos, sin)
    return pl.pallas_call(
        _mlstm_kernel,
        grid=(B, nch),
        in_specs=[pl.BlockSpec(memory_space=pltpu.SMEM)] + specs(0) + specs(1),
        out_specs=[ospec(0), ospec(1)],
        out_shape=[hshape, hshape],
        scratch_shapes=[pltpu.VMEM((2 * H, ML_DQK, ML_DV + 128), F32), pltpu.VMEM((2 * H, 1, 1), F32)],
        compiler_params=_cparams(("parallel", "arbitrary")),
        name="mlstm",
    )(gate_b, *args, *args)


def _mlstm_post_kernel(hf_ref, hb_ref, o_ref, g_ref, y_ref):
    h = hf_ref[0] + hb_ref[0]
    h = h * lax.rsqrt(jnp.mean(h * h, axis=-1, keepdims=True) + EPS) * g_ref[...]
    y_ref[0] = (h * _sigmoid(o_ref[0])).astype(y_ref.dtype)


def _mlstm_post(hf, hb, p, norm_g):
    B, T, W = hf.shape
    tm = _tile(T, 1024, 16)
    hspec = pl.BlockSpec((1, tm, ML_DV), lambda b, i, h: (b, i, h))
    return pl.pallas_call(
        _mlstm_post_kernel,
        grid=(B, T // tm, ML_HEADS),
        in_specs=[hspec, hspec,
                  pl.BlockSpec((1, tm, ML_DV), lambda b, i, h: (b, i, ML_O0 // ML_DV + h)),
                  pl.BlockSpec((1, ML_DV), lambda b, i, h: (0, h))],
        out_specs=hspec,
        out_shape=jax.ShapeDtypeStruct((B, T, W), BF16),
        compiler_params=_cparams(("parallel", "parallel", "parallel")),
        name="mlstm_post",
    )(hf, hb, p, norm_g.reshape(1, W))


def _rwkv_prep_kernel(x_ref, xp_ref, xn_ref, mu_ref, w0_ref, a0_ref, wdh_ref, wdl_ref, wa_ref, wg_ref,
                      r_ref, k_ref, v_ref, ld_ref, ar_ref, g_ref, *, tt, ctx_len, nt):
    i = pl.program_id(1)
    x = x_ref[0][:, :RW_COLS]
    first = (i == 0) | (i * tt == ctx_len)
    last = ((i + 1) * tt == ctx_len) | (i == nt - 1)
    prev_row = jnp.where(first, 0.0, xp_ref[0][7:8, :RW_COLS])
    next_row = jnp.where(last, 0.0, xn_ref[0][0:1, :RW_COLS])
    ridx = lax.broadcasted_iota(jnp.int32, (tt, 1), 0)
    prev = jnp.where(ridx == 0, prev_row, pltpu.roll(x, 1, 0))
    nxt = jnp.where(ridx == tt - 1, next_row, pltpu.roll(x, tt - 1, 0))
    xs = x + mu_ref[...] * (0.5 * (prev + nxt) - x)

    W = BRANCH_W
    r_ref[0] = xs[:, 0:W]
    k_ref[0] = xs[:, W:2 * W]
    v_ref[0] = xs[:, 2 * W:3 * W]
    wlo = _mm3(_split2(jnp.tanh(xs[:, 3 * W:3 * W + 256])), (wdh_ref[...], wdl_ref[...]))
    alo = _dot(xs[:, 3 * W + 128:3 * W + 384].astype(BF16), wa_ref[...])
    for d in range(2):
        w = -_softplus(-(w0_ref[d:d + 1, :] + wlo[:, d * W:(d + 1) * W])) - 0.5
        ld_ref[d, 0] = -jnp.exp(w)
        ar_ref[d, 0] = _sigmoid(a0_ref[d:d + 1, :] + alo[:, d * W:(d + 1) * W])
    g_ref[0] = _dot(_sigmoid(xs[:, 3 * W + 384:3 * W + 640]).astype(BF16), wg_ref[...])


def _rwkv_prep(p, prm, ctx_len):
    B, T, _ = p.shape
    W = BRANCH_W
    tt = _tile(math.gcd(T, ctx_len), 256, 8)
    nt = T // tt
    h8 = tt // 8
    nb8 = T // 8
    z = lambda *s: jnp.zeros(s, F32)
    wd = jnp.concatenate([
        jnp.concatenate([prm['rw_w_up'][0], z(RW_LORA, W)], axis=1),
        jnp.concatenate([z(RW_LORA, W), prm['rw_w_up'][1]], axis=1),
        z(256 - 2 * RW_LORA, 2 * W)], axis=0)
    wa = jnp.concatenate([
        z(2 * RW_LORA - 128, 2 * W),
        jnp.concatenate([prm['rw_a_up'][0], z(RW_LORA, W)], axis=1),
        jnp.concatenate([z(RW_LORA, W), prm['rw_a_up'][1]], axis=1)], axis=0)
    wg = prm['rw_g_up'].astype(BF16)
    wdh, wdl = _split2(wd)
    wa = wa.astype(BF16)
    mu = prm['rw_mu'].reshape(1, RW_COLS)
    kern = functools.partial(_rwkv_prep_kernel, tt=tt, ctx_len=ctx_len, nt=nt)
    rwblk = RW_0 // RW_SPAN
    tok = jax.ShapeDtypeStruct((B, T, W), F32)
    tok2 = jax.ShapeDtypeStruct((2, B, T, W), F32)
    full = lambda shape: pl.BlockSpec(shape, lambda b, i: (0,) * len(shape))
    ospec = pl.BlockSpec((1, tt, W), lambda b, i: (b, i, 0))
    ospec2 = pl.BlockSpec((2, 1, tt, W), lambda b, i: (0, b, i, 0))
    return pl.pallas_call(
        kern,
        grid=(B, nt),
        in_specs=[pl.BlockSpec((1, tt, RW_SPAN), lambda b, i: (b, i, rwblk)),
                  pl.BlockSpec((1, 8, RW_SPAN), lambda b, i: (b, jnp.maximum(i * h8 - 1, 0), rwblk)),
                  pl.BlockSpec((1, 8, RW_SPAN), lambda b, i: (b, jnp.minimum((i + 1) * h8, nb8 - 1), rwblk)),
                  full((1, RW_COLS)), full((2, W)), full((2, W)),
                  full((256, 2 * W)), full((256, 2 * W)), full((256, 2 * W)), full((256, W))],
        out_specs=[ospec, ospec, ospec, ospec2, ospec2, ospec],
        out_shape=[tok, tok, tok, tok2, tok2, tok],
        compiler_params=_cparams(("parallel", "parallel")),
        name="rwkv_prep",
    )(p, p, p, mu, prm['rw_w0'], prm['rw_a0'], wdh, wdl, wa, wg)


def _split2(x):
    hi = x.astype(BF16)
    return hi, (x - hi.astype(F32)).astype(BF16)


def _split3(x):
    hi = x.astype(BF16)
    r1 = x - hi.astype(F32)
    mid = r1.astype(BF16)
    return hi, mid, (r1 - mid.astype(F32)).astype(BF16)


def _split1(x):
    return (x.astype(BF16),)


def _mm3(a, b, f=_dot):
    acc = f(a[0], b[0])
    if len(b) > 1:
        acc = acc + f(a[0], b[1])
    if len(a) > 1:
        acc = acc + f(a[1], b[0])
    return acc


def _mm_exact(e, parts, f=_dot):
    acc = f(e, parts[0])
    for x in parts[1:]:
        acc = acc + f(e, x)
    return acc


def _rwkv_scan_kernel(r_ref, k_ref, v_ref, ld_ref, ar_ref, kk_ref, ka_ref, y_ref, st_ref, *, n_pairs):
    L = CHUNK
    P = 2 * RW_DH
    d = pl.program_id(0)

    @pl.when(pl.program_id(2) == 0)
    def _():
        st_ref[...] = jnp.zeros_like(st_ref)

    row = lax.broadcasted_iota(jnp.int32, (L, P), 0)
    col = lax.broadcasted_iota(jnp.int32, (L, P), 1) % RW_DH
    diff = (row - col) * jnp.where(d == 0, 1, -1)
    before = diff > 0
    incl = diff >= 0
    tri = incl[:, :L].astype(BF16)
    eye = (row == col).astype(F32)
    ones = jnp.ones((L, P), BF16)
    lane = lax.broadcasted_iota(jnp.int32, (1, P), 1)
    m0 = lane < RW_DH
    r2 = lax.broadcasted_iota(jnp.int32, (P, P), 0) // RW_DH
    c2 = lax.broadcasted_iota(jnp.int32, (P, P), 1) // RW_DH
    same_head = r2 == c2
    ones_bd = same_head.astype(BF16)

    def bd(x):
        z = jnp.zeros_like(x)
        return jnp.concatenate([jnp.where(m0, x, z), jnp.where(m0, z, x)], axis=0)

    def bd2(x):
        return tuple(bd(t) for t in x)

    sp = spi = _split1

    def cat(xs, axis):
        return tuple(jnp.concatenate(list(t), axis=axis) for t in zip(*xs))

    def st_load(s):
        sl = s['sl']
        s['R'], s['K'], s['V'] = r_ref[0, :, sl], k_ref[0, :, sl], v_ref[0, :, sl]
        s['LD'], s['AR'] = ld_ref[0, 0, :, sl], ar_ref[0, 0, :, sl]
        s['S0'] = st_ref[s['j']]
        s['kk'] = s['K'] * kk_ref[:, sl]
        s['nrm2'] = _mm_exact(ones_bd, _split3(s['kk'] * s['kk']), lambda e, x: _dot(x, e))

    def st_cum(s):
        s['kk'] = s['kk'] / jnp.maximum(jnp.sqrt(s.pop('nrm2')), 1e-12)
        s['KD'] = s.pop('K') * (1.0 + (s['AR'] - 1.0) * ka_ref[:, s['sl']])
        s['BV'] = s['kk'] * s.pop('AR')
        s['LD3'] = _split3(s['LD'])
        s['c'] = _mm_exact(tri, s['LD3'])

    def st_x(s):
        c, LD = s.pop('c'), s.pop('LD')
        c_end = jnp.sum(LD, axis=0, keepdims=True)
        At = -s.pop('kk') * jnp.exp(c - LD)
        e_neg = jnp.exp(-c)
        BV, KD = s.pop('BV'), s.pop('KD')
        s['Rt'] = s.pop('R') * jnp.exp(c)
        e_end = jnp.exp(c_end - c)
        s['Bh_s'], s['Kh_s'] = sp(BV * e_end), sp(KD * e_end)
        s['At_s'], s['V_s'] = sp(At), sp(s.pop('V'))
        s['X'] = _mm3(cat([s['At_s'], sp(s['Rt'])], 0),
                      cat([bd2(sp(BV * e_neg)), bd2(sp(KD * e_neg))], 0), _dot_nt)

    def st_inv0(s):
        X = s.pop('X')
        Nab = jnp.where(before, X[:L, :P], 0.0)
        s['Nak_s'] = sp(jnp.where(before, X[:L, P:], 0.0))
        s['Mrb_s'] = sp(jnp.where(incl, X[L:, :P], 0.0))
        s['Mrk_s'] = sp(jnp.where(incl, X[L:, P:], 0.0))
        s['T'] = eye + Nab
        Np_s = spi(Nab)
        s['Z'] = _mm3(Np_s, bd2(Np_s))
        s['NV'] = _mm3(s.pop('Nak_s'), bd2(s['V_s']))

    def st_inv1(s):
        Z = s.pop('Z')
        if Z.shape[0] == 2 * L:
            s['T'] = s['T'] + Z[L:]
        Np_s = spi(Z[:L])
        s['Z'] = _mm3(cat([Np_s, spi(s['T'])], 0), bd2(Np_s))

    def st_inv2(s):
        Z = s.pop('Z')
        s['T'] = s['T'] + Z[L:]
        s['Z'] = _mm3(spi(s['T']), bd2(spi(Z[:L])))

    def st_wu(s):
        T_s = sp(s.pop('T') + s.pop('Z'))
        s['WU'] = _mm3(T_s, cat([bd2(s.pop('At_s')), bd2(sp(s.pop('NV')))], 1))

    def st_loc(s):
        WU = s.pop('WU')
        W_s, Ul_s = sp(WU[:, :P]), sp(WU[:, P:])
        MM = _mm3(s.pop('Mrb_s'), cat([bd2(W_s), bd2(Ul_s)], 1))
        s['Q_s'] = sp(s.pop('Rt') + MM[:, :P])
        s['Yl'] = MM[:, P:] + _mm3(s.pop('Mrk_s'), bd2(s['V_s']))
        PS = _mm3(s.pop('Bh_s'), cat([W_s, Ul_s], 1), _dot_tn)
        s['Pm_s'] = sp(jnp.where(same_head, PS[:, :P], 0.0))
        s['Sl'] = jnp.where(same_head, PS[:, P:] + _mm3(s.pop('Kh_s'), s.pop('V_s'), _dot_tn), 0.0)
        s['dcol'] = jnp.exp(_mm_exact(ones, s.pop('LD3'), lambda e, x: _dot_tn(x, e)))

    def st_out(s):
        S0 = s.pop('S0')
        S0_s = sp(S0)
        y_ref[0, 0, :, s['sl']] = _mm3(s.pop('Q_s'), S0_s) + s.pop('Yl')
        st_ref[s['j']] = s.pop('dcol') * S0 + _mm3(s.pop('Pm_s'), S0_s) + s.pop('Sl')

    n_dbl = int(math.log2(L)) - 2
    stages = [st_load, st_cum, st_x, st_inv0] + [st_inv1] * n_dbl + [st_inv2, st_wu, st_loc, st_out]
    states = [{'j': j, 'sl': slice(P * j, P * (j + 1))} for j in range(n_pairs)]
    for stage in stages:
        for s in states:
            stage(s)


def _rwkv_scan(r, k, v, ld, ar, k_k, k_a, ctx_len):
    B, T, W = r.shape
    L = CHUNK
    nch = T // L
    nctx = ctx_len // L

    def smap(d, b, s):
        return (b, _chunk_index(d, s, nctx, nch), 0)

    def dmap(d, b, s):
        return (d, b, _chunk_index(d, s, nctx, nch), 0)

    pspec = pl.BlockSpec((1, W), lambda d, b, s: (0, 0))
    kern = functools.partial(_rwkv_scan_kernel, n_pairs=W // (2 * RW_DH))
    return pl.pallas_call(
        kern,
        grid=(2, B, nch),
        in_specs=[pl.BlockSpec((1, L, W), smap), pl.BlockSpec((1, L, W), smap), pl.BlockSpec((1, L, W), smap),
                  pl.BlockSpec((1, 1, L, W), dmap), pl.BlockSpec((1, 1, L, W), dmap),
                  pspec, pspec],
        out_specs=pl.BlockSpec((1, 1, L, W), dmap),
        out_shape=jax.ShapeDtypeStruct((2, B, T, W), F32),
        scratch_shapes=[pltpu.VMEM((W // (2 * RW_DH), 2 * RW_DH, 2 * RW_DH), F32)],
        compiler_params=_cparams(("parallel", "parallel", "arbitrary")),
        name="rwkv_scan",
    )(r, k, v, ld, ar, k_k.reshape(1, W), k_a.reshape(1, W))


def _rwkv_post_kernel(yf_ref, yb_ref, r_ref, k_ref, v_ref, g_ref, rk_ref, lw_ref, lb_ref, o_ref):
    P = 2 * RW_DH
    r2 = lax.broadcasted_iota(jnp.int32, (P, P), 0) // RW_DH
    c2 = lax.broadcasted_iota(jnp.int32, (P, P), 1) // RW_DH
    ones_bd = (r2 == c2).astype(BF16)

    def head_sum(x):
        return _mm_exact(ones_bd, _split2(x), lambda e, t: _dot(t, e))

    for j in range(o_ref.shape[2] // P):
        sl = slice(P * j, P * (j + 1))
        y = yf_ref[0, 0, :, sl] + yb_ref[0, 0, :, sl]
        yc = y - head_sum(y) * (1.0 / RW_DH)
        var = head_sum(yc * yc) * (1.0 / RW_DH)
        yn = yc * lax.rsqrt(var + RW_GN_EPS)
        bonus = head_sum(r_ref[0, :, sl] * k_ref[0, :, sl] * rk_ref[:, sl]) * v_ref[0, :, sl]
        o_ref[0, :, sl] = ((yn * lw_ref[:, sl] + lb_ref[:, sl] + bonus) * g_ref[0, :, sl]).astype(o_ref.dtype)


def _rwkv_post(y2, r, k, v, g, r_k, ln_w, ln_b):
    _, B, T, W = y2.shape
    tt = _tile(T, 512, 16)
    yspec = lambda d: pl.BlockSpec((1, 1, tt, W), lambda b, i: (d, b, i, 0))
    tspec = pl.BlockSpec((1, tt, W), lambda b, i: (b, i, 0))
    pspec = pl.BlockSpec((1, W), lambda b, i: (0, 0))
    return pl.pallas_call(
        _rwkv_post_kernel,
        grid=(B, T // tt),
        in_specs=[yspec(0), yspec(1), tspec, tspec, tspec, tspec, pspec, pspec, pspec],
        out_specs=tspec,
        out_shape=jax.ShapeDtypeStruct((B, T, W), BF16),
        compiler_params=_cparams(("parallel", "parallel")),
        name="rwkv_post",
    )(y2, y2, r, k, v, g, r_k.reshape(1, W), ln_w.reshape(1, W), ln_b.reshape(1, W))


def _rwkv_branch(p, prm, ctx_len):
    r, k, v, ld, ar, g = _rwkv_prep(p, prm, ctx_len)
    y2 = _rwkv_scan(r, k, v, ld, ar, prm['rw_k_k'], prm['rw_k_a'], ctx_len)
    return _rwkv_post(y2, r, k, v, g, prm['rw_r_k'], prm['rw_ln_w'], prm['rw_ln_b'])


NA_RB = 4
NA_SLAB = NA_RB + NA_WIN_R - 1
NA_HB = 2


def _na_kernel(q_ref, k_ref, v_ref, bias_ref, o_ref, *, ctx_len, rows):
    rb = pl.program_id(2)
    scale = NA_DH ** -0.5
    heads = [slice(i * NA_DH, (i + 1) * NA_DH) for i in range(NA_HB)]

    @pl.when(rb == 0)
    def _():
        for hs in heads:
            s_ctx = _dot_nt(q_ref[0, :, hs].astype(BF16), k_ref[0, 0:ctx_len, hs].astype(BF16)) * scale
            p_ctx = jnp.exp(s_ctx - jnp.max(s_ctx, axis=-1, keepdims=True))
            y = _dot(p_ctx.astype(BF16), v_ref[0, 0:ctx_len, hs].astype(BF16))
            o_ref[0, :, hs] = (y / jnp.sum(p_ctx, axis=-1, keepdims=True)).astype(o_ref.dtype)

    @pl.when(rb > 0)
    def _():
        nslab = NA_SLAB * GRID_W
        start = jnp.clip((rb - 1) * NA_RB - NA_WIN_R // 2, 0, rows - NA_SLAB)
        koff = pl.multiple_of(ctx_len + start * GRID_W, GRID_W)

        def st_scores(s):
            hs = s['hs']
            q = q_ref[0, :, hs].astype(BF16)
            s['s_win'] = _dot_nt(q, k_ref[0, pl.ds(koff, nslab), hs].astype(BF16)) * scale + bias_ref[0, s['i'], 0]
            s['s_ctx'] = _dot_nt(q, k_ref[0, 0:ctx_len, hs].astype(BF16)) * scale

        def st_softmax(s):
            s_win, s_ctx = s.pop('s_win'), s.pop('s_ctx')
            m = jnp.maximum(jnp.max(s_win, axis=-1, keepdims=True), jnp.max(s_ctx, axis=-1, keepdims=True))
            p_win = jnp.exp(s_win - m)
            p_ctx = jnp.exp(s_ctx - m)
            s['den'] = jnp.sum(p_win, axis=-1, keepdims=True) + jnp.sum(p_ctx, axis=-1, keepdims=True)
            hs = s['hs']
            s['y'] = (_dot(p_win.astype(BF16), v_ref[0, pl.ds(koff, nslab), hs].astype(BF16))
                      + _dot(p_ctx.astype(BF16), v_ref[0, 0:ctx_len, hs].astype(BF16)))

        def st_out(s):
            o_ref[0, :, s['hs']] = (s.pop('y') / s.pop('den')).astype(o_ref.dtype)

        states = [{'i': i, 'hs': hs} for i, hs in enumerate(heads)]
        for stage in (st_scores, st_softmax, st_out):
            for s in states:
                stage(s)


def _na_block_type(lb, nblk):
    return jnp.where(lb <= 0, 0, jnp.where(lb == nblk - 1, 2, 1))


def _na_bias_table(rpb, rows):
    c = np.arange(GRID_W)
    c_start = np.clip(c - NA_WIN_C // 2, 0, GRID_W - NA_WIN_C)
    col_ok = (c[None, :] >= c_start[:, None]) & (c[None, :] < c_start[:, None] + NA_WIN_C)
    col_off = np.clip(c[None, :] - c[:, None], 1 - NA_WIN_C, NA_WIN_C - 1) + (NA_WIN_C - 1)
    nblk = rows // NA_RB
    reps = np.array([0, min(1, nblk - 1), nblk - 1])
    r = reps[:, None] * NA_RB + np.arange(NA_RB)[None, :]
    slab0 = np.clip(reps * NA_RB - NA_WIN_R // 2, 0, rows - NA_SLAB)
    kr = slab0[:, None] + np.arange(NA_SLAB)[None, :]
    win0 = np.clip(r - NA_WIN_R // 2, 0, rows - NA_WIN_R)
    ro = kr[:, None, :] - r[:, :, None] + (NA_WIN_R - 1)
    row_ok = (kr[:, None, :] >= win0[:, :, None]) & (kr[:, None, :] < win0[:, :, None] + NA_WIN_R)
    sel_r = (ro[..., None] == np.arange(2 * NA_WIN_R - 1)) & row_ok[..., None]
    sel_c = col_off[..., None] == np.arange(2 * NA_WIN_C - 1)
    ok = row_ok[:, :, None, :, None] & col_ok[None, None, :, None, :]
    tab = jnp.einsum('trjo,qkc,lhoc->lhtrqjk', jnp.asarray(sel_r, F32), jnp.asarray(sel_c, F32), rpb,
                     precision=HI)
    tab = tab + jnp.asarray(np.where(ok, 0.0, NEG), F32)
    return tab.reshape(rpb.shape[0], rpb.shape[1], 3, NA_RB * GRID_W, NA_SLAB * GRID_W)


def _na_branch(p, table, l, ctx_len):
    B, T, _ = p.shape
    seq = T - ctx_len
    rows = seq // GRID_W
    H = NA_HEADS
    nq = NA_RB * GRID_W
    assert ctx_len == nq and rows % NA_RB == 0 and rows >= NA_SLAB, (ctx_len, rows)
    nblk = rows // NA_RB
    wb = NA_HB * NA_DH
    qb, kb, vb = NA_0 // wb, (NA_0 + BRANCH_W) // wb, (NA_0 + 2 * BRANCH_W) // wb
    kern = functools.partial(_na_kernel, ctx_len=ctx_len, rows=rows)
    return pl.pallas_call(
        kern,
        grid=(B, H // NA_HB, 1 + nblk),
        in_specs=[pl.BlockSpec((1, nq, wb), lambda b, h, r: (b, r, qb + h)),
                  pl.BlockSpec((1, T, wb), lambda b, h, r: (b, 0, kb + h)),
                  pl.BlockSpec((1, T, wb), lambda b, h, r: (b, 0, vb + h)),
                  pl.BlockSpec((1, NA_HB, 1, nq, NA_SLAB * GRID_W),
                               lambda b, h, r: (l, h, _na_block_type(r - 1, nblk), 0, 0))],
        out_specs=pl.BlockSpec((1, nq, wb), lambda b, h, r: (b, r, h)),
        out_shape=jax.ShapeDtypeStruct((B, T, BRANCH_W), BF16),
        compiler_params=_cparams(("parallel", "parallel", "arbitrary")),
        name="na",
    )(p, p, p, table)


def _pack_kernel(w_ref, o_ref):
    ml, rw, na, gt = 3088, 3712, 3072, 6144
    o_rw, o_na, o_gt = ml, ml + rw, ml + rw + na
    w = w_ref[0]
    n = w.shape[0]
    o_ref[0, :, 0:NA_0] = w[:, 0:NA_0].astype(BF16)
    o_ref[0, :, NA_0:GATE_0] = w[:, o_na:o_na + na].astype(BF16)
    o_ref[0, :, GATE_0:RW_0] = w[:, o_gt:o_gt + gt].astype(BF16)
    o_ref[0, :, RW_0:MLG_0] = w[:, o_rw:o_rw + rw].astype(BF16)
    tail = jnp.concatenate([w[:, NA_0:ml], jnp.zeros((n, 128 - (ml - NA_0)), F32)], axis=1)
    o_ref[0, :, MLG_0:MLG_0 + 128] = tail.astype(BF16)
    o_ref[0, :, MLG_0 + 128:] = jnp.zeros((n, P_COLS - MLG_0 - 128), BF16)


def _pack_w_in(w_in, l):
    _, D, N = w_in.shape
    tr = _tile(D, 64, 16)
    return pl.pallas_call(
        _pack_kernel,
        grid=(D // tr,),
        in_specs=[pl.BlockSpec((1, tr, N), lambda i: (l, i, 0))],
        out_specs=pl.BlockSpec((1, tr, P_COLS), lambda i: (0, i, 0)),
        out_shape=jax.ShapeDtypeStruct((1, D, P_COLS), BF16),
        compiler_params=_cparams(("parallel",)),
        name="pack_w_in",
    )(w_in)


def _rope_tables(seq, ctx_len):
    t = jnp.arange(seq)
    nf = ML_DQK // 4
    freqs = ROPE_BASE ** (-jnp.arange(nf, dtype=F32) / nf)
    ang_r = (t // GRID_W).astype(F32)[:, None] * freqs[None, :]
    ang_c = (t % GRID_W).astype(F32)[:, None] * freqs[None, :]
    cos = jnp.concatenate([jnp.cos(ang_r)] * 2 + [jnp.cos(ang_c)] * 2, axis=1)
    sin = jnp.concatenate([-jnp.sin(ang_r), jnp.sin(ang_r), -jnp.sin(ang_c), jnp.sin(ang_c)], axis=1)
    cos = jnp.concatenate([jnp.ones((ctx_len, ML_DQK), F32), cos], axis=0)
    sin = jnp.concatenate([jnp.zeros((ctx_len, ML_DQK), F32), sin], axis=0)
    return cos, sin


def kernel(x, c, ctx, c_ctx, w_mod, b_mod, norm1_g, w_in, ml_gate_b, ml_norm_g, rw_mu, rw_w0, rw_w_up, rw_a0, rw_a_up, rw_g_up, rw_k_k, rw_k_a, rw_r_k, rw_ln_w, rw_ln_b, na_rpb, w_branch, w_out, norm2_g, w_ff1, w_ff2, final_g):
    B, seq, D = x.shape
    ctx_len = ctx.shape[1]
    depth = w_mod.shape[0]
    xs = jnp.concatenate([ctx, x], axis=1)
    s_all = jax.nn.silu(jnp.concatenate([c_ctx[None, :], c], axis=0))
    s_all = jnp.concatenate([s_all, jnp.zeros((8 - (B + 1) % 8, D), F32)], axis=0) if (B + 1) % 8 else s_all
    cos, sin = _rope_tables(seq, ctx_len)
    na_table = _na_bias_table(na_rpb, seq // GRID_W)
    wb_bf, wo_bf, w1_bf, w2_bf = (w.astype(BF16) for w in (w_branch, w_out, w_ff1, w_ff2))
    for l in range(depth):
        mod = _modulation(s_all, w_mod, b_mod, l).reshape(-1, 6, D)
        mc = jnp.broadcast_to(mod[0:1], (B, 6, D))
        ml_ = mod[1:B + 1]
        mod1 = jnp.stack([mc[:, 0], mc[:, 1], ml_[:, 0], ml_[:, 1]], axis=1)
        gate1 = jnp.stack([mc[:, 2], ml_[:, 2]], axis=1)
        mod2 = jnp.stack([mc[:, 3], mc[:, 4], ml_[:, 3], ml_[:, 4]], axis=1)
        gate2 = jnp.stack([mc[:, 5], ml_[:, 5]], axis=1)
        prm = {'rw_mu': rw_mu[l], 'rw_w0': rw_w0[l], 'rw_w_up': rw_w_up[l], 'rw_a0': rw_a0[l],
               'rw_a_up': rw_a_up[l], 'rw_g_up': rw_g_up[l], 'rw_k_k': rw_k_k[l], 'rw_k_a': rw_k_a[l],
               'rw_r_k': rw_r_k[l], 'rw_ln_w': rw_ln_w[l], 'rw_ln_b': rw_ln_b[l]}

        p = _mm(_norm_mod(xs, norm1_g[l], mod1, ctx_len), _pack_w_in(w_in, l), 0, F32, False)
        h_f, h_b = _mlstm(p, ml_gate_b[l], cos, sin, ctx_len)
        y_m = _mlstm_post(h_f, h_b, p, ml_norm_g[l])
        y_r = _rwkv_branch(p, prm, ctx_len)
        y_n = _na_branch(p, na_table, l, ctx_len)
        merged = _merge(y_m, y_r, y_n, p, wb_bf, l)
        xs = _mm_res(merged, wo_bf, l, xs, gate1, ctx_len)
        hff = _mm(_norm_mod(xs, norm2_g[l], mod2, ctx_len), w1_bf, l, BF16, True)
        xs = _mm_res(hff, w2_bf, l, xs, gate2, ctx_len)
    return _final_norm(xs, final_g, ctx_len)
```

```python
import functools
import math

import jax
import jax.numpy as jnp
import numpy as np
from jax import lax
from jax.experimental import pallas as pl
from jax.experimental.pallas import tpu as pltpu

F32 = jnp.float32
BF16 = jnp.bfloat16
HI = lax.Precision.HIGHEST

D_MODEL = 2048
GRID_W = 64
BRANCH_W = D_MODEL // 2
N_BRANCH = 3
ML_HEADS = 4
ML_DV = BRANCH_W // ML_HEADS
ML_DQK = ML_DV // 2
RW_DH = 64
RW_HEADS = BRANCH_W // RW_DH
RW_LORA = 96
RW_GATE_LORA = 256
RW_GN_EPS = 64e-5
NA_DH = 128
NA_HEADS = BRANCH_W // NA_DH
NA_WIN_R = 8
NA_WIN_C = 16
D_FF = 4 * D_MODEL
ROPE_BASE = 10000.0
EPS = 1e-6
CHUNK = 64
NEG = -1e30

ML_Q0, ML_K0, ML_V0, ML_O0 = 0, 512, 1024, 2048
NA_0 = 3072
GATE_0 = 6144
RW_0 = 12288
RW_SPAN = 4096
RW_COLS = 3712
MLG_0 = RW_0 + RW_COLS
P_COLS = RW_0 + RW_SPAN

VMEM_LIMIT = 56 * 1024 * 1024
MM_ROWS = 1100


def _cparams(sem):
    return pltpu.CompilerParams(dimension_semantics=sem, vmem_limit_bytes=VMEM_LIMIT)


def _tile(n, cap, mult):
    best = None
    for t in range(mult, min(n, cap) + 1, mult):
        if n % t == 0:
            best = t
    assert best is not None, (n, cap, mult)
    return best


def _dot(a, b, prec=None):
    return jnp.dot(a, b, preferred_element_type=F32, precision=prec)


def _dot_nt(a, b, prec=None):
    return lax.dot_general(a, b, (((1,), (1,)), ((), ())), preferred_element_type=F32, precision=prec)


def _dot_tn(a, b, prec=None):
    return lax.dot_general(a, b, (((0,), (0,)), ((), ())), preferred_element_type=F32, precision=prec)


def _sigmoid(x):
    return 1.0 / (1.0 + jnp.exp(-x))


def _softplus(x):
    return jnp.maximum(x, 0.0) + jnp.log(1.0 + jnp.exp(-jnp.abs(x)))


def _mod_kernel(s_ref, w_ref, b_ref, o_ref):
    o_ref[...] = _dot(s_ref[...], w_ref[0], HI) + b_ref[0]


def _modulation(s, w, b, l):
    R, D = s.shape
    N = w.shape[2]
    tn = _tile(N, 1024, 128)
    return pl.pallas_call(
        _mod_kernel,
        grid=(N // tn,),
        in_specs=[pl.BlockSpec((R, D), lambda j: (0, 0)),
                  pl.BlockSpec((1, D, tn), lambda j: (l, 0, j)),
                  pl.BlockSpec((1, 1, tn), lambda j: (l, 0, j))],
        out_specs=pl.BlockSpec((R, tn), lambda j: (0, j)),
        out_shape=jax.ShapeDtypeStruct((R, N), F32),
        compiler_params=_cparams(("arbitrary",)),
        name="modulation",
    )(s, w, b.reshape(b.shape[0], 1, N))


def _norm_mod_kernel(x_ref, g_ref, mod_ref, o_ref, *, tm, ctx_len):
    x = x_ref[0]
    y = x * lax.rsqrt(jnp.mean(x * x, axis=-1, keepdims=True) + EPS) * g_ref[...]
    row = pl.program_id(1) * tm + lax.broadcasted_iota(jnp.int32, (tm, 1), 0)
    is_ctx = row < ctx_len
    m = mod_ref[0]
    shift = jnp.where(is_ctx, m[0:1], m[2:3])
    scale = jnp.where(is_ctx, m[1:2], m[3:4])
    o_ref[0] = (y * (1.0 + scale) + shift).astype(o_ref.dtype)


def _norm_mod(x, g, mod4, ctx_len):
    B, T, D = x.shape
    tm = _tile(T, 320, 16)
    kern = functools.partial(_norm_mod_kernel, tm=tm, ctx_len=ctx_len)
    return pl.pallas_call(
        kern,
        grid=(B, T // tm),
        in_specs=[pl.BlockSpec((1, tm, D), lambda b, i: (b, i, 0)),
                  pl.BlockSpec((1, D), lambda b, i: (0, 0)),
                  pl.BlockSpec((1, 4, D), lambda b, i: (b, 0, 0))],
        out_specs=pl.BlockSpec((1, tm, D), lambda b, i: (b, i, 0)),
        out_shape=jax.ShapeDtypeStruct((B, T, D), BF16),
        compiler_params=_cparams(("parallel", "parallel")),
        name="norm_mod",
    )(x, g.reshape(1, D), mod4)


def _mm_kernel(h_ref, w_ref, o_ref, *, relu2):
    acc = _dot(h_ref[0], w_ref[0].astype(BF16))
    if relu2:
        acc = jnp.square(jnp.maximum(acc, 0.0))
    o_ref[0] = acc.astype(o_ref.dtype)


def _mm(h, w, l, out_dtype, relu2):
    B, T, D = h.shape
    N = w.shape[2]
    tm = _tile(T, 2 * MM_ROWS, 16)
    tn = _tile(N, 512, 128)
    kern = functools.partial(_mm_kernel, relu2=relu2)
    return pl.pallas_call(
        kern,
        grid=(B, T // tm, N // tn),
        in_specs=[pl.BlockSpec((1, tm, D), lambda b, i, j: (b, i, 0)),
                  pl.BlockSpec((1, D, tn), lambda b, i, j: (l, 0, j))],
        out_specs=pl.BlockSpec((1, tm, tn), lambda b, i, j: (b, i, j)),
        out_shape=jax.ShapeDtypeStruct((B, T, N), out_dtype),
        compiler_params=_cparams(("parallel", "parallel", "arbitrary")),
        name="mm",
    )(h, w)


def _mm_res_kernel(a_ref, w_ref, x_ref, gate_ref, o_ref, acc_ref, *, tm, ctx_len, nk):
    i = pl.program_id(1)
    k = pl.program_id(3)

    @pl.when(k == 0)
    def _():
        acc_ref[...] = jnp.zeros_like(acc_ref)

    acc_ref[...] += _dot(a_ref[0], w_ref[0])

    @pl.when(k == nk - 1)
    def _():
        row = i * tm + lax.broadcasted_iota(jnp.int32, (tm, 1), 0)
        gt = gate_ref[0]
        gate = jnp.where(row < ctx_len, gt[0:1], gt[1:2])
        o_ref[0] = x_ref[0] + gate * acc_ref[...]


def _mm_res(a, w, l, x, gate2, ctx_len):
    B, T, K = a.shape
    D = w.shape[2]
    tm = _tile(T, MM_ROWS, 16)
    tn = _tile(D, 1024, 128)
    tk = _tile(K, 2048, 128)
    nk = K // tk
    kern = functools.partial(_mm_res_kernel, tm=tm, ctx_len=ctx_len, nk=nk)
    return pl.pallas_call(
        kern,
        grid=(B, T // tm, D // tn, nk),
        in_specs=[pl.BlockSpec((1, tm, tk), lambda b, i, j, k: (b, i, k)),
                  pl.BlockSpec((1, tk, tn), lambda b, i, j, k: (l, k, j)),
                  pl.BlockSpec((1, tm, tn), lambda b, i, j, k: (b, i, j)),
                  pl.BlockSpec((1, 2, tn), lambda b, i, j, k: (b, 0, j))],
        out_specs=pl.BlockSpec((1, tm, tn), lambda b, i, j, k: (b, i, j)),
        out_shape=jax.ShapeDtypeStruct((B, T, D), F32),
        scratch_shapes=[pltpu.VMEM((tm, tn), F32)],
        compiler_params=_cparams(("parallel", "parallel", "parallel", "arbitrary")),
        name="mm_res",
    )(a, w, x, gate2)


def _merge_kernel(ym_ref, yr_ref, yn_ref, g0_ref, g1_ref, g2_ref, wb_ref, o_ref):
    acc = _sigmoid(g0_ref[0]) * _dot(ym_ref[0], wb_ref[0, 0])
    acc += _sigmoid(g1_ref[0]) * _dot(yr_ref[0], wb_ref[0, 1])
    acc += _sigmoid(g2_ref[0]) * _dot(yn_ref[0], wb_ref[0, 2])
    o_ref[0] = acc.astype(o_ref.dtype)


def _merge(ym, yr, yn, p, wb, l):
    B, T, W = ym.shape
    D = wb.shape[3]
    tm = _tile(T, MM_ROWS, 16)
    tn = _tile(D, 512, 128)
    yspec = pl.BlockSpec((1, tm, W), lambda b, i, j: (b, i, 0))

    def gspec(n):
        base = (GATE_0 + n * D) // tn
        return pl.BlockSpec((1, tm, tn), lambda b, i, j: (b, i, base + j))

    return pl.pallas_call(
        _merge_kernel,
        grid=(B, T // tm, D // tn),
        in_specs=[yspec, yspec, yspec, gspec(0), gspec(1), gspec(2),
                  pl.BlockSpec((1, N_BRANCH, W, tn), lambda b, i, j: (l, 0, 0, j))],
        out_specs=pl.BlockSpec((1, tm, tn), lambda b, i, j: (b, i, j)),
        out_shape=jax.ShapeDtypeStruct((B, T, D), BF16),
        compiler_params=_cparams(("parallel", "parallel", "arbitrary")),
        name="merge",
    )(ym, yr, yn, p, p, p, wb)


def _final_norm_kernel(x_ref, g_ref, o_ref):
    x = x_ref[0]
    o_ref[0] = x * lax.rsqrt(jnp.mean(x * x, axis=-1, keepdims=True) + EPS) * g_ref[...]


def _final_norm(xs, g, ctx_len):
    B, T, D = xs.shape
    seq = T - ctx_len
    tm = _tile(math.gcd(seq, ctx_len), 256, 8)
    off = ctx_len // tm
    return pl.pallas_call(
        _final_norm_kernel,
        grid=(B, seq // tm),
        in_specs=[pl.BlockSpec((1, tm, D), lambda b, i: (b, off + i, 0)),
                  pl.BlockSpec((1, D), lambda b, i: (0, 0))],
        out_specs=pl.BlockSpec((1, tm, D), lambda b, i: (b, i, 0)),
        out_shape=jax.ShapeDtypeStruct((B, seq, D), F32),
        compiler_params=_cparams(("parallel", "parallel")),
        name="final_norm",
    )(xs, g.reshape(1, D))


def _chunk_index(d, s, n_ctx, n_all):
    back = jnp.where(s < n_ctx, n_ctx - 1 - s, n_all + n_ctx - 1 - s)
    return jnp.where(d == 0, s, back)


def _rope(x, cos, sin_signed, first_half):
    swapped = jnp.where(first_half, pltpu.roll(x, 96, 1), pltpu.roll(x, 32, 1))
    return x * cos + swapped * sin_signed


def _log_sigmoid(x):
    return jnp.minimum(x, 0.0) - jnp.log(1.0 + jnp.exp(-jnp.abs(x)))


def _mlstm_kernel(bias_ref, *refs):
    L = CHUNK
    n_in = 7
    dir_refs = (refs[:n_in], refs[n_in:2 * n_in])
    out_refs = refs[2 * n_in:2 * n_in + 2]
    ct_ref, m_ref = refs[2 * n_in + 2:]

    @pl.when(pl.program_id(1) == 0)
    def _():
        ct_ref[...] = jnp.zeros_like(ct_ref)
        m_ref[...] = jnp.zeros_like(m_ref)

    row = lax.broadcasted_iota(jnp.int32, (L, L), 0)
    col = lax.broadcasted_iota(jnp.int32, (L, L), 1)
    lane = lax.broadcasted_iota(jnp.int32, (L, ML_DQK), 1)
    first_half = (lane % 64) < 32
    ones = jnp.ones((L, 128), BF16)

    def st_qk(s):
        d, h = s['d'], s['h']
        q_ref, k_ref, v_ref, _, _, cos_ref, sin_ref = dir_refs[d]
        cos, sin = cos_ref[...], sin_ref[...]
        hq = slice(h * ML_DQK, (h + 1) * ML_DQK)
        s['q'] = _rope(q_ref[0, :, hq] * (ML_DQK ** -0.5), cos, sin, first_half).astype(BF16)
        s['k'] = _rope(k_ref[0, :, hq], cos, sin, first_half).astype(BF16)
        s['v'] = v_ref[0, :, h * ML_DV:(h + 1) * ML_DV]
        s['qk'] = _dot_nt(s['q'], s['k'])

    def st_gate(s):
        d, h = s['d'], s['h']
        gc_ref, gr_ref = dir_refs[d][3:5]
        b_i = bias_ref[8 * d + h]
        b_f = bias_ref[8 * d + 4 + h]
        ji, jf = 8 * d + h, 8 * d + 4 + h
        gc = gc_ref[0]
        gr = gr_ref[0, 0]
        s['ig_c'] = gc[:, ji:ji + 1] + b_i
        lf_c = _log_sigmoid(gc[:, jf:jf + 1] + b_f)
        s['ig_r'] = gr[ji:ji + 1, :] + b_i
        lf_r = _log_sigmoid(gr[jf:jf + 1, :] + b_f)
        s['incl'] = (col <= row) if d == 0 else (col >= row)
        incl_t = (row <= col) if d == 0 else (row >= col)
        s['b_c'] = jnp.sum(jnp.where(s['incl'], lf_r, 0.0), axis=1, keepdims=True)
        s['b_r'] = jnp.sum(jnp.where(incl_t, lf_c, 0.0), axis=0, keepdims=True)
        s['b_end'] = jnp.sum(lf_r, axis=1, keepdims=True)

    def st_max(s):
        b_c, b_r, b_end, ig_r = s['b_c'], s.pop('b_r'), s['b_end'], s.pop('ig_r')
        s['a_c'] = b_end - b_c + s.pop('ig_c')
        s['a_max'] = jnp.max(b_end - b_r + ig_r, axis=1, keepdims=True)
        s['log_intra'] = jnp.where(s.pop('incl'), b_c - b_r + ig_r, NEG)
        s['mx'] = jnp.max(s['log_intra'], axis=1, keepdims=True)

    def st_exp(s):
        m0 = m_ref[s['c']]
        b_end, a_max = s.pop('b_end'), s.pop('a_max')
        log_inter = s.pop('b_c') + m0
        m_j = jnp.maximum(log_inter, s.pop('mx'))
        s['decay'] = jnp.exp(s.pop('log_intra') - m_j)
        s['w_inter'] = jnp.exp(log_inter - m_j)
        s['floor'] = jnp.exp(-m_j)
        s['m_new'] = jnp.maximum(b_end + m0, a_max)
        s['s_old'] = jnp.exp(b_end + m0 - s['m_new'])
        s['s_loc'] = jnp.exp(a_max - s['m_new'])
        v = s.pop('v')
        wgt = jnp.exp(s.pop('a_c') - a_max)
        s['vext'] = jnp.concatenate([v.astype(BF16), ones], axis=1)
        s['vw'] = jnp.concatenate([v * wgt, jnp.broadcast_to(wgt, (L, 128))], axis=1).astype(BF16)

    def st_pv(s):
        smat = (s.pop('qk') * s.pop('decay')).astype(BF16)
        ct0 = ct_ref[s['c']]
        s['num'] = _dot(smat, s.pop('vext')) + s.pop('w_inter') * _dot(s.pop('q'), ct0.astype(BF16))
        s['ct'] = s.pop('s_old') * ct0 + s.pop('s_loc') * _dot_tn(s.pop('k'), s.pop('vw'))

    def st_out(s):
        num = s.pop('num')
        den = jnp.maximum(jnp.abs(num[:, ML_DV:]), s.pop('floor'))
        h = s['h']
        out_refs[s['d']][0, :, h * ML_DV:(h + 1) * ML_DV] = num[:, :ML_DV] / jnp.concatenate([den, den], axis=1)
        ct_ref[s['c']] = s.pop('ct')
        m_ref[s['c']] = s.pop('m_new')

    states = [{'d': d, 'h': h, 'c': d * ML_HEADS + h} for d in range(2) for h in range(ML_HEADS)]
    for stage in (st_qk, st_gate, st_max, st_exp, st_pv, st_out):
        for s in states:
            stage(s)


def _mlstm(p, gate_b, cos, sin, ctx_len):
    B, T, _ = p.shape
    L = CHUNK
    nch = T // L
    nctx = ctx_len // L
    H = ML_HEADS
    grow = jnp.swapaxes(p[:, :, MLG_0:MLG_0 + 4 * H].reshape(B, nch, L, 4 * H), 2, 3)

    def specs(d):
        ch = lambda s: _chunk_index(d, s, nctx, nch)
        return [pl.BlockSpec((1, L, H * ML_DQK), lambda b, s: (b, ch(s), ML_Q0 // (H * ML_DQK))),
                pl.BlockSpec((1, L, H * ML_DQK), lambda b, s: (b, ch(s), ML_K0 // (H * ML_DQK))),
                pl.BlockSpec((1, L, H * ML_DV), lambda b, s: (b, ch(s), ML_V0 // (H * ML_DV))),
                pl.BlockSpec((1, L, 128), lambda b, s: (b, ch(s), MLG_0 // 128)),
                pl.BlockSpec((1, 1, 4 * H, L), lambda b, s: (b, ch(s), 0, 0)),
                pl.BlockSpec((L, ML_DQK), lambda b, s: (ch(s), 0)),
                pl.BlockSpec((L, ML_DQK), lambda b, s: (ch(s), 0))]

    def ospec(d):
        return pl.BlockSpec((1, L, BRANCH_W), lambda b, s: (b, _chunk_index(d, s, nctx, nch), 0))

    hshape = jax.ShapeDtypeStruct((B, T, BRANCH_W), F32)
    args = (p, p, p, p, grow, cos, sin)
    return pl.pallas_call(
        _mlstm_kernel,
        grid=(B, nch),
        in_specs=[pl.BlockSpec(memory_space=pltpu.SMEM)] + specs(0) + specs(1),
        out_specs=[ospec(0), ospec(1)],
        out_shape=[hshape, hshape],
        scratch_shapes=[pltpu.VMEM((2 * H, ML_DQK, ML_DV + 128), F32), pltpu.VMEM((2 * H, 1, 1), F32)],
        compiler_params=_cparams(("parallel", "arbitrary")),
        name="mlstm",
    )(gate_b, *args, *args)


def _mlstm_post_kernel(hf_ref, hb_ref, o_ref, g_ref, y_ref):
    h = hf_ref[0] + hb_ref[0]
    h = h * lax.rsqrt(jnp.mean(h * h, axis=-1, keepdims=True) + EPS) * g_ref[...]
    y_ref[0] = (h * _sigmoid(o_ref[0])).astype(y_ref.dtype)


def _mlstm_post(hf, hb, p, norm_g):
    B, T, W = hf.shape
    tm = _tile(T, 1024, 16)
    hspec = pl.BlockSpec((1, tm, ML_DV), lambda b, i, h: (b, i, h))
    return pl.pallas_call(
        _mlstm_post_kernel,
        grid=(B, T // tm, ML_HEADS),
        in_specs=[hspec, hspec,
                  pl.BlockSpec((1, tm, ML_DV), lambda b, i, h: (b, i, ML_O0 // ML_DV + h)),
                  pl.BlockSpec((1, ML_DV), lambda b, i, h: (0, h))],
        out_specs=hspec,
        out_shape=jax.ShapeDtypeStruct((B, T, W), BF16),
        compiler_params=_cparams(("parallel", "parallel", "parallel")),
        name="mlstm_post",
    )(hf, hb, p, norm_g.reshape(1, W))


def _rwkv_prep_kernel(x_ref, xp_ref, xn_ref, mu_ref, w0_ref, a0_ref, wdh_ref, wdl_ref, wa_ref, wg_ref,
                      r_ref, k_ref, v_ref, ld_ref, ar_ref, g_ref, *, tt, ctx_len, nt):
    i = pl.program_id(1)
    x = x_ref[0][:, :RW_COLS]
    first = (i == 0) | (i * tt == ctx_len)
    last = ((i + 1) * tt == ctx_len) | (i == nt - 1)
    prev_row = jnp.where(first, 0.0, xp_ref[0][7:8, :RW_COLS])
    next_row = jnp.where(last, 0.0, xn_ref[0][0:1, :RW_COLS])
    ridx = lax.broadcasted_iota(jnp.int32, (tt, 1), 0)
    prev = jnp.where(ridx == 0, prev_row, pltpu.roll(x, 1, 0))
    nxt = jnp.where(ridx == tt - 1, next_row, pltpu.roll(x, tt - 1, 0))
    xs = x + mu_ref[...] * (0.5 * (prev + nxt) - x)

    W = BRANCH_W
    r_ref[0] = xs[:, 0:W]
    k_ref[0] = xs[:, W:2 * W]
    v_ref[0] = xs[:, 2 * W:3 * W]
    wlo = _mm3(_split2(jnp.tanh(xs[:, 3 * W:3 * W + 256])), (wdh_ref[...], wdl_ref[...]))
    alo = _dot(xs[:, 3 * W + 128:3 * W + 384].astype(BF16), wa_ref[...])
    for d in range(2):
        ld_ref[d, 0] = -math.exp(-0.5) * _sigmoid(w0_ref[d:d + 1, :] + wlo[:, d * W:(d + 1) * W])
        ar_ref[d, 0] = _sigmoid(a0_ref[d:d + 1, :] + alo[:, d * W:(d + 1) * W])
    g_ref[0] = _dot(_sigmoid(xs[:, 3 * W + 384:3 * W + 640]).astype(BF16), wg_ref[...])


def _rwkv_prep(p, prm, ctx_len):
    B, T, _ = p.shape
    W = BRANCH_W
    tt = _tile(math.gcd(T, ctx_len), 256, 8)
    nt = T // tt
    h8 = tt // 8
    nb8 = T // 8
    z = lambda *s: jnp.zeros(s, F32)
    wd = jnp.concatenate([
        jnp.concatenate([prm['rw_w_up'][0], z(RW_LORA, W)], axis=1),
        jnp.concatenate([z(RW_LORA, W), prm['rw_w_up'][1]], axis=1),
        z(256 - 2 * RW_LORA, 2 * W)], axis=0)
    wa = jnp.concatenate([
        z(2 * RW_LORA - 128, 2 * W),
        jnp.concatenate([prm['rw_a_up'][0], z(RW_LORA, W)], axis=1),
        jnp.concatenate([z(RW_LORA, W), prm['rw_a_up'][1]], axis=1)], axis=0)
    wg = prm['rw_g_up'].astype(BF16)
    wdh, wdl = _split2(wd)
    wa = wa.astype(BF16)
    mu = prm['rw_mu'].reshape(1, RW_COLS)
    kern = functools.partial(_rwkv_prep_kernel, tt=tt, ctx_len=ctx_len, nt=nt)
    rwblk = RW_0 // RW_SPAN
    tok = jax.ShapeDtypeStruct((B, T, W), F32)
    tok2 = jax.ShapeDtypeStruct((2, B, T, W), F32)
    full = lambda shape: pl.BlockSpec(shape, lambda b, i: (0,) * len(shape))
    ospec = pl.BlockSpec((1, tt, W), lambda b, i: (b, i, 0))
    ospec2 = pl.BlockSpec((2, 1, tt, W), lambda b, i: (0, b, i, 0))
    return pl.pallas_call(
        kern,
        grid=(B, nt),
        in_specs=[pl.BlockSpec((1, tt, RW_SPAN), lambda b, i: (b, i, rwblk)),
                  pl.BlockSpec((1, 8, RW_SPAN), lambda b, i: (b, jnp.maximum(i * h8 - 1, 0), rwblk)),
                  pl.BlockSpec((1, 8, RW_SPAN), lambda b, i: (b, jnp.minimum((i + 1) * h8, nb8 - 1), rwblk)),
                  full((1, RW_COLS)), full((2, W)), full((2, W)),
                  full((256, 2 * W)), full((256, 2 * W)), full((256, 2 * W)), full((256, W))],
        out_specs=[ospec, ospec, ospec, ospec2, ospec2, ospec],
        out_shape=[tok, tok, tok, tok2, tok2, tok],
        compiler_params=_cparams(("parallel", "parallel")),
        name="rwkv_prep",
    )(p, p, p, mu, prm['rw_w0'], prm['rw_a0'], wdh, wdl, wa, wg)


def _split2(x):
    hi = x.astype(BF16)
    return hi, (x - hi.astype(F32)).astype(BF16)


def _split3(x):
    hi = x.astype(BF16)
    r1 = x - hi.astype(F32)
    mid = r1.astype(BF16)
    return hi, mid, (r1 - mid.astype(F32)).astype(BF16)


def _split1(x):
    return (x.astype(BF16),)


def _mm3(a, b, f=_dot):
    acc = f(a[0], b[0])
    if len(b) > 1:
        acc = acc + f(a[0], b[1])
    if len(a) > 1:
        acc = acc + f(a[1], b[0])
    return acc


def _mm_exact(e, parts, f=_dot):
    acc = f(e, parts[0])
    for x in parts[1:]:
        acc = acc + f(e, x)
    return acc


def _rwkv_scan_kernel(r_ref, k_ref, v_ref, ld_ref, ar_ref, kk_ref, ka_ref, y_ref, st_ref, *, n_pairs):
    L = CHUNK
    P = 2 * RW_DH
    d = pl.program_id(0)

    @pl.when(pl.program_id(2) == 0)
    def _():
        st_ref[...] = jnp.zeros_like(st_ref)

    row = lax.broadcasted_iota(jnp.int32, (L, P), 0)
    col = lax.broadcasted_iota(jnp.int32, (L, P), 1) % RW_DH
    diff = (row - col) * jnp.where(d == 0, 1, -1)
    before = diff > 0
    incl = diff >= 0
    tri = incl[:, :L].astype(BF16)
    eye = (row == col).astype(F32)
    ones = jnp.ones((L, P), BF16)
    lane = lax.broadcasted_iota(jnp.int32, (1, P), 1)
    m0 = lane < RW_DH
    r2 = lax.broadcasted_iota(jnp.int32, (P, P), 0) // RW_DH
    c2 = lax.broadcasted_iota(jnp.int32, (P, P), 1) // RW_DH
    same_head = r2 == c2
    ones_bd = same_head.astype(BF16)

    def bd(x):
        z = jnp.zeros_like(x)
        return jnp.concatenate([jnp.where(m0, x, z), jnp.where(m0, z, x)], axis=0)

    def bd2(x):
        return tuple(bd(t) for t in x)

    sp = spi = _split1

    def cat(xs, axis):
        return tuple(jnp.concatenate(list(t), axis=axis) for t in zip(*xs))

    def st_load(s):
        sl = s['sl']
        s['R'], s['K'], s['V'] = r_ref[0, :, sl], k_ref[0, :, sl], v_ref[0, :, sl]
        s['LD'], s['AR'] = ld_ref[0, 0, :, sl], ar_ref[0, 0, :, sl]
        s['S0'] = st_ref[s['j']]
        kk = s['K'] * kk_ref[:, sl]
        kk2 = kk * kk
        n0 = jnp.sum(jnp.where(m0, kk2, 0.0), axis=1, keepdims=True)
        n1 = jnp.sum(jnp.where(m0, 0.0, kk2), axis=1, keepdims=True)
        s['kk'] = kk / jnp.maximum(jnp.sqrt(jnp.where(m0, n0, n1)), 1e-12)

    def st_cum(s):
        s['KD'] = s.pop('K') * (1.0 + (s['AR'] - 1.0) * ka_ref[:, s['sl']])
        s['BV'] = s['kk'] * s.pop('AR')
        s['c'] = _mm_exact(tri, _split3(s['LD']))

    def st_x(s):
        c, LD = s.pop('c'), s.pop('LD')
        c_end = jnp.sum(LD, axis=0, keepdims=True)
        s['dcol'] = jnp.exp(jnp.broadcast_to(c_end, (P, P)).T)
        At = -s.pop('kk') * jnp.exp(c - LD)
        e_neg = jnp.exp(-c)
        BV, KD = s.pop('BV'), s.pop('KD')
        s['Rt'] = s.pop('R') * jnp.exp(c)
        e_end = jnp.exp(c_end - c)
        s['Bh_s'], s['Kh_s'] = sp(BV * e_end), sp(KD * e_end)
        s['At_s'], s['V_s'] = sp(At), sp(s.pop('V'))
        s['X'] = _mm3(cat([s['At_s'], sp(s['Rt'])], 0),
                      cat([bd2(sp(BV * e_neg)), bd2(sp(KD * e_neg))], 0), _dot_nt)

    def st_inv0(s):
        X = s.pop('X')
        Nab = jnp.where(before, X[:L, :P], 0.0)
        s['Nak_s'] = sp(jnp.where(before, X[:L, P:], 0.0))
        s['Mrb_s'] = sp(jnp.where(incl, X[L:, :P], 0.0))
        s['Mrk_s'] = sp(jnp.where(incl, X[L:, P:], 0.0))
        s['T'] = eye + Nab
        Np_s = spi(Nab)
        s['Z'] = _mm3(Np_s, bd2(Np_s))
        s['NV'] = _mm3(s.pop('Nak_s'), bd2(s['V_s']))

    def st_inv1(s):
        Z = s.pop('Z')
        if Z.shape[0] == 2 * L:
            s['T'] = s['T'] + Z[L:]
        Np_s = spi(Z[:L])
        s['Z'] = _mm3(cat([Np_s, spi(s['T'])], 0), bd2(Np_s))

    def st_inv2(s):
        Z = s.pop('Z')
        s['T'] = s['T'] + Z[L:]
        s['Z'] = _mm3(spi(s['T']), bd2(spi(Z[:L])))

    def st_wu(s):
        T_s = sp(s.pop('T') + s.pop('Z'))
        s['WU'] = _mm3(T_s, cat([bd2(s.pop('At_s')), bd2(sp(s.pop('NV')))], 1))

    def st_loc(s):
        WU = s.pop('WU')
        W_s, Ul_s = sp(WU[:, :P]), sp(WU[:, P:])
        MM = _mm3(s.pop('Mrb_s'), cat([bd2(W_s), bd2(Ul_s)], 1))
        s['Q_s'] = sp(s.pop('Rt') + MM[:, :P])
        s['Yl'] = MM[:, P:] + _mm3(s.pop('Mrk_s'), bd2(s['V_s']))
        PS = _mm3(s.pop('Bh_s'), cat([W_s, Ul_s], 1), _dot_tn)
        s['Pm_s'] = sp(jnp.where(same_head, PS[:, :P], 0.0))
        s['Sl'] = jnp.where(same_head, PS[:, P:] + _mm3(s.pop('Kh_s'), s.pop('V_s'), _dot_tn), 0.0)

    def st_out(s):
        S0 = s.pop('S0')
        S0_s = sp(S0)
        y_ref[0, 0, :, s['sl']] = _mm3(s.pop('Q_s'), S0_s) + s.pop('Yl')
        st_ref[s['j']] = s.pop('dcol') * S0 + _mm3(s.pop('Pm_s'), S0_s) + s.pop('Sl')

    n_dbl = int(math.log2(L)) - 2
    stages = [st_load, st_cum, st_x, st_inv0] + [st_inv1] * n_dbl + [st_inv2, st_wu, st_loc, st_out]
    states = [{'j': j, 'sl': slice(P * j, P * (j + 1))} for j in range(n_pairs)]
    for stage in stages:
        for s in states:
            stage(s)


def _rwkv_scan(r, k, v, ld, ar, k_k, k_a, ctx_len):
    B, T, W = r.shape
    L = CHUNK
    nch = T // L
    nctx = ctx_len // L

    def smap(d, b, s):
        return (b, _chunk_index(d, s, nctx, nch), 0)

    def dmap(d, b, s):
        return (d, b, _chunk_index(d, s, nctx, nch), 0)

    pspec = pl.BlockSpec((1, W), lambda d, b, s: (0, 0))
    kern = functools.partial(_rwkv_scan_kernel, n_pairs=W // (2 * RW_DH))
    return pl.pallas_call(
        kern,
        grid=(2, B, nch),
        in_specs=[pl.BlockSpec((1, L, W), smap), pl.BlockSpec((1, L, W), smap), pl.BlockSpec((1, L, W), smap),
                  pl.BlockSpec((1, 1, L, W), dmap), pl.BlockSpec((1, 1, L, W), dmap),
                  pspec, pspec],
        out_specs=pl.BlockSpec((1, 1, L, W), dmap),
        out_shape=jax.ShapeDtypeStruct((2, B, T, W), F32),
        scratch_shapes=[pltpu.VMEM((W // (2 * RW_DH), 2 * RW_DH, 2 * RW_DH), F32)],
        compiler_params=_cparams(("parallel", "parallel", "arbitrary")),
        name="rwkv_scan",
    )(r, k, v, ld, ar, k_k.reshape(1, W), k_a.reshape(1, W))


def _rwkv_post_kernel(yf_ref, yb_ref, r_ref, k_ref, v_ref, g_ref, rk_ref, lw_ref, lb_ref, o_ref):
    P = 2 * RW_DH
    r2 = lax.broadcasted_iota(jnp.int32, (P, P), 0) // RW_DH
    c2 = lax.broadcasted_iota(jnp.int32, (P, P), 1) // RW_DH
    ones_bd = (r2 == c2).astype(BF16)

    def head_sum(x):
        return _mm_exact(ones_bd, _split2(x), lambda e, t: _dot(t, e))

    for j in range(o_ref.shape[2] // P):
        sl = slice(P * j, P * (j + 1))
        y = yf_ref[0, 0, :, sl] + yb_ref[0, 0, :, sl]
        yc = y - head_sum(y) * (1.0 / RW_DH)
        var = head_sum(yc * yc) * (1.0 / RW_DH)
        yn = yc * lax.rsqrt(var + RW_GN_EPS)
        bonus = head_sum(r_ref[0, :, sl] * k_ref[0, :, sl] * rk_ref[:, sl]) * v_ref[0, :, sl]
        o_ref[0, :, sl] = ((yn * lw_ref[:, sl] + lb_ref[:, sl] + bonus) * g_ref[0, :, sl]).astype(o_ref.dtype)


def _rwkv_post(y2, r, k, v, g, r_k, ln_w, ln_b):
    _, B, T, W = y2.shape
    tt = _tile(T, 512, 16)
    yspec = lambda d: pl.BlockSpec((1, 1, tt, W), lambda b, i: (d, b, i, 0))
    tspec = pl.BlockSpec((1, tt, W), lambda b, i: (b, i, 0))
    pspec = pl.BlockSpec((1, W), lambda b, i: (0, 0))
    return pl.pallas_call(
        _rwkv_post_kernel,
        grid=(B, T // tt),
        in_specs=[yspec(0), yspec(1), tspec, tspec, tspec, tspec, pspec, pspec, pspec],
        out_specs=tspec,
        out_shape=jax.ShapeDtypeStruct((B, T, W), BF16),
        compiler_params=_cparams(("parallel", "parallel")),
        name="rwkv_post",
    )(y2, y2, r, k, v, g, r_k.reshape(1, W), ln_w.reshape(1, W), ln_b.reshape(1, W))


def _rwkv_branch(p, prm, ctx_len):
    r, k, v, ld, ar, g = _rwkv_prep(p, prm, ctx_len)
    y2 = _rwkv_scan(r, k, v, ld, ar, prm['rw_k_k'], prm['rw_k_a'], ctx_len)
    return _rwkv_post(y2, r, k, v, g, prm['rw_r_k'], prm['rw_ln_w'], prm['rw_ln_b'])


NA_RB = 4
NA_SLAB = NA_RB + NA_WIN_R - 1
NA_HB = 2


def _na_kernel(q_ref, k_ref, v_ref, bias_ref, o_ref, *, ctx_len, rows):
    rb = pl.program_id(2)
    scale = NA_DH ** -0.5
    heads = [slice(i * NA_DH, (i + 1) * NA_DH) for i in range(NA_HB)]

    @pl.when(rb == 0)
    def _():
        for hs in heads:
            s_ctx = _dot_nt(q_ref[0, :, hs].astype(BF16), k_ref[0, 0:ctx_len, hs].astype(BF16)) * scale
            p_ctx = jnp.exp(s_ctx - jnp.max(s_ctx, axis=-1, keepdims=True))
            y = _dot(p_ctx.astype(BF16), v_ref[0, 0:ctx_len, hs].astype(BF16))
            o_ref[0, :, hs] = (y / jnp.sum(p_ctx, axis=-1, keepdims=True)).astype(o_ref.dtype)

    @pl.when(rb > 0)
    def _():
        nslab = NA_SLAB * GRID_W
        start = jnp.clip((rb - 1) * NA_RB - NA_WIN_R // 2, 0, rows - NA_SLAB)
        koff = pl.multiple_of(ctx_len + start * GRID_W, GRID_W)

        def st_scores(s):
            hs = s['hs']
            q = q_ref[0, :, hs].astype(BF16)
            s['s_win'] = _dot_nt(q, k_ref[0, pl.ds(koff, nslab), hs].astype(BF16)) * scale + bias_ref[0, s['i'], 0]
            s['s_ctx'] = _dot_nt(q, k_ref[0, 0:ctx_len, hs].astype(BF16)) * scale

        def st_softmax(s):
            s_win, s_ctx = s.pop('s_win'), s.pop('s_ctx')
            m = jnp.maximum(jnp.max(s_win, axis=-1, keepdims=True), jnp.max(s_ctx, axis=-1, keepdims=True))
            p_win = jnp.exp(s_win - m)
            p_ctx = jnp.exp(s_ctx - m)
            s['den'] = jnp.sum(p_win, axis=-1, keepdims=True) + jnp.sum(p_ctx, axis=-1, keepdims=True)
            hs = s['hs']
            s['y'] = (_dot(p_win.astype(BF16), v_ref[0, pl.ds(koff, nslab), hs].astype(BF16))
                      + _dot(p_ctx.astype(BF16), v_ref[0, 0:ctx_len, hs].astype(BF16)))

        def st_out(s):
            o_ref[0, :, s['hs']] = (s.pop('y') / s.pop('den')).astype(o_ref.dtype)

        states = [{'i': i, 'hs': hs} for i, hs in enumerate(heads)]
        for stage in (st_scores, st_softmax, st_out):
            for s in states:
                stage(s)


def _na_block_type(lb, nblk):
    return jnp.where(lb <= 0, 0, jnp.where(lb == nblk - 1, 2, 1))


def _na_bias_kernel(toe_ref, o_ref, *, ro, row_ok):
    neg = jnp.full((GRID_W, GRID_W), NEG, F32)
    for t in range(ro.shape[0]):
        for ri in range(NA_RB):
            for kj in range(NA_SLAB):
                tile = toe_ref[0, 0, int(ro[t, ri, kj])] if row_ok[t, ri, kj] else neg
                o_ref[0, 0, t, ri * GRID_W:(ri + 1) * GRID_W, kj * GRID_W:(kj + 1) * GRID_W] = tile


def _na_bias_table(rpb, rows):
    depth, H = rpb.shape[:2]
    c = np.arange(GRID_W)
    c_start = np.clip(c - NA_WIN_C // 2, 0, GRID_W - NA_WIN_C)
    col_ok = (c[None, :] >= c_start[:, None]) & (c[None, :] < c_start[:, None] + NA_WIN_C)
    col_off = np.clip(c[None, :] - c[:, None], 1 - NA_WIN_C, NA_WIN_C - 1) + (NA_WIN_C - 1)
    sel_c = jnp.asarray(col_off[..., None] == np.arange(2 * NA_WIN_C - 1), F32)
    toe = jnp.einsum('lhoc,qkc->lhoqk', rpb, sel_c, precision=HI) + jnp.asarray(np.where(col_ok, 0.0, NEG), F32)
    nblk = rows // NA_RB
    reps = np.array([0, min(1, nblk - 1), nblk - 1])
    r = reps[:, None] * NA_RB + np.arange(NA_RB)[None, :]
    slab0 = np.clip(reps * NA_RB - NA_WIN_R // 2, 0, rows - NA_SLAB)
    kr = slab0[:, None] + np.arange(NA_SLAB)[None, :]
    win0 = np.clip(r - NA_WIN_R // 2, 0, rows - NA_WIN_R)
    ro = kr[:, None, :] - r[:, :, None] + (NA_WIN_R - 1)
    row_ok = (kr[:, None, :] >= win0[:, :, None]) & (kr[:, None, :] < win0[:, :, None] + NA_WIN_R)
    kern = functools.partial(_na_bias_kernel, ro=ro, row_ok=row_ok)
    nq, nk = NA_RB * GRID_W, NA_SLAB * GRID_W
    return pl.pallas_call(
        kern,
        grid=(depth, H),
        in_specs=[pl.BlockSpec((1, 1, 2 * NA_WIN_R - 1, GRID_W, GRID_W), lambda l, h: (l, h, 0, 0, 0))],
        out_specs=pl.BlockSpec((1, 1, 3, nq, nk), lambda l, h: (l, h, 0, 0, 0)),
        out_shape=jax.ShapeDtypeStruct((depth, H, 3, nq, nk), F32),
        compiler_params=_cparams(("parallel", "parallel")),
        name="na_bias",
    )(toe)


def _na_branch(p, table, l, ctx_len):
    B, T, _ = p.shape
    seq = T - ctx_len
    rows = seq // GRID_W
    H = NA_HEADS
    nq = NA_RB * GRID_W
    assert ctx_len == nq and rows % NA_RB == 0 and rows >= NA_SLAB, (ctx_len, rows)
    nblk = rows // NA_RB
    wb = NA_HB * NA_DH
    qb, kb, vb = NA_0 // wb, (NA_0 + BRANCH_W) // wb, (NA_0 + 2 * BRANCH_W) // wb
    kern = functools.partial(_na_kernel, ctx_len=ctx_len, rows=rows)
    return pl.pallas_call(
        kern,
        grid=(B, H // NA_HB, 1 + nblk),
        in_specs=[pl.BlockSpec((1, nq, wb), lambda b, h, r: (b, r, qb + h)),
                  pl.BlockSpec((1, T, wb), lambda b, h, r: (b, 0, kb + h)),
                  pl.BlockSpec((1, T, wb), lambda b, h, r: (b, 0, vb + h)),
                  pl.BlockSpec((1, NA_HB, 1, nq, NA_SLAB * GRID_W),
                               lambda b, h, r: (l, h, _na_block_type(r - 1, nblk), 0, 0))],
        out_specs=pl.BlockSpec((1, nq, wb), lambda b, h, r: (b, r, h)),
        out_shape=jax.ShapeDtypeStruct((B, T, BRANCH_W), BF16),
        compiler_params=_cparams(("parallel", "parallel", "arbitrary")),
        name="na",
    )(p, p, p, table)


def _pack_kernel(w_ref, o_ref):
    ml, rw, na, gt = 3088, 3712, 3072, 6144
    o_rw, o_na, o_gt = ml, ml + rw, ml + rw + na
    w = w_ref[0]
    n = w.shape[0]
    o_ref[0, :, 0:NA_0] = w[:, 0:NA_0].astype(BF16)
    o_ref[0, :, NA_0:GATE_0] = w[:, o_na:o_na + na].astype(BF16)
    o_ref[0, :, GATE_0:RW_0] = w[:, o_gt:o_gt + gt].astype(BF16)
    o_ref[0, :, RW_0:MLG_0] = w[:, o_rw:o_rw + rw].astype(BF16)
    tail = jnp.concatenate([w[:, NA_0:ml], jnp.zeros((n, 128 - (ml - NA_0)), F32)], axis=1)
    o_ref[0, :, MLG_0:MLG_0 + 128] = tail.astype(BF16)
    o_ref[0, :, MLG_0 + 128:] = jnp.zeros((n, P_COLS - MLG_0 - 128), BF16)


def _pack_w_in(w_in, l):
    _, D, N = w_in.shape
    tr = _tile(D, 64, 16)
    return pl.pallas_call(
        _pack_kernel,
        grid=(D // tr,),
        in_specs=[pl.BlockSpec((1, tr, N), lambda i: (l, i, 0))],
        out_specs=pl.BlockSpec((1, tr, P_COLS), lambda i: (0, i, 0)),
        out_shape=jax.ShapeDtypeStruct((1, D, P_COLS), BF16),
        compiler_params=_cparams(("parallel",)),
        name="pack_w_in",
    )(w_in)


def _rope_tables(seq, ctx_len):
    t = jnp.arange(seq)
    nf = ML_DQK // 4
    freqs = ROPE_BASE ** (-jnp.arange(nf, dtype=F32) / nf)
    ang_r = (t // GRID_W).astype(F32)[:, None] * freqs[None, :]
    ang_c = (t % GRID_W).astype(F32)[:, None] * freqs[None, :]
    cos = jnp.concatenate([jnp.cos(ang_r)] * 2 + [jnp.cos(ang_c)] * 2, axis=1)
    sin = jnp.concatenate([-jnp.sin(ang_r), jnp.sin(ang_r), -jnp.sin(ang_c), jnp.sin(ang_c)], axis=1)
    cos = jnp.concatenate([jnp.ones((ctx_len, ML_DQK), F32), cos], axis=0)
    sin = jnp.concatenate([jnp.zeros((ctx_len, ML_DQK), F32), sin], axis=0)
    return cos, sin


def kernel(x, c, ctx, c_ctx, w_mod, b_mod, norm1_g, w_in, ml_gate_b, ml_norm_g, rw_mu, rw_w0, rw_w_up, rw_a0, rw_a_up, rw_g_up, rw_k_k, rw_k_a, rw_r_k, rw_ln_w, rw_ln_b, na_rpb, w_branch, w_out, norm2_g, w_ff1, w_ff2, final_g):
    B, seq, D = x.shape
    ctx_len = ctx.shape[1]
    depth = w_mod.shape[0]
    xs = jnp.concatenate([ctx, x], axis=1)
    s_all = jax.nn.silu(jnp.concatenate([c_ctx[None, :], c], axis=0))
    s_all = jnp.concatenate([s_all, jnp.zeros((8 - (B + 1) % 8, D), F32)], axis=0) if (B + 1) % 8 else s_all
    cos, sin = _rope_tables(seq, ctx_len)
    na_table = _na_bias_table(na_rpb, seq // GRID_W)
    wb_bf, wo_bf, w2_bf = (w.astype(BF16) for w in (w_branch, w_out, w_ff2))
    for l in range(depth):
        mod = _modulation(s_all, w_mod, b_mod, l).reshape(-1, 6, D)
        mc = jnp.broadcast_to(mod[0:1], (B, 6, D))
        ml_ = mod[1:B + 1]
        mod1 = jnp.stack([mc[:, 0], mc[:, 1], ml_[:, 0], ml_[:, 1]], axis=1)
        gate1 = jnp.stack([mc[:, 2], ml_[:, 2]], axis=1)
        mod2 = jnp.stack([mc[:, 3], mc[:, 4], ml_[:, 3], ml_[:, 4]], axis=1)
        gate2 = jnp.stack([mc[:, 5], ml_[:, 5]], axis=1)
        prm = {'rw_mu': rw_mu[l], 'rw_w0': rw_w0[l], 'rw_w_up': rw_w_up[l], 'rw_a0': rw_a0[l],
               'rw_a_up': rw_a_up[l], 'rw_g_up': rw_g_up[l], 'rw_k_k': rw_k_k[l], 'rw_k_a': rw_k_a[l],
               'rw_r_k': rw_r_k[l], 'rw_ln_w': rw_ln_w[l], 'rw_ln_b': rw_ln_b[l]}

        p = _mm(_norm_mod(xs, norm1_g[l], mod1, ctx_len), _pack_w_in(w_in, l), 0, F32, False)
        h_f, h_b = _mlstm(p, ml_gate_b[l], cos, sin, ctx_len)
        y_m = _mlstm_post(h_f, h_b, p, ml_norm_g[l])
        y_r = _rwkv_branch(p, prm, ctx_len)
        y_n = _na_branch(p, na_table, l, ctx_len)
        merged = _merge(y_m, y_r, y_n, p, wb_bf, l)
        xs = _mm_res(merged, wo_bf, l, xs, gate1, ctx_len)
        hff = _mm(_norm_mod(xs, norm2_g[l], mod2, ctx_len), w_ff1, l, BF16, True)
        xs = _mm_res(hff, w2_bf, l, xs, gate2, ctx_len)
    return _final_norm(xs, final_g, ctx_len)
```

```python
import functools
import math

import jax
import jax.numpy as jnp
import numpy as np
from jax import lax
from jax.experimental import pallas as pl
from jax.experimental.pallas import tpu as pltpu

F32 = jnp.float32
BF16 = jnp.bfloat16
HI = lax.Precision.HIGHEST

D_MODEL = 2048
GRID_W = 64
BRANCH_W = D_MODEL // 2
N_BRANCH = 3
ML_HEADS = 4
ML_DV = BRANCH_W // ML_HEADS
ML_DQK = ML_DV // 2
RW_DH = 64
RW_HEADS = BRANCH_W // RW_DH
RW_LORA = 96
RW_GATE_LORA = 256
RW_GN_EPS = 64e-5
NA_DH = 128
NA_HEADS = BRANCH_W // NA_DH
NA_WIN_R = 8
NA_WIN_C = 16
D_FF = 4 * D_MODEL
ROPE_BASE = 10000.0
EPS = 1e-6
CHUNK = 64
NEG = -1e30

ML_Q0, ML_K0, ML_V0, ML_O0 = 0, 512, 1024, 2048
NA_0 = 3072
GATE_0 = 6144
RW_0 = 12288
RW_SPAN = 4096
RW_COLS = 3712
MLG_0 = RW_0 + RW_COLS
P_COLS = RW_0 + RW_SPAN

VMEM_LIMIT = 56 * 1024 * 1024
MM_ROWS = 1100


def _cparams(sem):
    return pltpu.CompilerParams(dimension_semantics=sem, vmem_limit_bytes=VMEM_LIMIT)


def _tile(n, cap, mult):
    best = None
    for t in range(mult, min(n, cap) + 1, mult):
        if n % t == 0:
            best = t
    assert best is not None, (n, cap, mult)
    return best


def _dot(a, b, prec=None):
    return jnp.dot(a, b, preferred_element_type=F32, precision=prec)


def _dot_nt(a, b, prec=None):
    return lax.dot_general(a, b, (((1,), (1,)), ((), ())), preferred_element_type=F32, precision=prec)


def _dot_tn(a, b, prec=None):
    return lax.dot_general(a, b, (((0,), (0,)), ((), ())), preferred_element_type=F32, precision=prec)


def _sigmoid(x):
    return 1.0 / (1.0 + jnp.exp(-x))


def _softplus(x):
    return jnp.maximum(x, 0.0) + jnp.log(1.0 + jnp.exp(-jnp.abs(x)))


def _mod_kernel(s_ref, w_ref, b_ref, o_ref):
    o_ref[...] = _dot(s_ref[...], w_ref[0], HI) + b_ref[0]


def _modulation(s, w, b, l):
    R, D = s.shape
    N = w.shape[2]
    tn = _tile(N, 1024, 128)
    return pl.pallas_call(
        _mod_kernel,
        grid=(N // tn,),
        in_specs=[pl.BlockSpec((R, D), lambda j: (0, 0)),
                  pl.BlockSpec((1, D, tn), lambda j: (l, 0, j)),
                  pl.BlockSpec((1, 1, tn), lambda j: (l, 0, j))],
        out_specs=pl.BlockSpec((R, tn), lambda j: (0, j)),
        out_shape=jax.ShapeDtypeStruct((R, N), F32),
        compiler_params=_cparams(("arbitrary",)),
        name="modulation",
    )(s, w, b.reshape(b.shape[0], 1, N))


def _norm_mod_kernel(x_ref, g_ref, mod_ref, o_ref, *, tm, ctx_len):
    x = x_ref[0]
    y = x * lax.rsqrt(jnp.mean(x * x, axis=-1, keepdims=True) + EPS) * g_ref[...]
    row = pl.program_id(1) * tm + lax.broadcasted_iota(jnp.int32, (tm, 1), 0)
    is_ctx = row < ctx_len
    m = mod_ref[0]
    shift = jnp.where(is_ctx, m[0:1], m[2:3])
    scale = jnp.where(is_ctx, m[1:2], m[3:4])
    o_ref[0] = (y * (1.0 + scale) + shift).astype(o_ref.dtype)


def _norm_mod(x, g, mod4, ctx_len):
    B, T, D = x.shape
    tm = _tile(T, 320, 16)
    kern = functools.partial(_norm_mod_kernel, tm=tm, ctx_len=ctx_len)
    return pl.pallas_call(
        kern,
        grid=(B, T // tm),
        in_specs=[pl.BlockSpec((1, tm, D), lambda b, i: (b, i, 0)),
                  pl.BlockSpec((1, D), lambda b, i: (0, 0)),
                  pl.BlockSpec((1, 4, D), lambda b, i: (b, 0, 0))],
        out_specs=pl.BlockSpec((1, tm, D), lambda b, i: (b, i, 0)),
        out_shape=jax.ShapeDtypeStruct((B, T, D), BF16),
        compiler_params=_cparams(("parallel", "parallel")),
        name="norm_mod",
    )(x, g.reshape(1, D), mod4)


def _mm_kernel(h_ref, w_ref, o_ref, *, relu2):
    acc = _dot(h_ref[0], w_ref[0].astype(BF16))
    if relu2:
        acc = jnp.square(jnp.maximum(acc, 0.0))
    o_ref[0] = acc.astype(o_ref.dtype)


def _mm(h, w, l, out_dtype, relu2):
    B, T, D = h.shape
    N = w.shape[2]
    tm = _tile(T, 2 * MM_ROWS, 16)
    tn = _tile(N, 512, 128)
    kern = functools.partial(_mm_kernel, relu2=relu2)
    return pl.pallas_call(
        kern,
        grid=(B, T // tm, N // tn),
        in_specs=[pl.BlockSpec((1, tm, D), lambda b, i, j: (b, i, 0)),
                  pl.BlockSpec((1, D, tn), lambda b, i, j: (l, 0, j))],
        out_specs=pl.BlockSpec((1, tm, tn), lambda b, i, j: (b, i, j)),
        out_shape=jax.ShapeDtypeStruct((B, T, N), out_dtype),
        compiler_params=_cparams(("parallel", "parallel", "arbitrary")),
        name="mm",
    )(h, w)


def _row_is_ctx(tm, ctx_len):
    row = pl.program_id(1) * tm + lax.broadcasted_iota(jnp.int32, (tm, 1), 0)
    return row < ctx_len


def _mm_res_kernel(a_ref, w_ref, x_ref, gate_ref, o_ref, *, tm, ctx_len):
    gt = gate_ref[0]
    gate = jnp.where(_row_is_ctx(tm, ctx_len), gt[0:1], gt[1:2])
    o_ref[0] = x_ref[0] + gate * _dot(a_ref[0], w_ref[0])


def _mm_res(a, w, l, x, gate2, ctx_len):
    B, T, K = a.shape
    D = w.shape[2]
    tm = _tile(T, MM_ROWS // 2, 16)
    tn = _tile(D, 256, 128)
    kern = functools.partial(_mm_res_kernel, tm=tm, ctx_len=ctx_len)
    return pl.pallas_call(
        kern,
        grid=(B, T // tm, D // tn),
        in_specs=[pl.BlockSpec((1, tm, K), lambda b, i, j: (b, i, 0)),
                  pl.BlockSpec((1, K, tn), lambda b, i, j: (l, 0, j)),
                  pl.BlockSpec((1, tm, tn), lambda b, i, j: (b, i, j)),
                  pl.BlockSpec((1, 2, tn), lambda b, i, j: (b, 0, j))],
        out_specs=pl.BlockSpec((1, tm, tn), lambda b, i, j: (b, i, j)),
        out_shape=jax.ShapeDtypeStruct((B, T, D), F32),
        compiler_params=_cparams(("parallel", "parallel", "arbitrary")),
        name="mm_res",
    )(a, w, x, gate2)


def _mm_res_norm_kernel(a_ref, w_ref, x_ref, gate_ref, g_ref, mod_ref, o_ref, h_ref, *, tm, ctx_len):
    is_ctx = _row_is_ctx(tm, ctx_len)
    gt = gate_ref[0]
    xs = x_ref[0] + jnp.where(is_ctx, gt[0:1], gt[1:2]) * _dot(a_ref[0], w_ref[0])
    o_ref[0] = xs
    y = xs * lax.rsqrt(jnp.mean(xs * xs, axis=-1, keepdims=True) + EPS) * g_ref[...]
    m = mod_ref[0]
    shift = jnp.where(is_ctx, m[0:1], m[2:3])
    scale = jnp.where(is_ctx, m[1:2], m[3:4])
    h_ref[0] = (y * (1.0 + scale) + shift).astype(h_ref.dtype)


def _mm_res_norm(a, w, l, x, gate2, g, mod4, ctx_len):
    B, T, K = a.shape
    D = w.shape[2]
    tm = _tile(T, MM_ROWS // 4, 16)
    kern = functools.partial(_mm_res_norm_kernel, tm=tm, ctx_len=ctx_len)
    row = lambda width: pl.BlockSpec((1, tm, width), lambda b, i: (b, i, 0))
    return pl.pallas_call(
        kern,
        grid=(B, T // tm),
        in_specs=[row(K),
                  pl.BlockSpec((1, K, D), lambda b, i: (l, 0, 0)),
                  row(D),
                  pl.BlockSpec((1, 2, D), lambda b, i: (b, 0, 0)),
                  pl.BlockSpec((1, D), lambda b, i: (0, 0)),
                  pl.BlockSpec((1, 4, D), lambda b, i: (b, 0, 0))],
        out_specs=[row(D), row(D)],
        out_shape=[jax.ShapeDtypeStruct((B, T, D), F32), jax.ShapeDtypeStruct((B, T, D), BF16)],
        compiler_params=_cparams(("parallel", "parallel")),
        name="mm_res_norm",
    )(a, w, x, gate2, g.reshape(1, D), mod4)


def _merge_kernel(ym_ref, yr_ref, yn_ref, g0_ref, g1_ref, g2_ref, wb_ref, o_ref):
    acc = _sigmoid(g0_ref[0]) * _dot(ym_ref[0], wb_ref[0, 0])
    acc += _sigmoid(g1_ref[0]) * _dot(yr_ref[0], wb_ref[0, 1])
    acc += _sigmoid(g2_ref[0]) * _dot(yn_ref[0], wb_ref[0, 2])
    o_ref[0] = acc.astype(o_ref.dtype)


def _merge(ym, yr, yn, p, wb, l):
    B, T, W = ym.shape
    D = wb.shape[3]
    tm = _tile(T, MM_ROWS, 16)
    tn = _tile(D, 512, 128)
    yspec = pl.BlockSpec((1, tm, W), lambda b, i, j: (b, i, 0))

    def gspec(n):
        base = (GATE_0 + n * D) // tn
        return pl.BlockSpec((1, tm, tn), lambda b, i, j: (b, i, base + j))

    return pl.pallas_call(
        _merge_kernel,
        grid=(B, T // tm, D // tn),
        in_specs=[yspec, yspec, yspec, gspec(0), gspec(1), gspec(2),
                  pl.BlockSpec((1, N_BRANCH, W, tn), lambda b, i, j: (l, 0, 0, j))],
        out_specs=pl.BlockSpec((1, tm, tn), lambda b, i, j: (b, i, j)),
        out_shape=jax.ShapeDtypeStruct((B, T, D), BF16),
        compiler_params=_cparams(("parallel", "parallel", "arbitrary")),
        name="merge",
    )(ym, yr, yn, p, p, p, wb)


def _final_norm_kernel(x_ref, g_ref, o_ref):
    x = x_ref[0]
    o_ref[0] = x * lax.rsqrt(jnp.mean(x * x, axis=-1, keepdims=True) + EPS) * g_ref[...]


def _final_norm(xs, g, ctx_len):
    B, T, D = xs.shape
    seq = T - ctx_len
    tm = _tile(math.gcd(seq, ctx_len), 256, 8)
    off = ctx_len // tm
    return pl.pallas_call(
        _final_norm_kernel,
        grid=(B, seq // tm),
        in_specs=[pl.BlockSpec((1, tm, D), lambda b, i: (b, off + i, 0)),
                  pl.BlockSpec((1, D), lambda b, i: (0, 0))],
        out_specs=pl.BlockSpec((1, tm, D), lambda b, i: (b, i, 0)),
        out_shape=jax.ShapeDtypeStruct((B, seq, D), F32),
        compiler_params=_cparams(("parallel", "parallel")),
        name="final_norm",
    )(xs, g.reshape(1, D))


def _chunk_index(d, s, n_ctx, n_all):
    back = jnp.where(s < n_ctx, n_ctx - 1 - s, n_all + n_ctx - 1 - s)
    return jnp.where(d == 0, s, back)


def _rope(x, cos, sin_signed, first_half):
    swapped = jnp.where(first_half, pltpu.roll(x, 96, 1), pltpu.roll(x, 32, 1))
    return x * cos + swapped * sin_signed


def _log_sigmoid(x):
    return jnp.minimum(x, 0.0) - jnp.log(1.0 + jnp.exp(-jnp.abs(x)))


def _mlstm_kernel(bias_ref, *refs):
    L = CHUNK
    n_in = 7
    dir_refs = (refs[:n_in], refs[n_in:2 * n_in])
    out_refs = refs[2 * n_in:2 * n_in + 2]
    ct_ref, m_ref = refs[2 * n_in + 2:]

    @pl.when(pl.program_id(1) == 0)
    def _():
        ct_ref[...] = jnp.zeros_like(ct_ref)
        m_ref[...] = jnp.zeros_like(m_ref)

    row = lax.broadcasted_iota(jnp.int32, (L, L), 0)
    col = lax.broadcasted_iota(jnp.int32, (L, L), 1)
    lane = lax.broadcasted_iota(jnp.int32, (L, ML_DQK), 1)
    first_half = (lane % 64) < 32
    ones = jnp.ones((L, 128), BF16)

    def st_qk(s):
        d, h = s['d'], s['h']
        q_ref, k_ref, v_ref, _, _, cos_ref, sin_ref = dir_refs[d]
        cos, sin = cos_ref[...], sin_ref[...]
        hq = slice(h * ML_DQK, (h + 1) * ML_DQK)
        s['q'] = _rope(q_ref[0, :, hq] * (ML_DQK ** -0.5), cos, sin, first_half).astype(BF16)
        s['k'] = _rope(k_ref[0, :, hq], cos, sin, first_half).astype(BF16)
        s['v'] = v_ref[0, :, h * ML_DV:(h + 1) * ML_DV]
        s['qk'] = _dot_nt(s['q'], s['k'])

    def st_gate(s):
        d, h = s['d'], s['h']
        gc_ref, gr_ref = dir_refs[d][3:5]
        b_i = bias_ref[8 * d + h]
        b_f = bias_ref[8 * d + 4 + h]
        ji, jf = 8 * d + h, 8 * d + 4 + h
        gc = gc_ref[0]
        gr = gr_ref[0, 0]
        s['ig_c'] = gc[:, ji:ji + 1] + b_i
        lf_c = _log_sigmoid(gc[:, jf:jf + 1] + b_f)
        s['ig_r'] = gr[ji:ji + 1, :] + b_i
        lf_r = _log_sigmoid(gr[jf:jf + 1, :] + b_f)
        s['incl'] = (col <= row) if d == 0 else (col >= row)
        incl_t = (row <= col) if d == 0 else (row >= col)
        s['b_c'] = jnp.sum(jnp.where(s['incl'], lf_r, 0.0), axis=1, keepdims=True)
        s['b_r'] = jnp.sum(jnp.where(incl_t, lf_c, 0.0), axis=0, keepdims=True)
        s['b_end'] = jnp.sum(lf_r, axis=1, keepdims=True)

    def st_max(s):
        b_c, b_r, b_end, ig_r = s['b_c'], s.pop('b_r'), s['b_end'], s.pop('ig_r')
        s['a_c'] = b_end - b_c + s.pop('ig_c')
        s['a_max'] = jnp.max(b_end - b_r + ig_r, axis=1, keepdims=True)
        s['log_intra'] = jnp.where(s.pop('incl'), b_c - b_r + ig_r, NEG)
        s['mx'] = jnp.max(s['log_intra'], axis=1, keepdims=True)

    def st_exp(s):
        m0 = m_ref[s['c']]
        b_end, a_max = s.pop('b_end'), s.pop('a_max')
        log_inter = s.pop('b_c') + m0
        m_j = jnp.maximum(log_inter, s.pop('mx'))
        s['decay'] = jnp.exp(s.pop('log_intra') - m_j)
        s['w_inter'] = jnp.exp(log_inter - m_j)
        s['floor'] = jnp.exp(-m_j)
        s['m_new'] = jnp.maximum(b_end + m0, a_max)
        s['s_old'] = jnp.exp(b_end + m0 - s['m_new'])
        s['s_loc'] = jnp.exp(a_max - s['m_new'])
        v = s.pop('v')
        wgt = jnp.exp(s.pop('a_c') - a_max)
        s['vext'] = jnp.concatenate([v.astype(BF16), ones], axis=1)
        s['vw'] = jnp.concatenate([v * wgt, jnp.broadcast_to(wgt, (L, 128))], axis=1).astype(BF16)

    def st_pv(s):
        smat = (s.pop('qk') * s.pop('decay')).astype(BF16)
        ct0 = ct_ref[s['c']]
        s['num'] = _dot(smat, s.pop('vext')) + s.pop('w_inter') * _dot(s.pop('q'), ct0.astype(BF16))
        s['ct'] = s.pop('s_old') * ct0 + s.pop('s_loc') * _dot_tn(s.pop('k'), s.pop('vw'))

    def st_out(s):
        num = s.pop('num')
        den = jnp.maximum(jnp.abs(num[:, ML_DV:]), s.pop('floor'))
        h = s['h']
        out_refs[s['d']][0, :, h * ML_DV:(h + 1) * ML_DV] = num[:, :ML_DV] / jnp.concatenate([den, den], axis=1)
        ct_ref[s['c']] = s.pop('ct')
        m_ref[s['c']] = s.pop('m_new')

    states = [{'d': d, 'h': h, 'c': d * ML_HEADS + h} for d in range(2) for h in range(ML_HEADS)]
    for stage in (st_qk, st_gate, st_max, st_exp, st_pv, st_out):
        for s in states:
            stage(s)


def _mlstm(p, gate_b, cos, sin, ctx_len):
    B, T, _ = p.shape
    L = CHUNK
    nch = T // L
    nctx = ctx_len // L
    H = ML_HEADS
    grow = jnp.swapaxes(p[:, :, MLG_0:MLG_0 + 4 * H].reshape(B, nch, L, 4 * H), 2, 3)

    def specs(d):
        ch = lambda s: _chunk_index(d, s, nctx, nch)
        return [pl.BlockSpec((1, L, H * ML_DQK), lambda b, s: (b, ch(s), ML_Q0 // (H * ML_DQK))),
                pl.BlockSpec((1, L, H * ML_DQK), lambda b, s: (b, ch(s), ML_K0 // (H * ML_DQK))),
                pl.BlockSpec((1, L, H * ML_DV), lambda b, s: (b, ch(s), ML_V0 // (H * ML_DV))),
                pl.BlockSpec((1, L, 128), lambda b, s: (b, ch(s), MLG_0 // 128)),
                pl.BlockSpec((1, 1, 4 * H, L), lambda b, s: (b, ch(s), 0, 0)),
                pl.BlockSpec((L, ML_DQK), lambda b, s: (ch(s), 0)),
                pl.BlockSpec((L, ML_DQK), lambda b, s: (ch(s), 0))]

    def ospec(d):
        return pl.BlockSpec((1, L, BRANCH_W), lambda b, s: (b, _chunk_index(d, s, nctx, nch), 0))

    hshape = jax.ShapeDtypeStruct((B, T, BRANCH_W), F32)
    args = (p, p, p, p, grow, cos, sin)
    return pl.pallas_call(
        _mlstm_kernel,
        grid=(B, nch),
        in_specs=[pl.BlockSpec(memory_space=pltpu.SMEM)] + specs(0) + specs(1),
        out_specs=[ospec(0), ospec(1)],
        out_shape=[hshape, hshape],
        scratch_shapes=[pltpu.VMEM((2 * H, ML_DQK, ML_DV + 128), F32), pltpu.VMEM((2 * H, 1, 1), F32)],
        compiler_params=_cparams(("parallel", "arbitrary")),
        name="mlstm",
    )(gate_b, *args, *args)


def _mlstm_post_kernel(hf_ref, hb_ref, o_ref, g_ref, y_ref):
    h = hf_ref[0] + hb_ref[0]
    h = h * lax.rsqrt(jnp.mean(h * h, axis=-1, keepdims=True) + EPS) * g_ref[...]
    y_ref[0] = (h * _sigmoid(o_ref[0])).astype(y_ref.dtype)


def _mlstm_post(hf, hb, p, norm_g):
    B, T, W = hf.shape
    tm = _tile(T, 1024, 16)
    hspec = pl.BlockSpec((1, tm, ML_DV), lambda b, i, h: (b, i, h))
    return pl.pallas_call(
        _mlstm_post_kernel,
        grid=(B, T // tm, ML_HEADS),
        in_specs=[hspec, hspec,
                  pl.BlockSpec((1, tm, ML_DV), lambda b, i, h: (b, i, ML_O0 // ML_DV + h)),
                  pl.BlockSpec((1, ML_DV), lambda b, i, h: (0, h))],
        out_specs=hspec,
        out_shape=jax.ShapeDtypeStruct((B, T, W), BF16),
        compiler_params=_cparams(("parallel", "parallel", "parallel")),
        name="mlstm_post",
    )(hf, hb, p, norm_g.reshape(1, W))


def _rwkv_prep_kernel(x_ref, xp_ref, xn_ref, mu_ref, w0_ref, a0_ref, wdh_ref, wdl_ref, wa_ref, wg_ref,
                      r_ref, k_ref, v_ref, ld_ref, ar_ref, g_ref, *, tt, ctx_len, nt):
    i = pl.program_id(1)
    x = x_ref[0][:, :RW_COLS]
    first = (i == 0) | (i * tt == ctx_len)
    last = ((i + 1) * tt == ctx_len) | (i == nt - 1)
    prev_row = jnp.where(first, 0.0, xp_ref[0][7:8, :RW_COLS])
    next_row = jnp.where(last, 0.0, xn_ref[0][0:1, :RW_COLS])
    ridx = lax.broadcasted_iota(jnp.int32, (tt, 1), 0)
    prev = jnp.where(ridx == 0, prev_row, pltpu.roll(x, 1, 0))
    nxt = jnp.where(ridx == tt - 1, next_row, pltpu.roll(x, tt - 1, 0))
    xs = x + mu_ref[...] * (0.5 * (prev + nxt) - x)

    W = BRANCH_W
    r_ref[0] = xs[:, 0:W]
    k_ref[0] = xs[:, W:2 * W]
    v_ref[0] = xs[:, 2 * W:3 * W]
    wlo = _mm3(_split2(jnp.tanh(xs[:, 3 * W:3 * W + 256])), (wdh_ref[...], wdl_ref[...]))
    alo = _dot(xs[:, 3 * W + 128:3 * W + 384].astype(BF16), wa_ref[...])
    for d in range(2):
        ld_ref[d, 0] = -math.exp(-0.5) * _sigmoid(w0_ref[d:d + 1, :] + wlo[:, d * W:(d + 1) * W])
        ar_ref[d, 0] = _sigmoid(a0_ref[d:d + 1, :] + alo[:, d * W:(d + 1) * W])
    g_ref[0] = _dot(_sigmoid(xs[:, 3 * W + 384:3 * W + 640]).astype(BF16), wg_ref[...])


def _rwkv_prep(p, prm, ctx_len):
    B, T, _ = p.shape
    W = BRANCH_W
    tt = _tile(math.gcd(T, ctx_len), 256, 8)
    nt = T // tt
    h8 = tt // 8
    nb8 = T // 8
    z = lambda *s: jnp.zeros(s, F32)
    wd = jnp.concatenate([
        jnp.concatenate([prm['rw_w_up'][0], z(RW_LORA, W)], axis=1),
        jnp.concatenate([z(RW_LORA, W), prm['rw_w_up'][1]], axis=1),
        z(256 - 2 * RW_LORA, 2 * W)], axis=0)
    wa = jnp.concatenate([
        z(2 * RW_LORA - 128, 2 * W),
        jnp.concatenate([prm['rw_a_up'][0], z(RW_LORA, W)], axis=1),
        jnp.concatenate([z(RW_LORA, W), prm['rw_a_up'][1]], axis=1)], axis=0)
    wg = prm['rw_g_up'].astype(BF16)
    wdh, wdl = _split2(wd)
    wa = wa.astype(BF16)
    mu = prm['rw_mu'].reshape(1, RW_COLS)
    kern = functools.partial(_rwkv_prep_kernel, tt=tt, ctx_len=ctx_len, nt=nt)
    rwblk = RW_0 // RW_SPAN
    tok = jax.ShapeDtypeStruct((B, T, W), F32)
    tok2 = jax.ShapeDtypeStruct((2, B, T, W), F32)
    full = lambda shape: pl.BlockSpec(shape, lambda b, i: (0,) * len(shape))
    ospec = pl.BlockSpec((1, tt, W), lambda b, i: (b, i, 0))
    ospec2 = pl.BlockSpec((2, 1, tt, W), lambda b, i: (0, b, i, 0))
    return pl.pallas_call(
        kern,
        grid=(B, nt),
        in_specs=[pl.BlockSpec((1, tt, RW_SPAN), lambda b, i: (b, i, rwblk)),
                  pl.BlockSpec((1, 8, RW_SPAN), lambda b, i: (b, jnp.maximum(i * h8 - 1, 0), rwblk)),
                  pl.BlockSpec((1, 8, RW_SPAN), lambda b, i: (b, jnp.minimum((i + 1) * h8, nb8 - 1), rwblk)),
                  full((1, RW_COLS)), full((2, W)), full((2, W)),
                  full((256, 2 * W)), full((256, 2 * W)), full((256, 2 * W)), full((256, W))],
        out_specs=[ospec, ospec, ospec, ospec2, ospec2, ospec],
        out_shape=[tok, tok, tok, tok2, tok2, tok],
        compiler_params=_cparams(("parallel", "parallel")),
        name="rwkv_prep",
    )(p, p, p, mu, prm['rw_w0'], prm['rw_a0'], wdh, wdl, wa, wg)


def _split2(x):
    hi = x.astype(BF16)
    return hi, (x - hi.astype(F32)).astype(BF16)


def _split3(x):
    hi = x.astype(BF16)
    r1 = x - hi.astype(F32)
    mid = r1.astype(BF16)
    return hi, mid, (r1 - mid.astype(F32)).astype(BF16)


def _split1(x):
    return (x.astype(BF16),)


def _mm3(a, b, f=_dot):
    acc = f(a[0], b[0])
    if len(b) > 1:
        acc = acc + f(a[0], b[1])
    if len(a) > 1:
        acc = acc + f(a[1], b[0])
    return acc


def _mm_exact(e, parts, f=_dot):
    acc = f(e, parts[0])
    for x in parts[1:]:
        acc = acc + f(e, x)
    return acc


def _rwkv_scan_kernel(*refs, n_pairs):
    L = CHUNK
    P = 2 * RW_DH
    dir_refs = (refs[0:5], refs[5:10])
    kk_ref, ka_ref = refs[10:12]
    y_refs = refs[12:14]
    st_ref = refs[14]

    @pl.when(pl.program_id(1) == 0)
    def _():
        st_ref[...] = jnp.zeros_like(st_ref)

    row = lax.broadcasted_iota(jnp.int32, (L, P), 0)
    col = lax.broadcasted_iota(jnp.int32, (L, P), 1) % RW_DH
    before = (col < row, col > row)
    incl = (col <= row, col >= row)
    eye = (row == col).astype(F32)
    tri = tuple(m[:, :L].astype(BF16) for m in incl)
    lane = lax.broadcasted_iota(jnp.int32, (1, P), 1)
    m0 = lane < RW_DH
    r2 = lax.broadcasted_iota(jnp.int32, (P, P), 0) // RW_DH
    c2 = lax.broadcasted_iota(jnp.int32, (P, P), 1) // RW_DH
    same_head = r2 == c2

    def bd(x):
        z = jnp.zeros_like(x)
        return jnp.concatenate([jnp.where(m0, x, z), jnp.where(m0, z, x)], axis=0)

    def bd2(x):
        return tuple(bd(t) for t in x)

    sp = spi = _split1

    def cat(xs, axis):
        return tuple(jnp.concatenate(list(t), axis=axis) for t in zip(*xs))

    def st_load(s):
        sl = s['sl']
        r_ref, k_ref, v_ref, ld_ref, ar_ref = dir_refs[s['d']]
        s['R'], s['K'], s['V'] = r_ref[0, :, sl], k_ref[0, :, sl], v_ref[0, :, sl]
        s['LD'], s['AR'] = ld_ref[0, 0, :, sl], ar_ref[0, 0, :, sl]
        s['S0'] = st_ref[s['si']]
        kk = s['K'] * kk_ref[:, sl]
        kk2 = kk * kk
        n0 = jnp.sum(jnp.where(m0, kk2, 0.0), axis=1, keepdims=True)
        n1 = jnp.sum(jnp.where(m0, 0.0, kk2), axis=1, keepdims=True)
        s['kk'] = kk / jnp.maximum(jnp.sqrt(jnp.where(m0, n0, n1)), 1e-12)

    def st_cum(s):
        s['KD'] = s.pop('K') * (1.0 + (s['AR'] - 1.0) * ka_ref[:, s['sl']])
        s['BV'] = s['kk'] * s.pop('AR')
        s['c'] = _mm_exact(tri[s['d']], _split3(s['LD']))

    def st_x(s):
        c, LD = s.pop('c'), s.pop('LD')
        c_end = jnp.sum(LD, axis=0, keepdims=True)
        s['dcol'] = jnp.exp(jnp.broadcast_to(c_end, (P, P)).T)
        At = -s.pop('kk') * jnp.exp(c - LD)
        e_neg = jnp.exp(-c)
        BV, KD = s.pop('BV'), s.pop('KD')
        s['Rt'] = s.pop('R') * jnp.exp(c)
        e_end = jnp.exp(c_end - c)
        s['Bh_s'], s['Kh_s'] = sp(BV * e_end), sp(KD * e_end)
        s['At_s'], s['V_s'] = sp(At), sp(s.pop('V'))
        s['X'] = _mm3(cat([s['At_s'], sp(s['Rt'])], 0),
                      cat([bd2(sp(BV * e_neg)), bd2(sp(KD * e_neg))], 0), _dot_nt)

    def st_inv0(s):
        X = s.pop('X')
        bef, inc = before[s['d']], incl[s['d']]
        Nab = jnp.where(bef, X[:L, :P], 0.0)
        s['Nak_s'] = sp(jnp.where(bef, X[:L, P:], 0.0))
        s['Mrb_s'] = sp(jnp.where(inc, X[L:, :P], 0.0))
        s['Mrk_s'] = sp(jnp.where(inc, X[L:, P:], 0.0))
        s['T'] = eye + Nab
        Np_s = spi(Nab)
        s['Z'] = _mm3(Np_s, bd2(Np_s))
        s['NV'] = _mm3(s.pop('Nak_s'), bd2(s['V_s']))

    def st_inv1(s):
        Z = s.pop('Z')
        if Z.shape[0] == 2 * L:
            s['T'] = s['T'] + Z[L:]
        Np_s = spi(Z[:L])
        s['Z'] = _mm3(cat([Np_s, spi(s['T'])], 0), bd2(Np_s))

    def st_inv2(s):
        Z = s.pop('Z')
        s['T'] = s['T'] + Z[L:]
        s['Z'] = _mm3(spi(s['T']), bd2(spi(Z[:L])))

    def st_wu(s):
        T_s = sp(s.pop('T') + s.pop('Z'))
        s['WU'] = _mm3(T_s, cat([bd2(s.pop('At_s')), bd2(sp(s.pop('NV')))], 1))

    def st_loc(s):
        WU = s.pop('WU')
        W_s, Ul_s = sp(WU[:, :P]), sp(WU[:, P:])
        MM = _mm3(s.pop('Mrb_s'), cat([bd2(W_s), bd2(Ul_s)], 1))
        s['Q_s'] = sp(s.pop('Rt') + MM[:, :P])
        s['Yl'] = MM[:, P:] + _mm3(s.pop('Mrk_s'), bd2(s['V_s']))
        PS = _mm3(s.pop('Bh_s'), cat([W_s, Ul_s], 1), _dot_tn)
        s['Pm_s'] = sp(jnp.where(same_head, PS[:, :P], 0.0))
        s['Sl'] = jnp.where(same_head, PS[:, P:] + _mm3(s.pop('Kh_s'), s.pop('V_s'), _dot_tn), 0.0)

    def st_out(s):
        S0 = s.pop('S0')
        S0_s = sp(S0)
        y_refs[s['d']][0, :, s['sl']] = _mm3(s.pop('Q_s'), S0_s) + s.pop('Yl')
        st_ref[s['si']] = s.pop('dcol') * S0 + _mm3(s.pop('Pm_s'), S0_s) + s.pop('Sl')

    n_dbl = int(math.log2(L)) - 2
    stages = [st_load, st_cum, st_x, st_inv0] + [st_inv1] * n_dbl + [st_inv2, st_wu, st_loc, st_out]
    states = [{'d': d, 'si': d * n_pairs + j, 'sl': slice(P * j, P * (j + 1))}
              for d in range(2) for j in range(n_pairs)]
    for stage in stages:
        for s in states:
            stage(s)


def _rwkv_scan(r, k, v, ld, ar, k_k, k_a, ctx_len):
    B, T, W = r.shape
    L = CHUNK
    nch = T // L
    nctx = ctx_len // L
    n_pairs = W // (2 * RW_DH)

    def specs(d):
        tok = pl.BlockSpec((1, L, W), lambda b, s: (b, _chunk_index(d, s, nctx, nch), 0))
        per_dir = pl.BlockSpec((1, 1, L, W), lambda b, s: (d, b, _chunk_index(d, s, nctx, nch), 0))
        return [tok, tok, tok, per_dir, per_dir]

    def ospec(d):
        return pl.BlockSpec((1, L, W), lambda b, s: (b, _chunk_index(d, s, nctx, nch), 0))

    pspec = pl.BlockSpec((1, W), lambda b, s: (0, 0))
    yshape = jax.ShapeDtypeStruct((B, T, W), F32)
    kern = functools.partial(_rwkv_scan_kernel, n_pairs=n_pairs)
    args = (r, k, v, ld, ar)
    return pl.pallas_call(
        kern,
        grid=(B, nch),
        in_specs=specs(0) + specs(1) + [pspec, pspec],
        out_specs=[ospec(0), ospec(1)],
        out_shape=[yshape, yshape],
        scratch_shapes=[pltpu.VMEM((2 * n_pairs, 2 * RW_DH, 2 * RW_DH), F32)],
        compiler_params=_cparams(("parallel", "arbitrary")),
        name="rwkv_scan",
    )(*args, *args, k_k.reshape(1, W), k_a.reshape(1, W))


def _rwkv_post_kernel(yf_ref, yb_ref, r_ref, k_ref, v_ref, g_ref, rk_ref, lw_ref, lb_ref, o_ref):
    P = 2 * RW_DH
    r2 = lax.broadcasted_iota(jnp.int32, (P, P), 0) // RW_DH
    c2 = lax.broadcasted_iota(jnp.int32, (P, P), 1) // RW_DH
    ones_bd = (r2 == c2).astype(BF16)

    def head_sum(x):
        return _mm_exact(ones_bd, _split2(x), lambda e, t: _dot(t, e))

    for j in range(o_ref.shape[2] // P):
        sl = slice(P * j, P * (j + 1))
        y = yf_ref[0, :, sl] + yb_ref[0, :, sl]
        yc = y - head_sum(y) * (1.0 / RW_DH)
        var = head_sum(yc * yc) * (1.0 / RW_DH)
        yn = yc * lax.rsqrt(var + RW_GN_EPS)
        bonus = head_sum(r_ref[0, :, sl] * k_ref[0, :, sl] * rk_ref[:, sl]) * v_ref[0, :, sl]
        o_ref[0, :, sl] = ((yn * lw_ref[:, sl] + lb_ref[:, sl] + bonus) * g_ref[0, :, sl]).astype(o_ref.dtype)


def _rwkv_post(y_f, y_b, r, k, v, g, r_k, ln_w, ln_b):
    B, T, W = y_f.shape
    tt = _tile(T, 512, 16)
    tspec = pl.BlockSpec((1, tt, W), lambda b, i: (b, i, 0))
    pspec = pl.BlockSpec((1, W), lambda b, i: (0, 0))
    return pl.pallas_call(
        _rwkv_post_kernel,
        grid=(B, T // tt),
        in_specs=[tspec, tspec, tspec, tspec, tspec, tspec, pspec, pspec, pspec],
        out_specs=tspec,
        out_shape=jax.ShapeDtypeStruct((B, T, W), BF16),
        compiler_params=_cparams(("parallel", "parallel")),
        name="rwkv_post",
    )(y_f, y_b, r, k, v, g, r_k.reshape(1, W), ln_w.reshape(1, W), ln_b.reshape(1, W))


def _rwkv_branch(p, prm, ctx_len):
    r, k, v, ld, ar, g = _rwkv_prep(p, prm, ctx_len)
    y_f, y_b = _rwkv_scan(r, k, v, ld, ar, prm['rw_k_k'], prm['rw_k_a'], ctx_len)
    return _rwkv_post(y_f, y_b, r, k, v, g, prm['rw_r_k'], prm['rw_ln_w'], prm['rw_ln_b'])


NA_RB = 4
NA_SLAB = NA_RB + NA_WIN_R - 1
NA_HB = 4


def _na_kernel(q_ref, k_ref, v_ref, bias_ref, o_ref, *, ctx_len, rows):
    rb = pl.program_id(2)
    scale = NA_DH ** -0.5
    heads = [slice(i * NA_DH, (i + 1) * NA_DH) for i in range(NA_HB)]

    @pl.when(rb == 0)
    def _():
        for hs in heads:
            s_ctx = _dot_nt(q_ref[0, :, hs].astype(BF16), k_ref[0, 0:ctx_len, hs].astype(BF16)) * scale
            p_ctx = jnp.exp(s_ctx - jnp.max(s_ctx, axis=-1, keepdims=True))
            y = _dot(p_ctx.astype(BF16), v_ref[0, 0:ctx_len, hs].astype(BF16))
            o_ref[0, :, hs] = (y / jnp.sum(p_ctx, axis=-1, keepdims=True)).astype(o_ref.dtype)

    @pl.when(rb > 0)
    def _():
        nslab = NA_SLAB * GRID_W
        start = jnp.clip((rb - 1) * NA_RB - NA_WIN_R // 2, 0, rows - NA_SLAB)
        koff = pl.multiple_of(ctx_len + start * GRID_W, GRID_W)

        def st_scores(s):
            hs = s['hs']
            q = q_ref[0, :, hs].astype(BF16)
            s['s_win'] = _dot_nt(q, k_ref[0, pl.ds(koff, nslab), hs].astype(BF16)) * scale + bias_ref[0, s['i'], 0]
            s['s_ctx'] = _dot_nt(q, k_ref[0, 0:ctx_len, hs].astype(BF16)) * scale

        def st_softmax(s):
            s_win, s_ctx = s.pop('s_win'), s.pop('s_ctx')
            m = jnp.maximum(jnp.max(s_win, axis=-1, keepdims=True), jnp.max(s_ctx, axis=-1, keepdims=True))
            p_win = jnp.exp(s_win - m)
            p_ctx = jnp.exp(s_ctx - m)
            s['den'] = jnp.sum(p_win, axis=-1, keepdims=True) + jnp.sum(p_ctx, axis=-1, keepdims=True)
            hs = s['hs']
            s['y'] = (_dot(p_win.astype(BF16), v_ref[0, pl.ds(koff, nslab), hs].astype(BF16))
                      + _dot(p_ctx.astype(BF16), v_ref[0, 0:ctx_len, hs].astype(BF16)))

        def st_out(s):
            o_ref[0, :, s['hs']] = (s.pop('y') / s.pop('den')).astype(o_ref.dtype)

        states = [{'i': i, 'hs': hs} for i, hs in enumerate(heads)]
        for stage in (st_scores, st_softmax, st_out):
            for s in states:
                stage(s)


def _na_block_type(lb, nblk):
    return jnp.where(lb <= 0, 0, jnp.where(lb == nblk - 1, 2, 1))


def _na_bias_kernel(toe_ref, o_ref, *, ro, row_ok):
    neg = jnp.full((GRID_W, GRID_W), NEG, F32)
    for t in range(ro.shape[0]):
        for ri in range(NA_RB):
            for kj in range(NA_SLAB):
                tile = toe_ref[0, 0, int(ro[t, ri, kj])] if row_ok[t, ri, kj] else neg
                o_ref[0, 0, t, ri * GRID_W:(ri + 1) * GRID_W, kj * GRID_W:(kj + 1) * GRID_W] = tile


def _na_bias_table(rpb, rows):
    depth, H = rpb.shape[:2]
    c = np.arange(GRID_W)
    c_start = np.clip(c - NA_WIN_C // 2, 0, GRID_W - NA_WIN_C)
    col_ok = (c[None, :] >= c_start[:, None]) & (c[None, :] < c_start[:, None] + NA_WIN_C)
    col_off = np.clip(c[None, :] - c[:, None], 1 - NA_WIN_C, NA_WIN_C - 1) + (NA_WIN_C - 1)
    sel_c = jnp.asarray(col_off[..., None] == np.arange(2 * NA_WIN_C - 1), F32)
    toe = jnp.einsum('lhoc,qkc->lhoqk', rpb, sel_c, precision=HI) + jnp.asarray(np.where(col_ok, 0.0, NEG), F32)
    nblk = rows // NA_RB
    reps = np.array([0, min(1, nblk - 1), nblk - 1])
    r = reps[:, None] * NA_RB + np.arange(NA_RB)[None, :]
    slab0 = np.clip(reps * NA_RB - NA_WIN_R // 2, 0, rows - NA_SLAB)
    kr = slab0[:, None] + np.arange(NA_SLAB)[None, :]
    win0 = np.clip(r - NA_WIN_R // 2, 0, rows - NA_WIN_R)
    ro = kr[:, None, :] - r[:, :, None] + (NA_WIN_R - 1)
    row_ok = (kr[:, None, :] >= win0[:, :, None]) & (kr[:, None, :] < win0[:, :, None] + NA_WIN_R)
    kern = functools.partial(_na_bias_kernel, ro=ro, row_ok=row_ok)
    nq, nk = NA_RB * GRID_W, NA_SLAB * GRID_W
    return pl.pallas_call(
        kern,
        grid=(depth, H),
        in_specs=[pl.BlockSpec((1, 1, 2 * NA_WIN_R - 1, GRID_W, GRID_W), lambda l, h: (l, h, 0, 0, 0))],
        out_specs=pl.BlockSpec((1, 1, 3, nq, nk), lambda l, h: (l, h, 0, 0, 0)),
        out_shape=jax.ShapeDtypeStruct((depth, H, 3, nq, nk), F32),
        compiler_params=_cparams(("parallel", "parallel")),
        name="na_bias",
    )(toe)


def _na_branch(p, table, l, ctx_len):
    B, T, _ = p.shape
    seq = T - ctx_len
    rows = seq // GRID_W
    H = NA_HEADS
    nq = NA_RB * GRID_W
    assert ctx_len == nq and rows % NA_RB == 0 and rows >= NA_SLAB, (ctx_len, rows)
    nblk = rows // NA_RB
    wb = NA_HB * NA_DH
    qb, kb, vb = NA_0 // wb, (NA_0 + BRANCH_W) // wb, (NA_0 + 2 * BRANCH_W) // wb
    kern = functools.partial(_na_kernel, ctx_len=ctx_len, rows=rows)
    return pl.pallas_call(
        kern,
        grid=(B, H // NA_HB, 1 + nblk),
        in_specs=[pl.BlockSpec((1, nq, wb), lambda b, h, r: (b, r, qb + h)),
                  pl.BlockSpec((1, T, wb), lambda b, h, r: (b, 0, kb + h)),
                  pl.BlockSpec((1, T, wb), lambda b, h, r: (b, 0, vb + h)),
                  pl.BlockSpec((1, NA_HB, 1, nq, NA_SLAB * GRID_W),
                               lambda b, h, r: (l, h, _na_block_type(r - 1, nblk), 0, 0))],
        out_specs=pl.BlockSpec((1, nq, wb), lambda b, h, r: (b, r, h)),
        out_shape=jax.ShapeDtypeStruct((B, T, BRANCH_W), BF16),
        compiler_params=_cparams(("parallel", "parallel", "arbitrary")),
        name="na",
    )(p, p, p, table)


def _pack_kernel(w_ref, o_ref):
    ml, rw, na, gt = 3088, 3712, 3072, 6144
    o_rw, o_na, o_gt = ml, ml + rw, ml + rw + na
    w = w_ref[0]
    n = w.shape[0]
    o_ref[0, :, 0:NA_0] = w[:, 0:NA_0].astype(BF16)
    o_ref[0, :, NA_0:GATE_0] = w[:, o_na:o_na + na].astype(BF16)
    o_ref[0, :, GATE_0:RW_0] = w[:, o_gt:o_gt + gt].astype(BF16)
    o_ref[0, :, RW_0:MLG_0] = w[:, o_rw:o_rw + rw].astype(BF16)
    tail = jnp.concatenate([w[:, NA_0:ml], jnp.zeros((n, 128 - (ml - NA_0)), F32)], axis=1)
    o_ref[0, :, MLG_0:MLG_0 + 128] = tail.astype(BF16)
    o_ref[0, :, MLG_0 + 128:] = jnp.zeros((n, P_COLS - MLG_0 - 128), BF16)


def _pack_w_in(w_in, l):
    _, D, N = w_in.shape
    tr = _tile(D, 64, 16)
    return pl.pallas_call(
        _pack_kernel,
        grid=(D // tr,),
        in_specs=[pl.BlockSpec((1, tr, N), lambda i: (l, i, 0))],
        out_specs=pl.BlockSpec((1, tr, P_COLS), lambda i: (0, i, 0)),
        out_shape=jax.ShapeDtypeStruct((1, D, P_COLS), BF16),
        compiler_params=_cparams(("parallel",)),
        name="pack_w_in",
    )(w_in)


def _rope_tables(seq, ctx_len):
    t = jnp.arange(seq)
    nf = ML_DQK // 4
    freqs = ROPE_BASE ** (-jnp.arange(nf, dtype=F32) / nf)
    ang_r = (t // GRID_W).astype(F32)[:, None] * freqs[None, :]
    ang_c = (t % GRID_W).astype(F32)[:, None] * freqs[None, :]
    cos = jnp.concatenate([jnp.cos(ang_r)] * 2 + [jnp.cos(ang_c)] * 2, axis=1)
    sin = jnp.concatenate([-jnp.sin(ang_r), jnp.sin(ang_r), -jnp.sin(ang_c), jnp.sin(ang_c)], axis=1)
    cos = jnp.concatenate([jnp.ones((ctx_len, ML_DQK), F32), cos], axis=0)
    sin = jnp.concatenate([jnp.zeros((ctx_len, ML_DQK), F32), sin], axis=0)
    return cos, sin


def kernel(x, c, ctx, c_ctx, w_mod, b_mod, norm1_g, w_in, ml_gate_b, ml_norm_g, rw_mu, rw_w0, rw_w_up, rw_a0, rw_a_up, rw_g_up, rw_k_k, rw_k_a, rw_r_k, rw_ln_w, rw_ln_b, na_rpb, w_branch, w_out, norm2_g, w_ff1, w_ff2, final_g):
    B, seq, D = x.shape
    ctx_len = ctx.shape[1]
    depth = w_mod.shape[0]
    xs = jnp.concatenate([ctx, x], axis=1)
    s_all = jax.nn.silu(jnp.concatenate([c_ctx[None, :], c], axis=0))
    s_all = jnp.concatenate([s_all, jnp.zeros((8 - (B + 1) % 8, D), F32)], axis=0) if (B + 1) % 8 else s_all
    cos, sin = _rope_tables(seq, ctx_len)
    na_table = _na_bias_table(na_rpb, seq // GRID_W)
    wb_bf, wo_bf, w2_bf = (w.astype(BF16) for w in (w_branch, w_out, w_ff2))
    for l in range(depth):
        mod = _modulation(s_all, w_mod, b_mod, l).reshape(-1, 6, D)
        mc = jnp.broadcast_to(mod[0:1], (B, 6, D))
        ml_ = mod[1:B + 1]
        mod1 = jnp.stack([mc[:, 0], mc[:, 1], ml_[:, 0], ml_[:, 1]], axis=1)
        gate1 = jnp.stack([mc[:, 2], ml_[:, 2]], axis=1)
        mod2 = jnp.stack([mc[:, 3], mc[:, 4], ml_[:, 3], ml_[:, 4]], axis=1)
        gate2 = jnp.stack([mc[:, 5], ml_[:, 5]], axis=1)
        prm = {'rw_mu': rw_mu[l], 'rw_w0': rw_w0[l], 'rw_w_up': rw_w_up[l], 'rw_a0': rw_a0[l],
               'rw_a_up': rw_a_up[l], 'rw_g_up': rw_g_up[l], 'rw_k_k': rw_k_k[l], 'rw_k_a': rw_k_a[l],
               'rw_r_k': rw_r_k[l], 'rw_ln_w': rw_ln_w[l], 'rw_ln_b': rw_ln_b[l]}

        p = _mm(_norm_mod(xs, norm1_g[l], mod1, ctx_len), _pack_w_in(w_in, l), 0, F32, False)
        h_f, h_b = _mlstm(p, ml_gate_b[l], cos, sin, ctx_len)
        y_m = _mlstm_post(h_f, h_b, p, ml_norm_g[l])
        y_r = _rwkv_branch(p, prm, ctx_len)
        y_n = _na_branch(p, na_table, l, ctx_len)
        merged = _merge(y_m, y_r, y_n, p, wb_bf, l)
        xs, h2 = _mm_res_norm(merged, wo_bf, l, xs, gate1, norm2_g[l], mod2, ctx_len)
        hff = _mm(h2, w_ff1, l, BF16, True)
        xs = _mm_res(hff, w2_bf, l, xs, gate2, ctx_len)
    return _final_norm(xs, final_g, ctx_len)
```

```python
import functools
import math

import jax
import jax.numpy as jnp
import numpy as np
from jax import lax
from jax.experimental import pallas as pl
from jax.experimental.pallas import tpu as pltpu

F32 = jnp.float32
BF16 = jnp.bfloat16
HI = lax.Precision.HIGHEST

D_MODEL = 2048
GRID_W = 64
BRANCH_W = D_MODEL // 2
N_BRANCH = 3
ML_HEADS = 4
ML_DV = BRANCH_W // ML_HEADS
ML_DQK = ML_DV // 2
RW_DH = 64
RW_HEADS = BRANCH_W // RW_DH
RW_LORA = 96
RW_GATE_LORA = 256
RW_GN_EPS = 64e-5
NA_DH = 128
NA_HEADS = BRANCH_W // NA_DH
NA_WIN_R = 8
NA_WIN_C = 16
D_FF = 4 * D_MODEL
ROPE_BASE = 10000.0
EPS = 1e-6
CHUNK = 64
NEG = -1e30

ML_Q0, ML_K0, ML_V0, ML_O0 = 0, 512, 1024, 2048
NA_0 = 3072
GATE_0 = 6144
RW_0 = 12288
RW_SPAN = 4096
RW_COLS = 3712
MLG_0 = RW_0 + RW_COLS
P_COLS = RW_0 + RW_SPAN

VMEM_LIMIT = 56 * 1024 * 1024
MM_ROWS = 1100


def _cparams(sem):
    return pltpu.CompilerParams(dimension_semantics=sem, vmem_limit_bytes=VMEM_LIMIT)


def _tile(n, cap, mult):
    best = None
    for t in range(mult, min(n, cap) + 1, mult):
        if n % t == 0:
            best = t
    assert best is not None, (n, cap, mult)
    return best


def _dot(a, b, prec=None):
    return jnp.dot(a, b, preferred_element_type=F32, precision=prec)


def _dot_nt(a, b, prec=None):
    return lax.dot_general(a, b, (((1,), (1,)), ((), ())), preferred_element_type=F32, precision=prec)


def _dot_tn(a, b, prec=None):
    return lax.dot_general(a, b, (((0,), (0,)), ((), ())), preferred_element_type=F32, precision=prec)


def _sigmoid(x):
    return 1.0 / (1.0 + jnp.exp(-x))


def _softplus(x):
    return jnp.maximum(x, 0.0) + jnp.log(1.0 + jnp.exp(-jnp.abs(x)))


def _mod_kernel(s_ref, w_ref, b_ref, o_ref):
    o_ref[...] = _dot(s_ref[...], w_ref[0], HI) + b_ref[0]


def _modulation(s, w, b, l):
    R, D = s.shape
    N = w.shape[2]
    tn = _tile(N, 1024, 128)
    return pl.pallas_call(
        _mod_kernel,
        grid=(N // tn,),
        in_specs=[pl.BlockSpec((R, D), lambda j: (0, 0)),
                  pl.BlockSpec((1, D, tn), lambda j: (l, 0, j)),
                  pl.BlockSpec((1, 1, tn), lambda j: (l, 0, j))],
        out_specs=pl.BlockSpec((R, tn), lambda j: (0, j)),
        out_shape=jax.ShapeDtypeStruct((R, N), F32),
        compiler_params=_cparams(("arbitrary",)),
        name="modulation",
    )(s, w, b.reshape(b.shape[0], 1, N))


def _norm_mod_kernel(x_ref, g_ref, mod_ref, o_ref, *, tm, ctx_len):
    x = x_ref[0]
    y = x * lax.rsqrt(jnp.mean(x * x, axis=-1, keepdims=True) + EPS) * g_ref[...]
    row = pl.program_id(1) * tm + lax.broadcasted_iota(jnp.int32, (tm, 1), 0)
    is_ctx = row < ctx_len
    m = mod_ref[0]
    shift = jnp.where(is_ctx, m[0:1], m[2:3])
    scale = jnp.where(is_ctx, m[1:2], m[3:4])
    o_ref[0] = (y * (1.0 + scale) + shift).astype(o_ref.dtype)


def _norm_mod(x, g, mod4, ctx_len):
    B, T, D = x.shape
    tm = _tile(T, 320, 16)
    kern = functools.partial(_norm_mod_kernel, tm=tm, ctx_len=ctx_len)
    return pl.pallas_call(
        kern,
        grid=(B, T // tm),
        in_specs=[pl.BlockSpec((1, tm, D), lambda b, i: (b, i, 0)),
                  pl.BlockSpec((1, D), lambda b, i: (0, 0)),
                  pl.BlockSpec((1, 4, D), lambda b, i: (b, 0, 0))],
        out_specs=pl.BlockSpec((1, tm, D), lambda b, i: (b, i, 0)),
        out_shape=jax.ShapeDtypeStruct((B, T, D), BF16),
        compiler_params=_cparams(("parallel", "parallel")),
        name="norm_mod",
    )(x, g.reshape(1, D), mod4)


def _mm_kernel(h_ref, w_ref, o_ref, *, relu2):
    acc = _dot(h_ref[0], w_ref[0].astype(BF16))
    if relu2:
        acc = jnp.square(jnp.maximum(acc, 0.0))
    o_ref[0] = acc.astype(o_ref.dtype)


def _mm(h, w, l, out_dtype, relu2):
    B, T, D = h.shape
    N = w.shape[2]
    tm = _tile(T, 2 * MM_ROWS, 16)
    tn = _tile(N, 512, 128)
    kern = functools.partial(_mm_kernel, relu2=relu2)
    return pl.pallas_call(
        kern,
        grid=(B, T // tm, N // tn),
        in_specs=[pl.BlockSpec((1, tm, D), lambda b, i, j: (b, i, 0)),
                  pl.BlockSpec((1, D, tn), lambda b, i, j: (l, 0, j))],
        out_specs=pl.BlockSpec((1, tm, tn), lambda b, i, j: (b, i, j)),
        out_shape=jax.ShapeDtypeStruct((B, T, N), out_dtype),
        compiler_params=_cparams(("parallel", "parallel", "arbitrary")),
        name="mm",
    )(h, w)


def _row_is_ctx(tm, ctx_len):
    row = pl.program_id(1) * tm + lax.broadcasted_iota(jnp.int32, (tm, 1), 0)
    return row < ctx_len


def _mm_res_kernel(a_ref, w_ref, x_ref, gate_ref, o_ref, acc_ref, *, tm, ctx_len, nk):
    k = pl.program_id(3)

    @pl.when(k == 0)
    def _():
        acc_ref[...] = jnp.zeros_like(acc_ref)

    acc_ref[...] += _dot(a_ref[0], w_ref[0])

    @pl.when(k == nk - 1)
    def _():
        gt = gate_ref[0]
        gate = jnp.where(_row_is_ctx(tm, ctx_len), gt[0:1], gt[1:2])
        o_ref[0] = x_ref[0] + gate * acc_ref[...]


def _mm_res(a, w, l, x, gate2, ctx_len):
    B, T, K = a.shape
    D = w.shape[2]
    tm = _tile(T, MM_ROWS, 16)
    tn = _tile(D, 1024, 128)
    tk = _tile(K, 2048, 128)
    nk = K // tk
    kern = functools.partial(_mm_res_kernel, tm=tm, ctx_len=ctx_len, nk=nk)
    return pl.pallas_call(
        kern,
        grid=(B, T // tm, D // tn, nk),
        in_specs=[pl.BlockSpec((1, tm, tk), lambda b, i, j, k: (b, i, k)),
                  pl.BlockSpec((1, tk, tn), lambda b, i, j, k: (l, k, j)),
                  pl.BlockSpec((1, tm, tn), lambda b, i, j, k: (b, i, j)),
                  pl.BlockSpec((1, 2, tn), lambda b, i, j, k: (b, 0, j))],
        out_specs=pl.BlockSpec((1, tm, tn), lambda b, i, j, k: (b, i, j)),
        out_shape=jax.ShapeDtypeStruct((B, T, D), F32),
        scratch_shapes=[pltpu.VMEM((tm, tn), F32)],
        compiler_params=_cparams(("parallel", "parallel", "parallel", "arbitrary")),
        name="mm_res",
    )(a, w, x, gate2)


def _mm_res_norm_kernel(a_ref, w_ref, x_ref, gate_ref, g_ref, mod_ref, o_ref, h_ref, *, tm, ctx_len):
    is_ctx = _row_is_ctx(tm, ctx_len)
    gt = gate_ref[0]
    xs = x_ref[0] + jnp.where(is_ctx, gt[0:1], gt[1:2]) * _dot(a_ref[0], w_ref[0])
    o_ref[0] = xs
    y = xs * lax.rsqrt(jnp.mean(xs * xs, axis=-1, keepdims=True) + EPS) * g_ref[...]
    m = mod_ref[0]
    shift = jnp.where(is_ctx, m[0:1], m[2:3])
    scale = jnp.where(is_ctx, m[1:2], m[3:4])
    h_ref[0] = (y * (1.0 + scale) + shift).astype(h_ref.dtype)


def _mm_res_norm(a, w, l, x, gate2, g, mod4, ctx_len):
    B, T, K = a.shape
    D = w.shape[2]
    tm = _tile(T, MM_ROWS // 4, 16)
    kern = functools.partial(_mm_res_norm_kernel, tm=tm, ctx_len=ctx_len)
    row = lambda width: pl.BlockSpec((1, tm, width), lambda b, i: (b, i, 0))
    return pl.pallas_call(
        kern,
        grid=(B, T // tm),
        in_specs=[row(K),
                  pl.BlockSpec((1, K, D), lambda b, i: (l, 0, 0)),
                  row(D),
                  pl.BlockSpec((1, 2, D), lambda b, i: (b, 0, 0)),
                  pl.BlockSpec((1, D), lambda b, i: (0, 0)),
                  pl.BlockSpec((1, 4, D), lambda b, i: (b, 0, 0))],
        out_specs=[row(D), row(D)],
        out_shape=[jax.ShapeDtypeStruct((B, T, D), F32), jax.ShapeDtypeStruct((B, T, D), BF16)],
        compiler_params=_cparams(("parallel", "parallel")),
        name="mm_res_norm",
    )(a, w, x, gate2, g.reshape(1, D), mod4)


def _merge_kernel(ym_ref, yr_ref, yn_ref, g0_ref, g1_ref, g2_ref, wb_ref, o_ref):
    acc = _sigmoid(g0_ref[0]) * _dot(ym_ref[0], wb_ref[0, 0])
    acc += _sigmoid(g1_ref[0]) * _dot(yr_ref[0], wb_ref[0, 1])
    acc += _sigmoid(g2_ref[0]) * _dot(yn_ref[0], wb_ref[0, 2])
    o_ref[0] = acc.astype(o_ref.dtype)


def _merge(ym, yr, yn, p, wb, l):
    B, T, W = ym.shape
    D = wb.shape[3]
    tm = _tile(T, MM_ROWS, 16)
    tn = _tile(D, 512, 128)
    yspec = pl.BlockSpec((1, tm, W), lambda b, i, j: (b, i, 0))

    def gspec(n):
        base = (GATE_0 + n * D) // tn
        return pl.BlockSpec((1, tm, tn), lambda b, i, j: (b, i, base + j))

    return pl.pallas_call(
        _merge_kernel,
        grid=(B, T // tm, D // tn),
        in_specs=[yspec, yspec, yspec, gspec(0), gspec(1), gspec(2),
                  pl.BlockSpec((1, N_BRANCH, W, tn), lambda b, i, j: (l, 0, 0, j))],
        out_specs=pl.BlockSpec((1, tm, tn), lambda b, i, j: (b, i, j)),
        out_shape=jax.ShapeDtypeStruct((B, T, D), BF16),
        compiler_params=_cparams(("parallel", "parallel", "arbitrary")),
        name="merge",
    )(ym, yr, yn, p, p, p, wb)


def _final_norm_kernel(x_ref, g_ref, o_ref):
    x = x_ref[0]
    o_ref[0] = x * lax.rsqrt(jnp.mean(x * x, axis=-1, keepdims=True) + EPS) * g_ref[...]


def _final_norm(xs, g, ctx_len):
    B, T, D = xs.shape
    seq = T - ctx_len
    tm = _tile(math.gcd(seq, ctx_len), 256, 8)
    off = ctx_len // tm
    return pl.pallas_call(
        _final_norm_kernel,
        grid=(B, seq // tm),
        in_specs=[pl.BlockSpec((1, tm, D), lambda b, i: (b, off + i, 0)),
                  pl.BlockSpec((1, D), lambda b, i: (0, 0))],
        out_specs=pl.BlockSpec((1, tm, D), lambda b, i: (b, i, 0)),
        out_shape=jax.ShapeDtypeStruct((B, seq, D), F32),
        compiler_params=_cparams(("parallel", "parallel")),
        name="final_norm",
    )(xs, g.reshape(1, D))


def _chunk_index(d, s, n_ctx, n_all):
    back = jnp.where(s < n_ctx, n_ctx - 1 - s, n_all + n_ctx - 1 - s)
    return jnp.where(d == 0, s, back)


def _rope(x, cos, sin_signed, first_half):
    swapped = jnp.where(first_half, pltpu.roll(x, 96, 1), pltpu.roll(x, 32, 1))
    return x * cos + swapped * sin_signed


def _log_sigmoid(x):
    return jnp.minimum(x, 0.0) - jnp.log(1.0 + jnp.exp(-jnp.abs(x)))


def _mlstm_kernel(bias_ref, *refs):
    L = CHUNK
    n_in = 7
    dir_refs = (refs[:n_in], refs[n_in:2 * n_in])
    out_refs = refs[2 * n_in:2 * n_in + 2]
    ct_ref, m_ref = refs[2 * n_in + 2:]

    @pl.when(pl.program_id(0) == 0)
    def _():
        ct_ref[...] = jnp.zeros_like(ct_ref)
        m_ref[...] = jnp.zeros_like(m_ref)

    row = lax.broadcasted_iota(jnp.int32, (L, L), 0)
    col = lax.broadcasted_iota(jnp.int32, (L, L), 1)
    lane = lax.broadcasted_iota(jnp.int32, (L, ML_DQK), 1)
    first_half = (lane % 64) < 32
    ones = jnp.ones((L, 128), BF16)

    def st_qk(s):
        d, h = s['d'], s['h']
        q_ref, k_ref, v_ref, _, _, cos_ref, sin_ref = dir_refs[d]
        cos, sin = cos_ref[...], sin_ref[...]
        hq = slice(h * ML_DQK, (h + 1) * ML_DQK)
        b = s['b']
        s['q'] = _rope(q_ref[b, :, hq] * (ML_DQK ** -0.5), cos, sin, first_half).astype(BF16)
        s['k'] = _rope(k_ref[b, :, hq], cos, sin, first_half).astype(BF16)
        s['v'] = v_ref[b, :, h * ML_DV:(h + 1) * ML_DV]
        s['qk'] = _dot_nt(s['q'], s['k'])

    def st_gate(s):
        d, h = s['d'], s['h']
        gc_ref, gr_ref = dir_refs[d][3:5]
        b_i = bias_ref[8 * d + h]
        b_f = bias_ref[8 * d + 4 + h]
        ji, jf = 8 * d + h, 8 * d + 4 + h
        gc = gc_ref[s['b']]
        gr = gr_ref[s['b'], 0]
        s['ig_c'] = gc[:, ji:ji + 1] + b_i
        lf_c = _log_sigmoid(gc[:, jf:jf + 1] + b_f)
        s['ig_r'] = gr[ji:ji + 1, :] + b_i
        lf_r = _log_sigmoid(gr[jf:jf + 1, :] + b_f)
        s['incl'] = (col <= row) if d == 0 else (col >= row)
        incl_t = (row <= col) if d == 0 else (row >= col)
        s['b_c'] = jnp.sum(jnp.where(s['incl'], lf_r, 0.0), axis=1, keepdims=True)
        s['b_r'] = jnp.sum(jnp.where(incl_t, lf_c, 0.0), axis=0, keepdims=True)
        s['b_end'] = jnp.sum(lf_r, axis=1, keepdims=True)

    def st_max(s):
        b_c, b_r, b_end, ig_r = s['b_c'], s.pop('b_r'), s['b_end'], s.pop('ig_r')
        s['a_c'] = b_end - b_c + s.pop('ig_c')
        s['a_max'] = jnp.max(b_end - b_r + ig_r, axis=1, keepdims=True)
        s['log_intra'] = jnp.where(s.pop('incl'), b_c - b_r + ig_r, NEG)
        s['mx'] = jnp.max(s['log_intra'], axis=1, keepdims=True)

    def st_exp(s):
        m0 = m_ref[s['c']]
        b_end, a_max = s.pop('b_end'), s.pop('a_max')
        log_inter = s.pop('b_c') + m0
        m_j = jnp.maximum(log_inter, s.pop('mx'))
        s['decay'] = jnp.exp(s.pop('log_intra') - m_j)
        s['w_inter'] = jnp.exp(log_inter - m_j)
        s['floor'] = jnp.exp(-m_j)
        s['m_new'] = jnp.maximum(b_end + m0, a_max)
        s['s_old'] = jnp.exp(b_end + m0 - s['m_new'])
        s['s_loc'] = jnp.exp(a_max - s['m_new'])
        v = s.pop('v')
        wgt = jnp.exp(s.pop('a_c') - a_max)
        s['vext'] = jnp.concatenate([v.astype(BF16), ones], axis=1)
        s['vw'] = jnp.concatenate([v * wgt, jnp.broadcast_to(wgt, (L, 128))], axis=1).astype(BF16)

    def st_pv(s):
        smat = (s.pop('qk') * s.pop('decay')).astype(BF16)
        ct0 = ct_ref[s['c']]
        s['num'] = _dot(smat, s.pop('vext')) + s.pop('w_inter') * _dot(s.pop('q'), ct0.astype(BF16))
        s['ct'] = s.pop('s_old') * ct0 + s.pop('s_loc') * _dot_tn(s.pop('k'), s.pop('vw'))

    def st_out(s):
        num = s.pop('num')
        den = jnp.maximum(jnp.abs(num[:, ML_DV:]), s.pop('floor'))
        h = s['h']
        out_refs[s['d']][s['b'], :, h * ML_DV:(h + 1) * ML_DV] = num[:, :ML_DV] / jnp.concatenate([den, den], axis=1)
        ct_ref[s['c']] = s.pop('ct')
        m_ref[s['c']] = s.pop('m_new')

    n_b = out_refs[0].shape[0]
    states = [{'d': d, 'b': b, 'h': h, 'c': (d * n_b + b) * ML_HEADS + h}
              for d in range(2) for b in range(n_b) for h in range(ML_HEADS)]
    for stage in (st_qk, st_gate, st_max, st_exp, st_pv, st_out):
        for s in states:
            stage(s)


def _mlstm(p, gate_b, cos, sin, ctx_len):
    B, T, _ = p.shape
    L = CHUNK
    nch = T // L
    nctx = ctx_len // L
    H = ML_HEADS
    grow = jnp.swapaxes(p[:, :, MLG_0:MLG_0 + 4 * H].reshape(B, nch, L, 4 * H), 2, 3)

    def specs(d):
        ch = lambda s: _chunk_index(d, s, nctx, nch)
        return [pl.BlockSpec((B, L, H * ML_DQK), lambda s: (0, ch(s), ML_Q0 // (H * ML_DQK))),
                pl.BlockSpec((B, L, H * ML_DQK), lambda s: (0, ch(s), ML_K0 // (H * ML_DQK))),
                pl.BlockSpec((B, L, H * ML_DV), lambda s: (0, ch(s), ML_V0 // (H * ML_DV))),
                pl.BlockSpec((B, L, 128), lambda s: (0, ch(s), MLG_0 // 128)),
                pl.BlockSpec((B, 1, 4 * H, L), lambda s: (0, ch(s), 0, 0)),
                pl.BlockSpec((L, ML_DQK), lambda s: (ch(s), 0)),
                pl.BlockSpec((L, ML_DQK), lambda s: (ch(s), 0))]

    def ospec(d):
        return pl.BlockSpec((B, L, BRANCH_W), lambda s: (0, _chunk_index(d, s, nctx, nch), 0))

    hshape = jax.ShapeDtypeStruct((B, T, BRANCH_W), F32)
    args = (p, p, p, p, grow, cos, sin)
    n_chain = 2 * B * H
    return pl.pallas_call(
        _mlstm_kernel,
        grid=(nch,),
        in_specs=[pl.BlockSpec(memory_space=pltpu.SMEM)] + specs(0) + specs(1),
        out_specs=[ospec(0), ospec(1)],
        out_shape=[hshape, hshape],
        scratch_shapes=[pltpu.VMEM((n_chain, ML_DQK, ML_DV + 128), F32), pltpu.VMEM((n_chain, 1, 1), F32)],
        compiler_params=_cparams(("arbitrary",)),
        name="mlstm",
    )(gate_b, *args, *args)


def _mlstm_post_kernel(hf_ref, hb_ref, o_ref, g_ref, y_ref):
    h = hf_ref[0] + hb_ref[0]
    h = h * lax.rsqrt(jnp.mean(h * h, axis=-1, keepdims=True) + EPS) * g_ref[...]
    y_ref[0] = (h * _sigmoid(o_ref[0])).astype(y_ref.dtype)


def _mlstm_post(hf, hb, p, norm_g):
    B, T, W = hf.shape
    tm = _tile(T, 1024, 16)
    hspec = pl.BlockSpec((1, tm, ML_DV), lambda b, i, h: (b, i, h))
    return pl.pallas_call(
        _mlstm_post_kernel,
        grid=(B, T // tm, ML_HEADS),
        in_specs=[hspec, hspec,
                  pl.BlockSpec((1, tm, ML_DV), lambda b, i, h: (b, i, ML_O0 // ML_DV + h)),
                  pl.BlockSpec((1, ML_DV), lambda b, i, h: (0, h))],
        out_specs=hspec,
        out_shape=jax.ShapeDtypeStruct((B, T, W), BF16),
        compiler_params=_cparams(("parallel", "parallel", "parallel")),
        name="mlstm_post",
    )(hf, hb, p, norm_g.reshape(1, W))


def _rwkv_prep_kernel(x_ref, xp_ref, xn_ref, mu_ref, w0_ref, a0_ref, wdh_ref, wdl_ref, wa_ref, wg_ref,
                      r_ref, k_ref, v_ref, ld_ref, ar_ref, g_ref, *, tt, ctx_len, nt):
    i = pl.program_id(1)
    x = x_ref[0][:, :RW_COLS]
    first = (i == 0) | (i * tt == ctx_len)
    last = ((i + 1) * tt == ctx_len) | (i == nt - 1)
    prev_row = jnp.where(first, 0.0, xp_ref[0][7:8, :RW_COLS])
    next_row = jnp.where(last, 0.0, xn_ref[0][0:1, :RW_COLS])
    ridx = lax.broadcasted_iota(jnp.int32, (tt, 1), 0)
    prev = jnp.where(ridx == 0, prev_row, pltpu.roll(x, 1, 0))
    nxt = jnp.where(ridx == tt - 1, next_row, pltpu.roll(x, tt - 1, 0))
    xs = x + mu_ref[...] * (0.5 * (prev + nxt) - x)

    W = BRANCH_W
    r_ref[0] = xs[:, 0:W]
    k_ref[0] = xs[:, W:2 * W]
    v_ref[0] = xs[:, 2 * W:3 * W]
    wlo = _mm3(_split2(jnp.tanh(xs[:, 3 * W:3 * W + 256])), (wdh_ref[...], wdl_ref[...]))
    alo = _dot(xs[:, 3 * W + 128:3 * W + 384].astype(BF16), wa_ref[...])
    for d in range(2):
        ld_ref[d, 0] = -math.exp(-0.5) * _sigmoid(w0_ref[d:d + 1, :] + wlo[:, d * W:(d + 1) * W])
        ar_ref[d, 0] = _sigmoid(a0_ref[d:d + 1, :] + alo[:, d * W:(d + 1) * W])
    g_ref[0] = _dot(_sigmoid(xs[:, 3 * W + 384:3 * W + 640]).astype(BF16), wg_ref[...])


def _rwkv_prep(p, prm, ctx_len):
    B, T, _ = p.shape
    W = BRANCH_W
    tt = _tile(math.gcd(T, ctx_len), 256, 8)
    nt = T // tt
    h8 = tt // 8
    nb8 = T // 8
    z = lambda *s: jnp.zeros(s, F32)
    wd = jnp.concatenate([
        jnp.concatenate([prm['rw_w_up'][0], z(RW_LORA, W)], axis=1),
        jnp.concatenate([z(RW_LORA, W), prm['rw_w_up'][1]], axis=1),
        z(256 - 2 * RW_LORA, 2 * W)], axis=0)
    wa = jnp.concatenate([
        z(2 * RW_LORA - 128, 2 * W),
        jnp.concatenate([prm['rw_a_up'][0], z(RW_LORA, W)], axis=1),
        jnp.concatenate([z(RW_LORA, W), prm['rw_a_up'][1]], axis=1)], axis=0)
    wg = prm['rw_g_up'].astype(BF16)
    wdh, wdl = _split2(wd)
    wa = wa.astype(BF16)
    mu = prm['rw_mu'].reshape(1, RW_COLS)
    kern = functools.partial(_rwkv_prep_kernel, tt=tt, ctx_len=ctx_len, nt=nt)
    rwblk = RW_0 // RW_SPAN
    tok = jax.ShapeDtypeStruct((B, T, W), F32)
    tok2 = jax.ShapeDtypeStruct((2, B, T, W), F32)
    full = lambda shape: pl.BlockSpec(shape, lambda b, i: (0,) * len(shape))
    ospec = pl.BlockSpec((1, tt, W), lambda b, i: (b, i, 0))
    ospec2 = pl.BlockSpec((2, 1, tt, W), lambda b, i: (0, b, i, 0))
    return pl.pallas_call(
        kern,
        grid=(B, nt),
        in_specs=[pl.BlockSpec((1, tt, RW_SPAN), lambda b, i: (b, i, rwblk)),
                  pl.BlockSpec((1, 8, RW_SPAN), lambda b, i: (b, jnp.maximum(i * h8 - 1, 0), rwblk)),
                  pl.BlockSpec((1, 8, RW_SPAN), lambda b, i: (b, jnp.minimum((i + 1) * h8, nb8 - 1), rwblk)),
                  full((1, RW_COLS)), full((2, W)), full((2, W)),
                  full((256, 2 * W)), full((256, 2 * W)), full((256, 2 * W)), full((256, W))],
        out_specs=[ospec, ospec, ospec, ospec2, ospec2, ospec],
        out_shape=[tok, tok, tok, tok2, tok2, tok],
        compiler_params=_cparams(("parallel", "parallel")),
        name="rwkv_prep",
    )(p, p, p, mu, prm['rw_w0'], prm['rw_a0'], wdh, wdl, wa, wg)


def _split2(x):
    hi = x.astype(BF16)
    return hi, (x - hi.astype(F32)).astype(BF16)


def _split3(x):
    hi = x.astype(BF16)
    r1 = x - hi.astype(F32)
    mid = r1.astype(BF16)
    return hi, mid, (r1 - mid.astype(F32)).astype(BF16)


def _split1(x):
    return (x.astype(BF16),)


def _mm3(a, b, f=_dot):
    acc = f(a[0], b[0])
    if len(b) > 1:
        acc = acc + f(a[0], b[1])
    if len(a) > 1:
        acc = acc + f(a[1], b[0])
    return acc


def _mm_exact(e, parts, f=_dot):
    acc = f(e, parts[0])
    for x in parts[1:]:
        acc = acc + f(e, x)
    return acc


def _rwkv_scan_kernel(*refs, n_pairs):
    L = CHUNK
    P = 2 * RW_DH
    dir_refs = (refs[0:5], refs[5:10])
    kk_ref, ka_ref = refs[10:12]
    y_refs = refs[12:14]
    st_ref = refs[14]

    @pl.when(pl.program_id(1) == 0)
    def _():
        st_ref[...] = jnp.zeros_like(st_ref)

    row = lax.broadcasted_iota(jnp.int32, (L, P), 0)
    col = lax.broadcasted_iota(jnp.int32, (L, P), 1) % RW_DH
    before = (col < row, col > row)
    incl = (col <= row, col >= row)
    eye = (row == col).astype(F32)
    tri = tuple(m[:, :L].astype(BF16) for m in incl)
    lane = lax.broadcasted_iota(jnp.int32, (1, P), 1)
    m0 = lane < RW_DH
    r2 = lax.broadcasted_iota(jnp.int32, (P, P), 0) // RW_DH
    c2 = lax.broadcasted_iota(jnp.int32, (P, P), 1) // RW_DH
    same_head = r2 == c2

    def bd(x):
        z = jnp.zeros_like(x)
        return jnp.concatenate([jnp.where(m0, x, z), jnp.where(m0, z, x)], axis=0)

    def bd2(x):
        return tuple(bd(t) for t in x)

    sp = spi = _split1

    def cat(xs, axis):
        return tuple(jnp.concatenate(list(t), axis=axis) for t in zip(*xs))

    def st_load(s):
        sl = s['sl']
        r_ref, k_ref, v_ref, ld_ref, ar_ref = dir_refs[s['d']]
        s['R'], s['K'], s['V'] = r_ref[0, :, sl], k_ref[0, :, sl], v_ref[0, :, sl]
        s['LD'], s['AR'] = ld_ref[0, 0, :, sl], ar_ref[0, 0, :, sl]
        s['S0'] = st_ref[s['si']]
        kk = s['K'] * kk_ref[:, sl]
        kk2 = kk * kk
        n0 = jnp.sum(jnp.where(m0, kk2, 0.0), axis=1, keepdims=True)
        n1 = jnp.sum(jnp.where(m0, 0.0, kk2), axis=1, keepdims=True)
        s['kk'] = kk / jnp.maximum(jnp.sqrt(jnp.where(m0, n0, n1)), 1e-12)

    def st_cum(s):
        s['KD'] = s.pop('K') * (1.0 + (s['AR'] - 1.0) * ka_ref[:, s['sl']])
        s['BV'] = s['kk'] * s.pop('AR')
        s['c'] = _mm_exact(tri[s['d']], _split3(s['LD']))

    def st_x(s):
        c, LD = s.pop('c'), s.pop('LD')
        c_end = jnp.sum(LD, axis=0, keepdims=True)
        s['dcol'] = jnp.exp(jnp.broadcast_to(c_end, (P, P)).T)
        At = -s.pop('kk') * jnp.exp(c - LD)
        e_neg = jnp.exp(-c)
        BV, KD = s.pop('BV'), s.pop('KD')
        s['Rt'] = s.pop('R') * jnp.exp(c)
        e_end = jnp.exp(c_end - c)
        s['Bh_s'], s['Kh_s'] = sp(BV * e_end), sp(KD * e_end)
        s['At_s'], s['V_s'] = sp(At), sp(s.pop('V'))
        s['X'] = _mm3(cat([s['At_s'], sp(s['Rt'])], 0),
                      cat([bd2(sp(BV * e_neg)), bd2(sp(KD * e_neg))], 0), _dot_nt)

    def st_inv0(s):
        X = s.pop('X')
        bef, inc = before[s['d']], incl[s['d']]
        Nab = jnp.where(bef, X[:L, :P], 0.0)
        s['Nak_s'] = sp(jnp.where(bef, X[:L, P:], 0.0))
        s['Mrb_s'] = sp(jnp.where(inc, X[L:, :P], 0.0))
        s['Mrk_s'] = sp(jnp.where(inc, X[L:, P:], 0.0))
        s['T'] = eye + Nab
        Np_s = spi(Nab)
        s['Z'] = _mm3(Np_s, bd2(Np_s))
        s['NV'] = _mm3(s.pop('Nak_s'), bd2(s['V_s']))

    def st_inv1(s):
        Z = s.pop('Z')
        if Z.shape[0] == 2 * L:
            s['T'] = s['T'] + Z[L:]
        Np_s = spi(Z[:L])
        s['Z'] = _mm3(cat([Np_s, spi(s['T'])], 0), bd2(Np_s))

    def st_inv2(s):
        Z = s.pop('Z')
        s['T'] = s['T'] + Z[L:]
        s['Z'] = _mm3(spi(s['T']), bd2(spi(Z[:L])))

    def st_wu(s):
        T_s = sp(s.pop('T') + s.pop('Z'))
        s['WU'] = _mm3(T_s, cat([bd2(s.pop('At_s')), bd2(sp(s.pop('NV')))], 1))

    def st_loc(s):
        WU = s.pop('WU')
        W_s, Ul_s = sp(WU[:, :P]), sp(WU[:, P:])
        MM = _mm3(s.pop('Mrb_s'), cat([bd2(W_s), bd2(Ul_s)], 1))
        s['Q_s'] = sp(s.pop('Rt') + MM[:, :P])
        s['Yl'] = MM[:, P:] + _mm3(s.pop('Mrk_s'), bd2(s['V_s']))
        PS = _mm3(s.pop('Bh_s'), cat([W_s, Ul_s], 1), _dot_tn)
        s['Pm_s'] = sp(jnp.where(same_head, PS[:, :P], 0.0))
        s['Sl'] = jnp.where(same_head, PS[:, P:] + _mm3(s.pop('Kh_s'), s.pop('V_s'), _dot_tn), 0.0)

    def st_out(s):
        S0 = s.pop('S0')
        S0_s = sp(S0)
        y_refs[s['d']][0, :, s['sl']] = _mm3(s.pop('Q_s'), S0_s) + s.pop('Yl')
        st_ref[s['si']] = s.pop('dcol') * S0 + _mm3(s.pop('Pm_s'), S0_s) + s.pop('Sl')

    n_dbl = int(math.log2(L)) - 2
    stages = [st_load, st_cum, st_x, st_inv0] + [st_inv1] * n_dbl + [st_inv2, st_wu, st_loc, st_out]
    states = [{'d': d, 'si': d * n_pairs + j, 'sl': slice(P * j, P * (j + 1))}
              for d in range(2) for j in range(n_pairs)]
    for stage in stages:
        for s in states:
            stage(s)


def _rwkv_scan(r, k, v, ld, ar, k_k, k_a, ctx_len):
    B, T, W = r.shape
    L = CHUNK
    nch = T // L
    nctx = ctx_len // L
    n_pairs = W // (2 * RW_DH)

    def specs(d):
        tok = pl.BlockSpec((1, L, W), lambda b, s: (b, _chunk_index(d, s, nctx, nch), 0))
        per_dir = pl.BlockSpec((1, 1, L, W), lambda b, s: (d, b, _chunk_index(d, s, nctx, nch), 0))
        return [tok, tok, tok, per_dir, per_dir]

    def ospec(d):
        return pl.BlockSpec((1, L, W), lambda b, s: (b, _chunk_index(d, s, nctx, nch), 0))

    pspec = pl.BlockSpec((1, W), lambda b, s: (0, 0))
    yshape = jax.ShapeDtypeStruct((B, T, W), F32)
    kern = functools.partial(_rwkv_scan_kernel, n_pairs=n_pairs)
    args = (r, k, v, ld, ar)
    return pl.pallas_call(
        kern,
        grid=(B, nch),
        in_specs=specs(0) + specs(1) + [pspec, pspec],
        out_specs=[ospec(0), ospec(1)],
        out_shape=[yshape, yshape],
        scratch_shapes=[pltpu.VMEM((2 * n_pairs, 2 * RW_DH, 2 * RW_DH), F32)],
        compiler_params=_cparams(("parallel", "arbitrary")),
        name="rwkv_scan",
    )(*args, *args, k_k.reshape(1, W), k_a.reshape(1, W))


def _rwkv_post_kernel(yf_ref, yb_ref, r_ref, k_ref, v_ref, g_ref, rk_ref, lw_ref, lb_ref, o_ref):
    P = 2 * RW_DH
    r2 = lax.broadcasted_iota(jnp.int32, (P, P), 0) // RW_DH
    c2 = lax.broadcasted_iota(jnp.int32, (P, P), 1) // RW_DH
    ones_bd = (r2 == c2).astype(BF16)

    def head_sum(x):
        return _mm_exact(ones_bd, _split2(x), lambda e, t: _dot(t, e))

    for j in range(o_ref.shape[2] // P):
        sl = slice(P * j, P * (j + 1))
        y = yf_ref[0, :, sl] + yb_ref[0, :, sl]
        yc = y - head_sum(y) * (1.0 / RW_DH)
        var = head_sum(yc * yc) * (1.0 / RW_DH)
        yn = yc * lax.rsqrt(var + RW_GN_EPS)
        bonus = head_sum(r_ref[0, :, sl] * k_ref[0, :, sl] * rk_ref[:, sl]) * v_ref[0, :, sl]
        o_ref[0, :, sl] = ((yn * lw_ref[:, sl] + lb_ref[:, sl] + bonus) * g_ref[0, :, sl]).astype(o_ref.dtype)


def _rwkv_post(y_f, y_b, r, k, v, g, r_k, ln_w, ln_b):
    B, T, W = y_f.shape
    tt = _tile(T, 512, 16)
    tspec = pl.BlockSpec((1, tt, W), lambda b, i: (b, i, 0))
    pspec = pl.BlockSpec((1, W), lambda b, i: (0, 0))
    return pl.pallas_call(
        _rwkv_post_kernel,
        grid=(B, T // tt),
        in_specs=[tspec, tspec, tspec, tspec, tspec, tspec, pspec, pspec, pspec],
        out_specs=tspec,
        out_shape=jax.ShapeDtypeStruct((B, T, W), BF16),
        compiler_params=_cparams(("parallel", "parallel")),
        name="rwkv_post",
    )(y_f, y_b, r, k, v, g, r_k.reshape(1, W), ln_w.reshape(1, W), ln_b.reshape(1, W))


def _rwkv_branch(p, prm, ctx_len):
    r, k, v, ld, ar, g = _rwkv_prep(p, prm, ctx_len)
    y_f, y_b = _rwkv_scan(r, k, v, ld, ar, prm['rw_k_k'], prm['rw_k_a'], ctx_len)
    return _rwkv_post(y_f, y_b, r, k, v, g, prm['rw_r_k'], prm['rw_ln_w'], prm['rw_ln_b'])


NA_RB = 4
NA_SLAB = NA_RB + NA_WIN_R - 1
NA_HB = 4


def _na_kernel(q_ref, k_ref, v_ref, bias_ref, o_ref, *, ctx_len, rows):
    rb = pl.program_id(2)
    scale = NA_DH ** -0.5
    heads = [slice(i * NA_DH, (i + 1) * NA_DH) for i in range(NA_HB)]

    @pl.when(rb == 0)
    def _():
        for hs in heads:
            s_ctx = _dot_nt(q_ref[0, :, hs].astype(BF16), k_ref[0, 0:ctx_len, hs].astype(BF16)) * scale
            p_ctx = jnp.exp(s_ctx - jnp.max(s_ctx, axis=-1, keepdims=True))
            y = _dot(p_ctx.astype(BF16), v_ref[0, 0:ctx_len, hs].astype(BF16))
            o_ref[0, :, hs] = (y / jnp.sum(p_ctx, axis=-1, keepdims=True)).astype(o_ref.dtype)

    @pl.when(rb > 0)
    def _():
        nslab = NA_SLAB * GRID_W
        start = jnp.clip((rb - 1) * NA_RB - NA_WIN_R // 2, 0, rows - NA_SLAB)
        koff = pl.multiple_of(ctx_len + start * GRID_W, GRID_W)

        def st_scores(s):
            hs = s['hs']
            q = q_ref[0, :, hs].astype(BF16)
            s['s_win'] = _dot_nt(q, k_ref[0, pl.ds(koff, nslab), hs].astype(BF16)) * scale + bias_ref[0, s['i'], 0]
            s['s_ctx'] = _dot_nt(q, k_ref[0, 0:ctx_len, hs].astype(BF16)) * scale

        def st_softmax(s):
            s_win, s_ctx = s.pop('s_win'), s.pop('s_ctx')
            m = jnp.maximum(jnp.max(s_win, axis=-1, keepdims=True), jnp.max(s_ctx, axis=-1, keepdims=True))
            p_win = jnp.exp(s_win - m)
            p_ctx = jnp.exp(s_ctx - m)
            s['den'] = jnp.sum(p_win, axis=-1, keepdims=True) + jnp.sum(p_ctx, axis=-1, keepdims=True)
            hs = s['hs']
            s['y'] = (_dot(p_win.astype(BF16), v_ref[0, pl.ds(koff, nslab), hs].astype(BF16))
                      + _dot(p_ctx.astype(BF16), v_ref[0, 0:ctx_len, hs].astype(BF16)))

        def st_out(s):
            o_ref[0, :, s['hs']] = (s.pop('y') / s.pop('den')).astype(o_ref.dtype)

        states = [{'i': i, 'hs': hs} for i, hs in enumerate(heads)]
        for stage in (st_scores, st_softmax, st_out):
            for s in states:
                stage(s)


def _na_block_type(lb, nblk):
    return jnp.where(lb <= 0, 0, jnp.where(lb == nblk - 1, 2, 1))


def _na_bias_kernel(toe_ref, o_ref, *, ro, row_ok):
    neg = jnp.full((GRID_W, GRID_W), NEG, F32)
    for t in range(ro.shape[0]):
        for ri in range(NA_RB):
            for kj in range(NA_SLAB):
                tile = toe_ref[0, 0, int(ro[t, ri, kj])] if row_ok[t, ri, kj] else neg
                o_ref[0, 0, t, ri * GRID_W:(ri + 1) * GRID_W, kj * GRID_W:(kj + 1) * GRID_W] = tile


def _na_bias_table(rpb, rows):
    depth, H = rpb.shape[:2]
    c = np.arange(GRID_W)
    c_start = np.clip(c - NA_WIN_C // 2, 0, GRID_W - NA_WIN_C)
    col_ok = (c[None, :] >= c_start[:, None]) & (c[None, :] < c_start[:, None] + NA_WIN_C)
    col_off = np.clip(c[None, :] - c[:, None], 1 - NA_WIN_C, NA_WIN_C - 1) + (NA_WIN_C - 1)
    sel_c = jnp.asarray(col_off[..., None] == np.arange(2 * NA_WIN_C - 1), F32)
    toe = jnp.einsum('lhoc,qkc->lhoqk', rpb, sel_c, precision=HI) + jnp.asarray(np.where(col_ok, 0.0, NEG), F32)
    nblk = rows // NA_RB
    reps = np.array([0, min(1, nblk - 1), nblk - 1])
    r = reps[:, None] * NA_RB + np.arange(NA_RB)[None, :]
    slab0 = np.clip(reps * NA_RB - NA_WIN_R // 2, 0, rows - NA_SLAB)
    kr = slab0[:, None] + np.arange(NA_SLAB)[None, :]
    win0 = np.clip(r - NA_WIN_R // 2, 0, rows - NA_WIN_R)
    ro = kr[:, None, :] - r[:, :, None] + (NA_WIN_R - 1)
    row_ok = (kr[:, None, :] >= win0[:, :, None]) & (kr[:, None, :] < win0[:, :, None] + NA_WIN_R)
    kern = functools.partial(_na_bias_kernel, ro=ro, row_ok=row_ok)
    nq, nk = NA_RB * GRID_W, NA_SLAB * GRID_W
    return pl.pallas_call(
        kern,
        grid=(depth, H),
        in_specs=[pl.BlockSpec((1, 1, 2 * NA_WIN_R - 1, GRID_W, GRID_W), lambda l, h: (l, h, 0, 0, 0))],
        out_specs=pl.BlockSpec((1, 1, 3, nq, nk), lambda l, h: (l, h, 0, 0, 0)),
        out_shape=jax.ShapeDtypeStruct((depth, H, 3, nq, nk), F32),
        compiler_params=_cparams(("parallel", "parallel")),
        name="na_bias",
    )(toe)


def _na_branch(p, table, l, ctx_len):
    B, T, _ = p.shape
    seq = T - ctx_len
    rows = seq // GRID_W
    H = NA_HEADS
    nq = NA_RB * GRID_W
    assert ctx_len == nq and rows % NA_RB == 0 and rows >= NA_SLAB, (ctx_len, rows)
    nblk = rows // NA_RB
    wb = NA_HB * NA_DH
    qb, kb, vb = NA_0 // wb, (NA_0 + BRANCH_W) // wb, (NA_0 + 2 * BRANCH_W) // wb
    kern = functools.partial(_na_kernel, ctx_len=ctx_len, rows=rows)
    return pl.pallas_call(
        kern,
        grid=(B, H // NA_HB, 1 + nblk),
        in_specs=[pl.BlockSpec((1, nq, wb), lambda b, h, r: (b, r, qb + h)),
                  pl.BlockSpec((1, T, wb), lambda b, h, r: (b, 0, kb + h)),
                  pl.BlockSpec((1, T, wb), lambda b, h, r: (b, 0, vb + h)),
                  pl.BlockSpec((1, NA_HB, 1, nq, NA_SLAB * GRID_W),
                               lambda b, h, r: (l, h, _na_block_type(r - 1, nblk), 0, 0))],
        out_specs=pl.BlockSpec((1, nq, wb), lambda b, h, r: (b, r, h)),
        out_shape=jax.ShapeDtypeStruct((B, T, BRANCH_W), BF16),
        compiler_params=_cparams(("parallel", "parallel", "arbitrary")),
        name="na",
    )(p, p, p, table)


def _pack_kernel(w_ref, o_ref):
    ml, rw, na, gt = 3088, 3712, 3072, 6144
    o_rw, o_na, o_gt = ml, ml + rw, ml + rw + na
    w = w_ref[0]
    n = w.shape[0]
    o_ref[0, :, 0:NA_0] = w[:, 0:NA_0].astype(BF16)
    o_ref[0, :, NA_0:GATE_0] = w[:, o_na:o_na + na].astype(BF16)
    o_ref[0, :, GATE_0:RW_0] = w[:, o_gt:o_gt + gt].astype(BF16)
    o_ref[0, :, RW_0:MLG_0] = w[:, o_rw:o_rw + rw].astype(BF16)
    tail = jnp.concatenate([w[:, NA_0:ml], jnp.zeros((n, 128 - (ml - NA_0)), F32)], axis=1)
    o_ref[0, :, MLG_0:MLG_0 + 128] = tail.astype(BF16)
    o_ref[0, :, MLG_0 + 128:] = jnp.zeros((n, P_COLS - MLG_0 - 128), BF16)


def _pack_w_in(w_in, l):
    _, D, N = w_in.shape
    tr = _tile(D, 64, 16)
    return pl.pallas_call(
        _pack_kernel,
        grid=(D // tr,),
        in_specs=[pl.BlockSpec((1, tr, N), lambda i: (l, i, 0))],
        out_specs=pl.BlockSpec((1, tr, P_COLS), lambda i: (0, i, 0)),
        out_shape=jax.ShapeDtypeStruct((1, D, P_COLS), BF16),
        compiler_params=_cparams(("parallel",)),
        name="pack_w_in",
    )(w_in)


def _rope_tables(seq, ctx_len):
    t = jnp.arange(seq)
    nf = ML_DQK // 4
    freqs = ROPE_BASE ** (-jnp.arange(nf, dtype=F32) / nf)
    ang_r = (t // GRID_W).astype(F32)[:, None] * freqs[None, :]
    ang_c = (t % GRID_W).astype(F32)[:, None] * freqs[None, :]
    cos = jnp.concatenate([jnp.cos(ang_r)] * 2 + [jnp.cos(ang_c)] * 2, axis=1)
    sin = jnp.concatenate([-jnp.sin(ang_r), jnp.sin(ang_r), -jnp.sin(ang_c), jnp.sin(ang_c)], axis=1)
    cos = jnp.concatenate([jnp.ones((ctx_len, ML_DQK), F32), cos], axis=0)
    sin = jnp.concatenate([jnp.zeros((ctx_len, ML_DQK), F32), sin], axis=0)
    return cos, sin


def kernel(x, c, ctx, c_ctx, w_mod, b_mod, norm1_g, w_in, ml_gate_b, ml_norm_g, rw_mu, rw_w0, rw_w_up, rw_a0, rw_a_up, rw_g_up, rw_k_k, rw_k_a, rw_r_k, rw_ln_w, rw_ln_b, na_rpb, w_branch, w_out, norm2_g, w_ff1, w_ff2, final_g):
    B, seq, D = x.shape
    ctx_len = ctx.shape[1]
    depth = w_mod.shape[0]
    xs = jnp.concatenate([ctx, x], axis=1)
    s_all = jax.nn.silu(jnp.concatenate([c_ctx[None, :], c], axis=0))
    s_all = jnp.concatenate([s_all, jnp.zeros((8 - (B + 1) % 8, D), F32)], axis=0) if (B + 1) % 8 else s_all
    cos, sin = _rope_tables(seq, ctx_len)
    na_table = _na_bias_table(na_rpb, seq // GRID_W)
    wb_bf, wo_bf, w2_bf = (w.astype(BF16) for w in (w_branch, w_out, w_ff2))
    for l in range(depth):
        mod = _modulation(s_all, w_mod, b_mod, l).reshape(-1, 6, D)
        mc = jnp.broadcast_to(mod[0:1], (B, 6, D))
        ml_ = mod[1:B + 1]
        mod1 = jnp.stack([mc[:, 0], mc[:, 1], ml_[:, 0], ml_[:, 1]], axis=1)
        gate1 = jnp.stack([mc[:, 2], ml_[:, 2]], axis=1)
        mod2 = jnp.stack([mc[:, 3], mc[:, 4], ml_[:, 3], ml_[:, 4]], axis=1)
        gate2 = jnp.stack([mc[:, 5], ml_[:, 5]], axis=1)
        prm = {'rw_mu': rw_mu[l], 'rw_w0': rw_w0[l], 'rw_w_up': rw_w_up[l], 'rw_a0': rw_a0[l],
               'rw_a_up': rw_a_up[l], 'rw_g_up': rw_g_up[l], 'rw_k_k': rw_k_k[l], 'rw_k_a': rw_k_a[l],
               'rw_r_k': rw_r_k[l], 'rw_ln_w': rw_ln_w[l], 'rw_ln_b': rw_ln_b[l]}

        p = _mm(_norm_mod(xs, norm1_g[l], mod1, ctx_len), _pack_w_in(w_in, l), 0, F32, False)
        h_f, h_b = _mlstm(p, ml_gate_b[l], cos, sin, ctx_len)
        y_m = _mlstm_post(h_f, h_b, p, ml_norm_g[l])
        y_r = _rwkv_branch(p, prm, ctx_len)
        y_n = _na_branch(p, na_table, l, ctx_len)
        merged = _merge(y_m, y_r, y_n, p, wb_bf, l)
        xs, h2 = _mm_res_norm(merged, wo_bf, l, xs, gate1, norm2_g[l], mod2, ctx_len)
        hff = _mm(h2, w_ff1, l, BF16, True)
        xs = _mm_res(hff, w2_bf, l, xs, gate2, ctx_len)
    return _final_norm(xs, final_g, ctx_len)
```

```python
import functools
import math

import jax
import jax.numpy as jnp
import numpy as np
from jax import lax
from jax.experimental import pallas as pl
from jax.experimental.pallas import tpu as pltpu

F32 = jnp.float32
BF16 = jnp.bfloat16
HI = lax.Precision.HIGHEST

D_MODEL = 2048
GRID_W = 64
BRANCH_W = D_MODEL // 2
N_BRANCH = 3
ML_HEADS = 4
ML_DV = BRANCH_W // ML_HEADS
ML_DQK = ML_DV // 2
RW_DH = 64
RW_HEADS = BRANCH_W // RW_DH
RW_LORA = 96
RW_GATE_LORA = 256
RW_GN_EPS = 64e-5
NA_DH = 128
NA_HEADS = BRANCH_W // NA_DH
NA_WIN_R = 8
NA_WIN_C = 16
D_FF = 4 * D_MODEL
ROPE_BASE = 10000.0
EPS = 1e-6
CHUNK = 64
NEG = -1e30

ML_Q0, ML_K0, ML_V0, ML_O0 = 0, 512, 1024, 2048
NA_0 = 3072
GATE_0 = 6144
RW_0 = 12288
RW_SPAN = 4096
RW_COLS = 3712
MLG_0 = RW_0 + RW_COLS
P_COLS = RW_0 + RW_SPAN

VMEM_LIMIT = 56 * 1024 * 1024
MM_ROWS = 1100


def _cparams(sem):
    return pltpu.CompilerParams(dimension_semantics=sem, vmem_limit_bytes=VMEM_LIMIT)


def _tile(n, cap, mult):
    best = None
    for t in range(mult, min(n, cap) + 1, mult):
        if n % t == 0:
            best = t
    assert best is not None, (n, cap, mult)
    return best


def _dot(a, b, prec=None):
    return jnp.dot(a, b, preferred_element_type=F32, precision=prec)


def _dot_nt(a, b, prec=None):
    return lax.dot_general(a, b, (((1,), (1,)), ((), ())), preferred_element_type=F32, precision=prec)


def _dot_tn(a, b, prec=None):
    return lax.dot_general(a, b, (((0,), (0,)), ((), ())), preferred_element_type=F32, precision=prec)


def _sigmoid(x):
    return 1.0 / (1.0 + jnp.exp(-x))


def _softplus(x):
    return jnp.maximum(x, 0.0) + jnp.log(1.0 + jnp.exp(-jnp.abs(x)))


def _mod_kernel(s_ref, w_ref, b_ref, o_ref):
    o_ref[...] = _dot(s_ref[...], w_ref[0], HI) + b_ref[0]


def _modulation(s, w, b, l):
    R, D = s.shape
    N = w.shape[2]
    tn = _tile(N, 1024, 128)
    return pl.pallas_call(
        _mod_kernel,
        grid=(N // tn,),
        in_specs=[pl.BlockSpec((R, D), lambda j: (0, 0)),
                  pl.BlockSpec((1, D, tn), lambda j: (l, 0, j)),
                  pl.BlockSpec((1, 1, tn), lambda j: (l, 0, j))],
        out_specs=pl.BlockSpec((R, tn), lambda j: (0, j)),
        out_shape=jax.ShapeDtypeStruct((R, N), F32),
        compiler_params=_cparams(("arbitrary",)),
        name="modulation",
    )(s, w, b.reshape(b.shape[0], 1, N))


def _norm_mod_kernel(x_ref, g_ref, mod_ref, o_ref, *, tm, ctx_len):
    x = x_ref[0]
    y = x * lax.rsqrt(jnp.mean(x * x, axis=-1, keepdims=True) + EPS) * g_ref[...]
    row = pl.program_id(1) * tm + lax.broadcasted_iota(jnp.int32, (tm, 1), 0)
    is_ctx = row < ctx_len
    m = mod_ref[0]
    shift = jnp.where(is_ctx, m[0:1], m[2:3])
    scale = jnp.where(is_ctx, m[1:2], m[3:4])
    o_ref[0] = (y * (1.0 + scale) + shift).astype(o_ref.dtype)


def _norm_mod(x, g, mod4, ctx_len):
    B, T, D = x.shape
    tm = _tile(T, 320, 16)
    kern = functools.partial(_norm_mod_kernel, tm=tm, ctx_len=ctx_len)
    return pl.pallas_call(
        kern,
        grid=(B, T // tm),
        in_specs=[pl.BlockSpec((1, tm, D), lambda b, i: (b, i, 0)),
                  pl.BlockSpec((1, D), lambda b, i: (0, 0)),
                  pl.BlockSpec((1, 4, D), lambda b, i: (b, 0, 0))],
        out_specs=pl.BlockSpec((1, tm, D), lambda b, i: (b, i, 0)),
        out_shape=jax.ShapeDtypeStruct((B, T, D), BF16),
        compiler_params=_cparams(("parallel", "parallel")),
        name="norm_mod",
    )(x, g.reshape(1, D), mod4)


def _mm_kernel(h_ref, w_ref, o_ref, *, relu2):
    acc = _dot(h_ref[0], w_ref[0].astype(BF16))
    if relu2:
        acc = jnp.square(jnp.maximum(acc, 0.0))
    o_ref[0] = acc.astype(o_ref.dtype)


def _mm(h, w, l, out_dtype, relu2):
    B, T, D = h.shape
    N = w.shape[2]
    tm = _tile(T, 2 * MM_ROWS, 16)
    tn = _tile(N, 512, 128)
    kern = functools.partial(_mm_kernel, relu2=relu2)
    return pl.pallas_call(
        kern,
        grid=(B, T // tm, N // tn),
        in_specs=[pl.BlockSpec((1, tm, D), lambda b, i, j: (b, i, 0)),
                  pl.BlockSpec((1, D, tn), lambda b, i, j: (l, 0, j))],
        out_specs=pl.BlockSpec((1, tm, tn), lambda b, i, j: (b, i, j)),
        out_shape=jax.ShapeDtypeStruct((B, T, N), out_dtype),
        compiler_params=_cparams(("parallel", "parallel", "arbitrary")),
        name="mm",
    )(h, w)


def _row_is_ctx(tm, ctx_len):
    row = pl.program_id(1) * tm + lax.broadcasted_iota(jnp.int32, (tm, 1), 0)
    return row < ctx_len


def _mm_res_kernel(a_ref, w_ref, x_ref, gate_ref, o_ref, acc_ref, *, tm, ctx_len, nk):
    k = pl.program_id(3)

    @pl.when(k == 0)
    def _():
        acc_ref[...] = jnp.zeros_like(acc_ref)

    acc_ref[...] += _dot(a_ref[0], w_ref[0])

    @pl.when(k == nk - 1)
    def _():
        gt = gate_ref[0]
        gate = jnp.where(_row_is_ctx(tm, ctx_len), gt[0:1], gt[1:2])
        o_ref[0] = x_ref[0] + gate * acc_ref[...]


def _mm_res(a, w, l, x, gate2, ctx_len):
    B, T, K = a.shape
    D = w.shape[2]
    tm = _tile(T, MM_ROWS, 16)
    tn = _tile(D, 1024, 128)
    tk = _tile(K, 2048, 128)
    nk = K // tk
    kern = functools.partial(_mm_res_kernel, tm=tm, ctx_len=ctx_len, nk=nk)
    return pl.pallas_call(
        kern,
        grid=(B, T // tm, D // tn, nk),
        in_specs=[pl.BlockSpec((1, tm, tk), lambda b, i, j, k: (b, i, k)),
                  pl.BlockSpec((1, tk, tn), lambda b, i, j, k: (l, k, j)),
                  pl.BlockSpec((1, tm, tn), lambda b, i, j, k: (b, i, j)),
                  pl.BlockSpec((1, 2, tn), lambda b, i, j, k: (b, 0, j))],
        out_specs=pl.BlockSpec((1, tm, tn), lambda b, i, j, k: (b, i, j)),
        out_shape=jax.ShapeDtypeStruct((B, T, D), F32),
        scratch_shapes=[pltpu.VMEM((tm, tn), F32)],
        compiler_params=_cparams(("parallel", "parallel", "parallel", "arbitrary")),
        name="mm_res",
    )(a, w, x, gate2)


def _mm_res_norm_kernel(a_ref, w_ref, x_ref, gate_ref, g_ref, mod_ref, o_ref, h_ref, *, tm, ctx_len):
    is_ctx = _row_is_ctx(tm, ctx_len)
    gt = gate_ref[0]
    xs = x_ref[0] + jnp.where(is_ctx, gt[0:1], gt[1:2]) * _dot(a_ref[0], w_ref[0])
    o_ref[0] = xs
    y = xs * lax.rsqrt(jnp.mean(xs * xs, axis=-1, keepdims=True) + EPS) * g_ref[...]
    m = mod_ref[0]
    shift = jnp.where(is_ctx, m[0:1], m[2:3])
    scale = jnp.where(is_ctx, m[1:2], m[3:4])
    h_ref[0] = (y * (1.0 + scale) + shift).astype(h_ref.dtype)


def _mm_res_norm(a, w, l, x, gate2, g, mod4, ctx_len):
    B, T, K = a.shape
    D = w.shape[2]
    tm = _tile(T, MM_ROWS // 4, 16)
    kern = functools.partial(_mm_res_norm_kernel, tm=tm, ctx_len=ctx_len)
    row = lambda width: pl.BlockSpec((1, tm, width), lambda b, i: (b, i, 0))
    return pl.pallas_call(
        kern,
        grid=(B, T // tm),
        in_specs=[row(K),
                  pl.BlockSpec((1, K, D), lambda b, i: (l, 0, 0)),
                  row(D),
                  pl.BlockSpec((1, 2, D), lambda b, i: (b, 0, 0)),
                  pl.BlockSpec((1, D), lambda b, i: (0, 0)),
                  pl.BlockSpec((1, 4, D), lambda b, i: (b, 0, 0))],
        out_specs=[row(D), row(D)],
        out_shape=[jax.ShapeDtypeStruct((B, T, D), F32), jax.ShapeDtypeStruct((B, T, D), BF16)],
        compiler_params=_cparams(("parallel", "parallel")),
        name="mm_res_norm",
    )(a, w, x, gate2, g.reshape(1, D), mod4)


def _merge_kernel(ym_ref, yr_ref, yn_ref, g0_ref, g1_ref, g2_ref, wb_ref, o_ref):
    acc = _sigmoid(g0_ref[0]) * _dot(ym_ref[0], wb_ref[0, 0])
    acc += _sigmoid(g1_ref[0]) * _dot(yr_ref[0], wb_ref[0, 1])
    acc += _sigmoid(g2_ref[0]) * _dot(yn_ref[0], wb_ref[0, 2])
    o_ref[0] = acc.astype(o_ref.dtype)


def _merge(ym, yr, yn, p, wb, l):
    B, T, W = ym.shape
    D = wb.shape[3]
    tm = _tile(T, MM_ROWS, 16)
    tn = _tile(D, 512, 128)
    yspec = pl.BlockSpec((1, tm, W), lambda b, i, j: (b, i, 0))

    def gspec(n):
        base = (GATE_0 + n * D) // tn
        return pl.BlockSpec((1, tm, tn), lambda b, i, j: (b, i, base + j))

    return pl.pallas_call(
        _merge_kernel,
        grid=(B, T // tm, D // tn),
        in_specs=[yspec, yspec, yspec, gspec(0), gspec(1), gspec(2),
                  pl.BlockSpec((1, N_BRANCH, W, tn), lambda b, i, j: (l, 0, 0, j))],
        out_specs=pl.BlockSpec((1, tm, tn), lambda b, i, j: (b, i, j)),
        out_shape=jax.ShapeDtypeStruct((B, T, D), BF16),
        compiler_params=_cparams(("parallel", "parallel", "arbitrary")),
        name="merge",
    )(ym, yr, yn, p, p, p, wb)


def _final_norm_kernel(x_ref, g_ref, o_ref):
    x = x_ref[0]
    o_ref[0] = x * lax.rsqrt(jnp.mean(x * x, axis=-1, keepdims=True) + EPS) * g_ref[...]


def _final_norm(xs, g, ctx_len):
    B, T, D = xs.shape
    seq = T - ctx_len
    tm = _tile(math.gcd(seq, ctx_len), 256, 8)
    off = ctx_len // tm
    return pl.pallas_call(
        _final_norm_kernel,
        grid=(B, seq // tm),
        in_specs=[pl.BlockSpec((1, tm, D), lambda b, i: (b, off + i, 0)),
                  pl.BlockSpec((1, D), lambda b, i: (0, 0))],
        out_specs=pl.BlockSpec((1, tm, D), lambda b, i: (b, i, 0)),
        out_shape=jax.ShapeDtypeStruct((B, seq, D), F32),
        compiler_params=_cparams(("parallel", "parallel")),
        name="final_norm",
    )(xs, g.reshape(1, D))


def _chunk_index(d, s, n_ctx, n_all):
    back = jnp.where(s < n_ctx, n_ctx - 1 - s, n_all + n_ctx - 1 - s)
    return jnp.where(d == 0, s, back)


def _rope(x, cos, sin_signed, first_half):
    swapped = jnp.where(first_half, pltpu.roll(x, 96, 1), pltpu.roll(x, 32, 1))
    return x * cos + swapped * sin_signed


def _log_sigmoid(x):
    return jnp.minimum(x, 0.0) - jnp.log(1.0 + jnp.exp(-jnp.abs(x)))


def _mlstm_kernel(bias_ref, *refs):
    L = CHUNK
    n_in = 7
    dir_refs = (refs[:n_in], refs[n_in:2 * n_in])
    out_refs = refs[2 * n_in:2 * n_in + 2]
    ct_ref, m_ref = refs[2 * n_in + 2:]

    @pl.when(pl.program_id(0) == 0)
    def _():
        ct_ref[...] = jnp.zeros_like(ct_ref)
        m_ref[...] = jnp.zeros_like(m_ref)

    row = lax.broadcasted_iota(jnp.int32, (L, L), 0)
    col = lax.broadcasted_iota(jnp.int32, (L, L), 1)
    lane = lax.broadcasted_iota(jnp.int32, (L, ML_DQK), 1)
    first_half = (lane % 64) < 32
    ones = jnp.ones((L, 128), BF16)

    def st_qk(s):
        d, h = s['d'], s['h']
        q_ref, k_ref, v_ref, _, _, cos_ref, sin_ref = dir_refs[d]
        cos, sin = cos_ref[...], sin_ref[...]
        hq = slice(h * ML_DQK, (h + 1) * ML_DQK)
        b = s['b']
        s['q'] = _rope(q_ref[b, :, hq] * (ML_DQK ** -0.5), cos, sin, first_half).astype(BF16)
        s['k'] = _rope(k_ref[b, :, hq], cos, sin, first_half).astype(BF16)
        s['v'] = v_ref[b, :, h * ML_DV:(h + 1) * ML_DV]
        s['qk'] = _dot_nt(s['q'], s['k'])

    def st_gate(s):
        d, h = s['d'], s['h']
        gc_ref, gr_ref = dir_refs[d][3:5]
        b_i = bias_ref[8 * d + h]
        b_f = bias_ref[8 * d + 4 + h]
        ji, jf = 8 * d + h, 8 * d + 4 + h
        gc = gc_ref[s['b']]
        gr = gr_ref[s['b'], 0]
        s['ig_c'] = gc[:, ji:ji + 1] + b_i
        lf_c = _log_sigmoid(gc[:, jf:jf + 1] + b_f)
        s['ig_r'] = gr[ji:ji + 1, :] + b_i
        lf_r = _log_sigmoid(gr[jf:jf + 1, :] + b_f)
        s['incl'] = (col <= row) if d == 0 else (col >= row)
        incl_t = (row <= col) if d == 0 else (row >= col)
        s['b_c'] = jnp.sum(jnp.where(s['incl'], lf_r, 0.0), axis=1, keepdims=True)
        s['b_r'] = jnp.sum(jnp.where(incl_t, lf_c, 0.0), axis=0, keepdims=True)
        s['b_end'] = jnp.sum(lf_r, axis=1, keepdims=True)

    def st_max(s):
        b_c, b_r, b_end, ig_r = s['b_c'], s.pop('b_r'), s['b_end'], s.pop('ig_r')
        s['a_c'] = b_end - b_c + s.pop('ig_c')
        s['a_max'] = jnp.max(b_end - b_r + ig_r, axis=1, keepdims=True)
        s['log_intra'] = jnp.where(s.pop('incl'), b_c - b_r + ig_r, NEG)
        s['mx'] = jnp.max(s['log_intra'], axis=1, keepdims=True)

    def st_exp(s):
        m0 = m_ref[s['c']]
        b_end, a_max = s.pop('b_end'), s.pop('a_max')
        log_inter = s.pop('b_c') + m0
        m_j = jnp.maximum(log_inter, s.pop('mx'))
        s['decay'] = jnp.exp(s.pop('log_intra') - m_j)
        s['w_inter'] = jnp.exp(log_inter - m_j)
        s['floor'] = jnp.exp(-m_j)
        s['m_new'] = jnp.maximum(b_end + m0, a_max)
        s['s_old'] = jnp.exp(b_end + m0 - s['m_new'])
        s['s_loc'] = jnp.exp(a_max - s['m_new'])
        v = s.pop('v')
        wgt = jnp.exp(s.pop('a_c') - a_max)
        s['vext'] = jnp.concatenate([v.astype(BF16), ones], axis=1)
        s['vw'] = jnp.concatenate([v * wgt, jnp.broadcast_to(wgt, (L, 128))], axis=1).astype(BF16)

    def st_pv(s):
        smat = (s.pop('qk') * s.pop('decay')).astype(BF16)
        ct0 = ct_ref[s['c']]
        s['num'] = _dot(smat, s.pop('vext')) + s.pop('w_inter') * _dot(s.pop('q'), ct0.astype(BF16))
        s['ct'] = s.pop('s_old') * ct0 + s.pop('s_loc') * _dot_tn(s.pop('k'), s.pop('vw'))

    def st_out(s):
        num = s.pop('num')
        den = jnp.maximum(jnp.abs(num[:, ML_DV:]), s.pop('floor'))
        h = s['h']
        out_refs[s['d']][s['b'], :, h * ML_DV:(h + 1) * ML_DV] = num[:, :ML_DV] / jnp.concatenate([den, den], axis=1)
        ct_ref[s['c']] = s.pop('ct')
        m_ref[s['c']] = s.pop('m_new')

    n_b = out_refs[0].shape[0]
    states = [{'d': d, 'b': b, 'h': h, 'c': (d * n_b + b) * ML_HEADS + h}
              for d in range(2) for b in range(n_b) for h in range(ML_HEADS)]
    for stage in (st_qk, st_gate, st_max, st_exp, st_pv, st_out):
        for s in states:
            stage(s)


def _mlstm(p, gate_b, cos, sin, ctx_len):
    B, T, _ = p.shape
    L = CHUNK
    nch = T // L
    nctx = ctx_len // L
    H = ML_HEADS
    grow = jnp.swapaxes(p[:, :, MLG_0:MLG_0 + 4 * H].reshape(B, nch, L, 4 * H), 2, 3)

    def specs(d):
        ch = lambda s: _chunk_index(d, s, nctx, nch)
        return [pl.BlockSpec((B, L, H * ML_DQK), lambda s: (0, ch(s), ML_Q0 // (H * ML_DQK))),
                pl.BlockSpec((B, L, H * ML_DQK), lambda s: (0, ch(s), ML_K0 // (H * ML_DQK))),
                pl.BlockSpec((B, L, H * ML_DV), lambda s: (0, ch(s), ML_V0 // (H * ML_DV))),
                pl.BlockSpec((B, L, 128), lambda s: (0, ch(s), MLG_0 // 128)),
                pl.BlockSpec((B, 1, 4 * H, L), lambda s: (0, ch(s), 0, 0)),
                pl.BlockSpec((L, ML_DQK), lambda s: (ch(s), 0)),
                pl.BlockSpec((L, ML_DQK), lambda s: (ch(s), 0))]

    def ospec(d):
        return pl.BlockSpec((B, L, BRANCH_W), lambda s: (0, _chunk_index(d, s, nctx, nch), 0))

    hshape = jax.ShapeDtypeStruct((B, T, BRANCH_W), F32)
    args = (p, p, p, p, grow, cos, sin)
    n_chain = 2 * B * H
    return pl.pallas_call(
        _mlstm_kernel,
        grid=(nch,),
        in_specs=[pl.BlockSpec(memory_space=pltpu.SMEM)] + specs(0) + specs(1),
        out_specs=[ospec(0), ospec(1)],
        out_shape=[hshape, hshape],
        scratch_shapes=[pltpu.VMEM((n_chain, ML_DQK, ML_DV + 128), F32), pltpu.VMEM((n_chain, 1, 1), F32)],
        compiler_params=_cparams(("arbitrary",)),
        name="mlstm",
    )(gate_b, *args, *args)


def _mlstm_post_kernel(hf_ref, hb_ref, o_ref, g_ref, y_ref):
    h = hf_ref[0] + hb_ref[0]
    h = h * lax.rsqrt(jnp.mean(h * h, axis=-1, keepdims=True) + EPS) * g_ref[...]
    y_ref[0] = (h * _sigmoid(o_ref[0])).astype(y_ref.dtype)


def _mlstm_post(hf, hb, p, norm_g):
    B, T, W = hf.shape
    tm = _tile(T, 1024, 16)
    hspec = pl.BlockSpec((1, tm, ML_DV), lambda b, i, h: (b, i, h))
    return pl.pallas_call(
        _mlstm_post_kernel,
        grid=(B, T // tm, ML_HEADS),
        in_specs=[hspec, hspec,
                  pl.BlockSpec((1, tm, ML_DV), lambda b, i, h: (b, i, ML_O0 // ML_DV + h)),
                  pl.BlockSpec((1, ML_DV), lambda b, i, h: (0, h))],
        out_specs=hspec,
        out_shape=jax.ShapeDtypeStruct((B, T, W), BF16),
        compiler_params=_cparams(("parallel", "parallel", "parallel")),
        name="mlstm_post",
    )(hf, hb, p, norm_g.reshape(1, W))


def _rwkv_prep_kernel(x_ref, xp_ref, xn_ref, mu_ref, w0_ref, a0_ref, wdh_ref, wdl_ref, wa_ref, wg_ref,
                      r_ref, k_ref, v_ref, ld_ref, ar_ref, g_ref, *, tt, ctx_len, nt):
    i = pl.program_id(1)
    x = x_ref[0][:, :RW_COLS]
    first = (i == 0) | (i * tt == ctx_len)
    last = ((i + 1) * tt == ctx_len) | (i == nt - 1)
    prev_row = jnp.where(first, 0.0, xp_ref[0][7:8, :RW_COLS])
    next_row = jnp.where(last, 0.0, xn_ref[0][0:1, :RW_COLS])
    ridx = lax.broadcasted_iota(jnp.int32, (tt, 1), 0)
    prev = jnp.where(ridx == 0, prev_row, pltpu.roll(x, 1, 0))
    nxt = jnp.where(ridx == tt - 1, next_row, pltpu.roll(x, tt - 1, 0))
    xs = x + mu_ref[...] * (0.5 * (prev + nxt) - x)

    W = BRANCH_W
    r_ref[0] = xs[:, 0:W]
    k_ref[0] = xs[:, W:2 * W]
    v_ref[0] = xs[:, 2 * W:3 * W]
    wlo = _mm3(_split2(jnp.tanh(xs[:, 3 * W:3 * W + 256])), (wdh_ref[...], wdl_ref[...]))
    alo = _dot(xs[:, 3 * W + 128:3 * W + 384].astype(BF16), wa_ref[...])
    for d in range(2):
        ld_ref[d, 0] = -math.exp(-0.5) * _sigmoid(w0_ref[d:d + 1, :] + wlo[:, d * W:(d + 1) * W])
        ar_ref[d, 0] = _sigmoid(a0_ref[d:d + 1, :] + alo[:, d * W:(d + 1) * W])
    g_ref[0] = _dot(_sigmoid(xs[:, 3 * W + 384:3 * W + 640]).astype(BF16), wg_ref[...])


def _rwkv_prep(p, prm, ctx_len):
    B, T, _ = p.shape
    W = BRANCH_W
    tt = _tile(math.gcd(T, ctx_len), 256, 8)
    nt = T // tt
    h8 = tt // 8
    nb8 = T // 8
    z = lambda *s: jnp.zeros(s, F32)
    wd = jnp.concatenate([
        jnp.concatenate([prm['rw_w_up'][0], z(RW_LORA, W)], axis=1),
        jnp.concatenate([z(RW_LORA, W), prm['rw_w_up'][1]], axis=1),
        z(256 - 2 * RW_LORA, 2 * W)], axis=0)
    wa = jnp.concatenate([
        z(2 * RW_LORA - 128, 2 * W),
        jnp.concatenate([prm['rw_a_up'][0], z(RW_LORA, W)], axis=1),
        jnp.concatenate([z(RW_LORA, W), prm['rw_a_up'][1]], axis=1)], axis=0)
    wg = prm['rw_g_up'].astype(BF16)
    wdh, wdl = _split2(wd)
    wa = wa.astype(BF16)
    mu = prm['rw_mu'].reshape(1, RW_COLS)
    kern = functools.partial(_rwkv_prep_kernel, tt=tt, ctx_len=ctx_len, nt=nt)
    rwblk = RW_0 // RW_SPAN
    tok = jax.ShapeDtypeStruct((B, T, W), F32)
    tok2 = jax.ShapeDtypeStruct((2, B, T, W), F32)
    full = lambda shape: pl.BlockSpec(shape, lambda b, i: (0,) * len(shape))
    ospec = pl.BlockSpec((1, tt, W), lambda b, i: (b, i, 0))
    ospec2 = pl.BlockSpec((2, 1, tt, W), lambda b, i: (0, b, i, 0))
    return pl.pallas_call(
        kern,
        grid=(B, nt),
        in_specs=[pl.BlockSpec((1, tt, RW_SPAN), lambda b, i: (b, i, rwblk)),
                  pl.BlockSpec((1, 8, RW_SPAN), lambda b, i: (b, jnp.maximum(i * h8 - 1, 0), rwblk)),
                  pl.BlockSpec((1, 8, RW_SPAN), lambda b, i: (b, jnp.minimum((i + 1) * h8, nb8 - 1), rwblk)),
                  full((1, RW_COLS)), full((2, W)), full((2, W)),
                  full((256, 2 * W)), full((256, 2 * W)), full((256, 2 * W)), full((256, W))],
        out_specs=[ospec, ospec, ospec, ospec2, ospec2, ospec],
        out_shape=[tok, tok, tok, tok2, tok2, tok],
        compiler_params=_cparams(("parallel", "parallel")),
        name="rwkv_prep",
    )(p, p, p, mu, prm['rw_w0'], prm['rw_a0'], wdh, wdl, wa, wg)


def _split2(x):
    hi = x.astype(BF16)
    return hi, (x - hi.astype(F32)).astype(BF16)


def _split3(x):
    hi = x.astype(BF16)
    r1 = x - hi.astype(F32)
    mid = r1.astype(BF16)
    return hi, mid, (r1 - mid.astype(F32)).astype(BF16)


def _split1(x):
    return (x.astype(BF16),)


def _mm3(a, b, f=_dot):
    acc = f(a[0], b[0])
    if len(b) > 1:
        acc = acc + f(a[0], b[1])
    if len(a) > 1:
        acc = acc + f(a[1], b[0])
    return acc


def _mm_exact(e, parts, f=_dot):
    acc = f(e, parts[0])
    for x in parts[1:]:
        acc = acc + f(e, x)
    return acc


def _rwkv_scan_kernel(*refs, n_pairs):
    L = CHUNK
    P = 2 * RW_DH
    dir_refs = (refs[0:5], refs[5:10])
    kk_ref, ka_ref = refs[10:12]
    y_refs = refs[12:14]
    st_ref = refs[14]

    @pl.when(pl.program_id(0) == 0)
    def _():
        st_ref[...] = jnp.zeros_like(st_ref)

    row = lax.broadcasted_iota(jnp.int32, (L, P), 0)
    col = lax.broadcasted_iota(jnp.int32, (L, P), 1) % RW_DH
    before = (col < row, col > row)
    incl = (col <= row, col >= row)
    eye = (row == col).astype(F32)
    tri = tuple(m[:, :L].astype(BF16) for m in incl)
    lane = lax.broadcasted_iota(jnp.int32, (1, P), 1)
    m0 = lane < RW_DH
    r2 = lax.broadcasted_iota(jnp.int32, (P, P), 0) // RW_DH
    c2 = lax.broadcasted_iota(jnp.int32, (P, P), 1) // RW_DH
    same_head = r2 == c2

    def bd(x):
        z = jnp.zeros_like(x)
        return jnp.concatenate([jnp.where(m0, x, z), jnp.where(m0, z, x)], axis=0)

    def bd2(x):
        return tuple(bd(t) for t in x)

    sp = spi = _split1

    def cat(xs, axis):
        return tuple(jnp.concatenate(list(t), axis=axis) for t in zip(*xs))

    def st_load(s):
        sl = s['sl']
        r_ref, k_ref, v_ref, ld_ref, ar_ref = dir_refs[s['d']]
        b = s['b']
        s['R'], s['K'], s['V'] = r_ref[b, :, sl], k_ref[b, :, sl], v_ref[b, :, sl]
        s['LD'], s['AR'] = ld_ref[0, b, :, sl], ar_ref[0, b, :, sl]
        s['S0'] = st_ref[s['si']]
        kk = s['K'] * kk_ref[:, sl]
        kk2 = kk * kk
        n0 = jnp.sum(jnp.where(m0, kk2, 0.0), axis=1, keepdims=True)
        n1 = jnp.sum(jnp.where(m0, 0.0, kk2), axis=1, keepdims=True)
        s['kk'] = kk / jnp.maximum(jnp.sqrt(jnp.where(m0, n0, n1)), 1e-12)

    def st_cum(s):
        s['KD'] = s.pop('K') * (1.0 + (s['AR'] - 1.0) * ka_ref[:, s['sl']])
        s['BV'] = s['kk'] * s.pop('AR')
        s['c'] = _mm_exact(tri[s['d']], _split3(s['LD']))

    def st_x(s):
        c, LD = s.pop('c'), s.pop('LD')
        c_end = jnp.sum(LD, axis=0, keepdims=True)
        s['dcol'] = jnp.exp(jnp.broadcast_to(c_end, (P, P)).T)
        At = -s.pop('kk') * jnp.exp(c - LD)
        e_neg = jnp.exp(-c)
        BV, KD = s.pop('BV'), s.pop('KD')
        s['Rt'] = s.pop('R') * jnp.exp(c)
        e_end = jnp.exp(c_end - c)
        s['Bh_s'], s['Kh_s'] = sp(BV * e_end), sp(KD * e_end)
        s['At_s'], s['V_s'] = sp(At), sp(s.pop('V'))
        s['X'] = _mm3(cat([s['At_s'], sp(s['Rt'])], 0),
                      cat([bd2(sp(BV * e_neg)), bd2(sp(KD * e_neg))], 0), _dot_nt)

    def st_inv0(s):
        X = s.pop('X')
        bef, inc = before[s['d']], incl[s['d']]
        Nab = jnp.where(bef, X[:L, :P], 0.0)
        s['Nak_s'] = sp(jnp.where(bef, X[:L, P:], 0.0))
        s['Mrb_s'] = sp(jnp.where(inc, X[L:, :P], 0.0))
        s['Mrk_s'] = sp(jnp.where(inc, X[L:, P:], 0.0))
        s['T'] = eye + Nab
        Np_s = spi(Nab)
        s['Z'] = _mm3(Np_s, bd2(Np_s))
        s['NV'] = _mm3(s.pop('Nak_s'), bd2(s['V_s']))

    def st_inv1(s):
        Z = s.pop('Z')
        if Z.shape[0] == 2 * L:
            s['T'] = s['T'] + Z[L:]
        Np_s = spi(Z[:L])
        s['Z'] = _mm3(cat([Np_s, spi(s['T'])], 0), bd2(Np_s))

    def st_inv2(s):
        Z = s.pop('Z')
        s['T'] = s['T'] + Z[L:]
        s['Z'] = _mm3(spi(s['T']), bd2(spi(Z[:L])))

    def st_wu(s):
        T_s = sp(s.pop('T') + s.pop('Z'))
        s['WU'] = _mm3(T_s, cat([bd2(s.pop('At_s')), bd2(sp(s.pop('NV')))], 1))

    def st_loc(s):
        WU = s.pop('WU')
        W_s, Ul_s = sp(WU[:, :P]), sp(WU[:, P:])
        MM = _mm3(s.pop('Mrb_s'), cat([bd2(W_s), bd2(Ul_s)], 1))
        s['Q_s'] = sp(s.pop('Rt') + MM[:, :P])
        s['Yl'] = MM[:, P:] + _mm3(s.pop('Mrk_s'), bd2(s['V_s']))
        PS = _mm3(s.pop('Bh_s'), cat([W_s, Ul_s], 1), _dot_tn)
        s['Pm_s'] = sp(jnp.where(same_head, PS[:, :P], 0.0))
        s['Sl'] = jnp.where(same_head, PS[:, P:] + _mm3(s.pop('Kh_s'), s.pop('V_s'), _dot_tn), 0.0)

    def st_out(s):
        S0 = s.pop('S0')
        S0_s = sp(S0)
        y_refs[s['d']][s['b'], :, s['sl']] =_mm3(s.pop('Q_s'), S0_s) + s.pop('Yl')
        st_ref[s['si']] = s.pop('dcol') * S0 + _mm3(s.pop('Pm_s'), S0_s) + s.pop('Sl')

    n_dbl = int(math.log2(L)) - 2
    stages = [st_load, st_cum, st_x, st_inv0] + [st_inv1] * n_dbl + [st_inv2, st_wu, st_loc, st_out]
    n_b = y_refs[0].shape[0]
    states = [{'d': d, 'b': b, 'si': (d * n_b + b) * n_pairs + j, 'sl': slice(P * j, P * (j + 1))}
              for d in range(2) for b in range(n_b) for j in range(n_pairs)]
    for stage in stages:
        for s in states:
            stage(s)


def _rwkv_scan(r, k, v, ld, ar, k_k, k_a, ctx_len):
    B, T, W = r.shape
    L = CHUNK
    nch = T // L
    nctx = ctx_len // L
    n_pairs = W // (2 * RW_DH)

    def specs(d):
        tok = pl.BlockSpec((B, L, W), lambda s: (0, _chunk_index(d, s, nctx, nch), 0))
        per_dir = pl.BlockSpec((1, B, L, W), lambda s: (d, 0, _chunk_index(d, s, nctx, nch), 0))
        return [tok, tok, tok, per_dir, per_dir]

    def ospec(d):
        return pl.BlockSpec((B, L, W), lambda s: (0, _chunk_index(d, s, nctx, nch), 0))

    pspec = pl.BlockSpec((1, W), lambda s: (0, 0))
    yshape = jax.ShapeDtypeStruct((B, T, W), F32)
    kern = functools.partial(_rwkv_scan_kernel, n_pairs=n_pairs)
    args = (r, k, v, ld, ar)
    return pl.pallas_call(
        kern,
        grid=(nch,),
        in_specs=specs(0) + specs(1) + [pspec, pspec],
        out_specs=[ospec(0), ospec(1)],
        out_shape=[yshape, yshape],
        scratch_shapes=[pltpu.VMEM((2 * B * n_pairs, 2 * RW_DH, 2 * RW_DH), F32)],
        compiler_params=_cparams(("arbitrary",)),
        name="rwkv_scan",
    )(*args, *args, k_k.reshape(1, W), k_a.reshape(1, W))


def _rwkv_post_kernel(yf_ref, yb_ref, r_ref, k_ref, v_ref, g_ref, rk_ref, lw_ref, lb_ref, o_ref):
    P = 2 * RW_DH
    r2 = lax.broadcasted_iota(jnp.int32, (P, P), 0) // RW_DH
    c2 = lax.broadcasted_iota(jnp.int32, (P, P), 1) // RW_DH
    ones_bd = (r2 == c2).astype(BF16)

    def head_sum(x):
        return _mm_exact(ones_bd, _split2(x), lambda e, t: _dot(t, e))

    for j in range(o_ref.shape[2] // P):
        sl = slice(P * j, P * (j + 1))
        y = yf_ref[0, :, sl] + yb_ref[0, :, sl]
        yc = y - head_sum(y) * (1.0 / RW_DH)
        var = head_sum(yc * yc) * (1.0 / RW_DH)
        yn = yc * lax.rsqrt(var + RW_GN_EPS)
        bonus = head_sum(r_ref[0, :, sl] * k_ref[0, :, sl] * rk_ref[:, sl]) * v_ref[0, :, sl]
        o_ref[0, :, sl] = ((yn * lw_ref[:, sl] + lb_ref[:, sl] + bonus) * g_ref[0, :, sl]).astype(o_ref.dtype)


def _rwkv_post(y_f, y_b, r, k, v, g, r_k, ln_w, ln_b):
    B, T, W = y_f.shape
    tt = _tile(T, 512, 16)
    tspec = pl.BlockSpec((1, tt, W), lambda b, i: (b, i, 0))
    pspec = pl.BlockSpec((1, W), lambda b, i: (0, 0))
    return pl.pallas_call(
        _rwkv_post_kernel,
        grid=(B, T // tt),
        in_specs=[tspec, tspec, tspec, tspec, tspec, tspec, pspec, pspec, pspec],
        out_specs=tspec,
        out_shape=jax.ShapeDtypeStruct((B, T, W), BF16),
        compiler_params=_cparams(("parallel", "parallel")),
        name="rwkv_post",
    )(y_f, y_b, r, k, v, g, r_k.reshape(1, W), ln_w.reshape(1, W), ln_b.reshape(1, W))


def _rwkv_branch(p, prm, ctx_len):
    r, k, v, ld, ar, g = _rwkv_prep(p, prm, ctx_len)
    y_f, y_b = _rwkv_scan(r, k, v, ld, ar, prm['rw_k_k'], prm['rw_k_a'], ctx_len)
    return _rwkv_post(y_f, y_b, r, k, v, g, prm['rw_r_k'], prm['rw_ln_w'], prm['rw_ln_b'])


NA_RB = 4
NA_SLAB = NA_RB + NA_WIN_R - 1
NA_HB = 4


def _na_kernel(q_ref, k_ref, v_ref, bias_ref, o_ref, *, ctx_len, rows):
    rb = pl.program_id(2)
    scale = NA_DH ** -0.5
    heads = [slice(i * NA_DH, (i + 1) * NA_DH) for i in range(NA_HB)]

    @pl.when(rb == 0)
    def _():
        for hs in heads:
            s_ctx = _dot_nt(q_ref[0, :, hs].astype(BF16), k_ref[0, 0:ctx_len, hs].astype(BF16)) * scale
            p_ctx = jnp.exp(s_ctx - jnp.max(s_ctx, axis=-1, keepdims=True))
            y = _dot(p_ctx.astype(BF16), v_ref[0, 0:ctx_len, hs].astype(BF16))
            o_ref[0, :, hs] = (y / jnp.sum(p_ctx, axis=-1, keepdims=True)).astype(o_ref.dtype)

    @pl.when(rb > 0)
    def _():
        nslab = NA_SLAB * GRID_W
        start = jnp.clip((rb - 1) * NA_RB - NA_WIN_R // 2, 0, rows - NA_SLAB)
        koff = pl.multiple_of(ctx_len + start * GRID_W, GRID_W)

        def st_scores(s):
            hs = s['hs']
            q = q_ref[0, :, hs].astype(BF16)
            s['s_win'] = _dot_nt(q, k_ref[0, pl.ds(koff, nslab), hs].astype(BF16)) * scale + bias_ref[0, s['i'], 0]
            s['s_ctx'] = _dot_nt(q, k_ref[0, 0:ctx_len, hs].astype(BF16)) * scale

        def st_softmax(s):
            s_win, s_ctx = s.pop('s_win'), s.pop('s_ctx')
            m = jnp.maximum(jnp.max(s_win, axis=-1, keepdims=True), jnp.max(s_ctx, axis=-1, keepdims=True))
            p_win = jnp.exp(s_win - m)
            p_ctx = jnp.exp(s_ctx - m)
            s['den'] = jnp.sum(p_win, axis=-1, keepdims=True) + jnp.sum(p_ctx, axis=-1, keepdims=True)
            hs = s['hs']
            s['y'] = (_dot(p_win.astype(BF16), v_ref[0, pl.ds(koff, nslab), hs].astype(BF16))
                      + _dot(p_ctx.astype(BF16), v_ref[0, 0:ctx_len, hs].astype(BF16)))

        def st_out(s):
            o_ref[0, :, s['hs']] = (s.pop('y') / s.pop('den')).astype(o_ref.dtype)

        states = [{'i': i, 'hs': hs} for i, hs in enumerate(heads)]
        for stage in (st_scores, st_softmax, st_out):
            for s in states:
                stage(s)


def _na_block_type(lb, nblk):
    return jnp.where(lb <= 0, 0, jnp.where(lb == nblk - 1, 2, 1))


def _na_bias_kernel(toe_ref, o_ref, *, ro, row_ok):
    neg = jnp.full((GRID_W, GRID_W), NEG, F32)
    for t in range(ro.shape[0]):
        for ri in range(NA_RB):
            for kj in range(NA_SLAB):
                tile = toe_ref[0, 0, int(ro[t, ri, kj])] if row_ok[t, ri, kj] else neg
                o_ref[0, 0, t, ri * GRID_W:(ri + 1) * GRID_W, kj * GRID_W:(kj + 1) * GRID_W] = tile


def _na_bias_table(rpb, rows):
    depth, H = rpb.shape[:2]
    c = np.arange(GRID_W)
    c_start = np.clip(c - NA_WIN_C // 2, 0, GRID_W - NA_WIN_C)
    col_ok = (c[None, :] >= c_start[:, None]) & (c[None, :] < c_start[:, None] + NA_WIN_C)
    col_off = np.clip(c[None, :] - c[:, None], 1 - NA_WIN_C, NA_WIN_C - 1) + (NA_WIN_C - 1)
    sel_c = jnp.asarray(col_off[..., None] == np.arange(2 * NA_WIN_C - 1), F32)
    toe = jnp.einsum('lhoc,qkc->lhoqk', rpb, sel_c, precision=HI) + jnp.asarray(np.where(col_ok, 0.0, NEG), F32)
    nblk = rows // NA_RB
    reps = np.array([0, min(1, nblk - 1), nblk - 1])
    r = reps[:, None] * NA_RB + np.arange(NA_RB)[None, :]
    slab0 = np.clip(reps * NA_RB - NA_WIN_R // 2, 0, rows - NA_SLAB)
    kr = slab0[:, None] + np.arange(NA_SLAB)[None, :]
    win0 = np.clip(r - NA_WIN_R // 2, 0, rows - NA_WIN_R)
    ro = kr[:, None, :] - r[:, :, None] + (NA_WIN_R - 1)
    row_ok = (kr[:, None, :] >= win0[:, :, None]) & (kr[:, None, :] < win0[:, :, None] + NA_WIN_R)
    kern = functools.partial(_na_bias_kernel, ro=ro, row_ok=row_ok)
    nq, nk = NA_RB * GRID_W, NA_SLAB * GRID_W
    return pl.pallas_call(
        kern,
        grid=(depth, H),
        in_specs=[pl.BlockSpec((1, 1, 2 * NA_WIN_R - 1, GRID_W, GRID_W), lambda l, h: (l, h, 0, 0, 0))],
        out_specs=pl.BlockSpec((1, 1, 3, nq, nk), lambda l, h: (l, h, 0, 0, 0)),
        out_shape=jax.ShapeDtypeStruct((depth, H, 3, nq, nk), F32),
        compiler_params=_cparams(("parallel", "parallel")),
        name="na_bias",
    )(toe)


def _na_branch(p, table, l, ctx_len):
    B, T, _ = p.shape
    seq = T - ctx_len
    rows = seq // GRID_W
    H = NA_HEADS
    nq = NA_RB * GRID_W
    assert ctx_len == nq and rows % NA_RB == 0 and rows >= NA_SLAB, (ctx_len, rows)
    nblk = rows // NA_RB
    wb = NA_HB * NA_DH
    qb, kb, vb = NA_0 // wb, (NA_0 + BRANCH_W) // wb, (NA_0 + 2 * BRANCH_W) // wb
    kern = functools.partial(_na_kernel, ctx_len=ctx_len, rows=rows)
    return pl.pallas_call(
        kern,
        grid=(B, H // NA_HB, 1 + nblk),
        in_specs=[pl.BlockSpec((1, nq, wb), lambda b, h, r: (b, r, qb + h)),
                  pl.BlockSpec((1, T, wb), lambda b, h, r: (b, 0, kb + h)),
                  pl.BlockSpec((1, T, wb), lambda b, h, r: (b, 0, vb + h)),
                  pl.BlockSpec((1, NA_HB, 1, nq, NA_SLAB * GRID_W),
                               lambda b, h, r: (l, h, _na_block_type(r - 1, nblk), 0, 0))],
        out_specs=pl.BlockSpec((1, nq, wb), lambda b, h, r: (b, r, h)),
        out_shape=jax.ShapeDtypeStruct((B, T, BRANCH_W), BF16),
        compiler_params=_cparams(("parallel", "parallel", "arbitrary")),
        name="na",
    )(p, p, p, table)


def _pack_kernel(w_ref, o_ref):
    ml, rw, na, gt = 3088, 3712, 3072, 6144
    o_rw, o_na, o_gt = ml, ml + rw, ml + rw + na
    w = w_ref[0]
    n = w.shape[0]
    o_ref[0, :, 0:NA_0] = w[:, 0:NA_0].astype(BF16)
    o_ref[0, :, NA_0:GATE_0] = w[:, o_na:o_na + na].astype(BF16)
    o_ref[0, :, GATE_0:RW_0] = w[:, o_gt:o_gt + gt].astype(BF16)
    o_ref[0, :, RW_0:MLG_0] = w[:, o_rw:o_rw + rw].astype(BF16)
    tail = jnp.concatenate([w[:, NA_0:ml], jnp.zeros((n, 128 - (ml - NA_0)), F32)], axis=1)
    o_ref[0, :, MLG_0:MLG_0 + 128] = tail.astype(BF16)
    o_ref[0, :, MLG_0 + 128:] = jnp.zeros((n, P_COLS - MLG_0 - 128), BF16)


def _pack_w_in(w_in, l):
    _, D, N = w_in.shape
    tr = _tile(D, 64, 16)
    return pl.pallas_call(
        _pack_kernel,
        grid=(D // tr,),
        in_specs=[pl.BlockSpec((1, tr, N), lambda i: (l, i, 0))],
        out_specs=pl.BlockSpec((1, tr, P_COLS), lambda i: (0, i, 0)),
        out_shape=jax.ShapeDtypeStruct((1, D, P_COLS), BF16),
        compiler_params=_cparams(("parallel",)),
        name="pack_w_in",
    )(w_in)


def _rope_tables(seq, ctx_len):
    t = jnp.arange(seq)
    nf = ML_DQK // 4
    freqs = ROPE_BASE ** (-jnp.arange(nf, dtype=F32) / nf)
    ang_r = (t // GRID_W).astype(F32)[:, None] * freqs[None, :]
    ang_c = (t % GRID_W).astype(F32)[:, None] * freqs[None, :]
    cos = jnp.concatenate([jnp.cos(ang_r)] * 2 + [jnp.cos(ang_c)] * 2, axis=1)
    sin = jnp.concatenate([-jnp.sin(ang_r), jnp.sin(ang_r), -jnp.sin(ang_c), jnp.sin(ang_c)], axis=1)
    cos = jnp.concatenate([jnp.ones((ctx_len, ML_DQK), F32), cos], axis=0)
    sin = jnp.concatenate([jnp.zeros((ctx_len, ML_DQK), F32), sin], axis=0)
    return cos, sin


def kernel(x, c, ctx, c_ctx, w_mod, b_mod, norm1_g, w_in, ml_gate_b, ml_norm_g, rw_mu, rw_w0, rw_w_up, rw_a0, rw_a_up, rw_g_up, rw_k_k, rw_k_a, rw_r_k, rw_ln_w, rw_ln_b, na_rpb, w_branch, w_out, norm2_g, w_ff1, w_ff2, final_g):
    B, seq, D = x.shape
    ctx_len = ctx.shape[1]
    depth = w_mod.shape[0]
    xs = jnp.concatenate([ctx, x], axis=1)
    s_all = jax.nn.silu(jnp.concatenate([c_ctx[None, :], c], axis=0))
    s_all = jnp.concatenate([s_all, jnp.zeros((8 - (B + 1) % 8, D), F32)], axis=0) if (B + 1) % 8 else s_all
    cos, sin = _rope_tables(seq, ctx_len)
    na_table = _na_bias_table(na_rpb, seq // GRID_W)
    wb_bf, wo_bf, w2_bf = (w.astype(BF16) for w in (w_branch, w_out, w_ff2))
    for l in range(depth):
        mod = _modulation(s_all, w_mod, b_mod, l).reshape(-1, 6, D)
        mc = jnp.broadcast_to(mod[0:1], (B, 6, D))
        ml_ = mod[1:B + 1]
        mod1 = jnp.stack([mc[:, 0], mc[:, 1], ml_[:, 0], ml_[:, 1]], axis=1)
        gate1 = jnp.stack([mc[:, 2], ml_[:, 2]], axis=1)
        mod2 = jnp.stack([mc[:, 3], mc[:, 4], ml_[:, 3], ml_[:, 4]], axis=1)
        gate2 = jnp.stack([mc[:, 5], ml_[:, 5]], axis=1)
        prm = {'rw_mu': rw_mu[l], 'rw_w0': rw_w0[l], 'rw_w_up': rw_w_up[l], 'rw_a0': rw_a0[l],
               'rw_a_up': rw_a_up[l], 'rw_g_up': rw_g_up[l], 'rw_k_k': rw_k_k[l], 'rw_k_a': rw_k_a[l],
               'rw_r_k': rw_r_k[l], 'rw_ln_w': rw_ln_w[l], 'rw_ln_b': rw_ln_b[l]}

        p = _mm(_norm_mod(xs, norm1_g[l], mod1, ctx_len), _pack_w_in(w_in, l), 0, F32, False)
        h_f, h_b = _mlstm(p, ml_gate_b[l], cos, sin, ctx_len)
        y_m = _mlstm_post(h_f, h_b, p, ml_norm_g[l])
        y_r = _rwkv_branch(p, prm, ctx_len)
        y_n = _na_branch(p, na_table, l, ctx_len)
        merged = _merge(y_m, y_r, y_n, p, wb_bf, l)
        xs, h2 = _mm_res_norm(merged, wo_bf, l, xs, gate1, norm2_g[l], mod2, ctx_len)
        hff = _mm(h2, w_ff1, l, BF16, True)
        xs = _mm_res(hff, w2_bf, l, xs, gate2, ctx_len)
    return _final_norm(xs, final_g, ctx_len)
```
